```python
import jax, jax.numpy as jnp
from jax import lax
import numpy as np

D_MODEL = 1024
BATCH = 4
SEQ = 4096
DEPTH = 1
DEC_BATCH = 128
DEC_SEQ = 8
PAST_LEN = 8192
PAGE_SIZE = 128

HEAD_DIM = 64
N_Q_HEADS = 8
N_KV_HEADS = 2
GQA_GROUP = N_Q_HEADS // N_KV_HEADS
ATTN_WIDTH = N_Q_HEADS * HEAD_DIM
WINDOW = 128
ATTN_BLOCK = WINDOW
ROPE_THETA = 10000.0
N_GMLP_HEADS = 8
GMLP_HEAD_DIM = 64
GMLP_WIDTH = N_GMLP_HEADS * GMLP_HEAD_DIM
CHUNK = 128
MIX_WIDTH = ATTN_WIDTH + GMLP_WIDTH
Q_END = ATTN_WIDTH
K_END = Q_END + N_KV_HEADS * HEAD_DIM
V_END = K_END + N_KV_HEADS * HEAD_DIM
GU_END = V_END + GMLP_WIDTH
IN_WIDTH = GU_END + GMLP_WIDTH
N_MEM = 256
MEM_HEADS = 4
MEM_HEAD_DIM = 64
MEM_WIDTH = MEM_HEADS * MEM_HEAD_DIM
PEER_HEADS = 8
N_KEYS = 128
N_EXPERTS = N_KEYS * N_KEYS
PEER_TOPK = 16
D_KEY = 256
D_HALF = D_KEY // 2
PEER_BLOCK = 128
EPS = 1e-6
NEG_INF = -1e30

kernel_name = 'hymba_swa_sgu_peer_decode_step'


def rms_norm(x, g):
    xf = x.astype(jnp.float32)
    y = xf * lax.rsqrt(jnp.mean(xf * xf, axis=-1, keepdims=True) + EPS)
    return (y * g.astype(jnp.float32)).astype(x.dtype)


def rope(x, pos):
    half = HEAD_DIM // 2
    inv = ROPE_THETA ** (-jnp.arange(half, dtype=jnp.float32) / half)
    ang = pos.astype(jnp.float32)[:, None] * inv[None, :]
    cos = jnp.cos(ang)[:, None, :]
    sin = jnp.sin(ang)[:, None, :]
    x1 = x[..., :half].astype(jnp.float32)
    x2 = x[..., half:].astype(jnp.float32)
    return jnp.concatenate([x1 * cos - x2 * sin, x2 * cos + x1 * sin], axis=-1).astype(x.dtype)


def sink_attention(q, k, v, qpos, kpos, sinks):
    s = jnp.einsum('...qhgd,...khd->...hgqk', q, k).astype(jnp.float32) * (HEAD_DIM ** -0.5)
    diff = qpos[..., :, None] - kpos[..., None, :]
    valid = (kpos[..., None, :] >= 0) & (diff >= 0) & (diff < WINDOW)
    s = jnp.where(valid[..., None, None, :, :], s, NEG_INF)
    sink = sinks.astype(jnp.float32)[:, :, None, None]
    m = jnp.maximum(jnp.max(s, axis=-1, keepdims=True), sink)
    p = jnp.exp(s - m)
    probs = p / (jnp.sum(p, axis=-1, keepdims=True) + jnp.exp(sink - m))
    return jnp.einsum('...hgqk,...khd->...qhgd', probs.astype(v.dtype), v)


def swa_prompt(q, k, v, sinks):
    B, S = q.shape[0], q.shape[1]
    nb = S // ATTN_BLOCK
    qb = q.reshape(B, nb, ATTN_BLOCK, N_KV_HEADS, GQA_GROUP, HEAD_DIM)
    kb = k.reshape(B, nb, ATTN_BLOCK, N_KV_HEADS, HEAD_DIM)
    vb = v.reshape(B, nb, ATTN_BLOCK, N_KV_HEADS, HEAD_DIM)
    pad = ((0, 0), (1, 0), (0, 0), (0, 0), (0, 0))
    k_band = jnp.concatenate([jnp.pad(kb[:, :-1], pad), kb], axis=2)
    v_band = jnp.concatenate([jnp.pad(vb[:, :-1], pad), vb], axis=2)
    starts = jnp.arange(nb, dtype=jnp.int32) * ATTN_BLOCK
    qpos = starts[:, None] + jnp.arange(ATTN_BLOCK, dtype=jnp.int32)[None, :]
    kpos = starts[:, None] - ATTN_BLOCK + jnp.arange(2 * ATTN_BLOCK, dtype=jnp.int32)[None, :]
    o = sink_attention(qb, k_band, v_band, qpos, kpos, sinks.reshape(N_KV_HEADS, GQA_GROUP))
    return o.reshape(B, S, ATTN_WIDTH)


def swa_sample(q, k, v, k_buf, v_buf, sinks):
    DB, T = q.shape[0], q.shape[1]
    W = k_buf.shape[1]
    qg = q.reshape(DB, T, N_KV_HEADS, GQA_GROUP, HEAD_DIM)
    kc = jnp.concatenate([k_buf, k], axis=1)
    vc = jnp.concatenate([v_buf, v], axis=1)
    qpos = PAST_LEN + jnp.arange(T, dtype=jnp.int32)
    kpos = PAST_LEN - W + jnp.arange(W + T, dtype=jnp.int32)
    o = sink_attention(qg, kc, vc, qpos, kpos, sinks.reshape(N_KV_HEADS, GQA_GROUP))
    return o.reshape(DB, T, ATTN_WIDTH), kc[:, T:], vc[:, T:]


def spatial_gate(u, v, w_s, b_s):
    L = u.shape[-3]
    w = jnp.tril(w_s[:, :L, :L]).astype(v.dtype)
    mixed = jnp.einsum('htr,...rhd->...thd', w, v) + b_s[:, :L].T[:, :, None].astype(v.dtype)
    return u * mixed


def token_mixer(xn, pos, w_in, sinks, g_sgu, w_s, b_s, g_attn_out, g_gmlp_out, w_out,
                k_buf=None, v_buf=None):
    B, T = xn.shape[0], xn.shape[1]
    proj = xn @ w_in
    q, k, v, gu, gv = jnp.split(proj, [Q_END, K_END, V_END, GU_END], axis=-1)
    q = rope(q.reshape(B, T, N_Q_HEADS, HEAD_DIM), pos)
    k = rope(k.reshape(B, T, N_KV_HEADS, HEAD_DIM), pos)
    v = v.reshape(B, T, N_KV_HEADS, HEAD_DIM)
    gu = jax.nn.gelu(gu, approximate=False).reshape(B, T, N_GMLP_HEADS, GMLP_HEAD_DIM)
    gv = rms_norm(jax.nn.gelu(gv, approximate=False).reshape(B, T, N_GMLP_HEADS, GMLP_HEAD_DIM), g_sgu)
    if k_buf is None:
        attn = swa_prompt(q, k, v, sinks)
        keep = min(WINDOW, T)
        new_k, new_v = k[:, T - keep:], v[:, T - keep:]
        nc = T // CHUNK
        sgu = spatial_gate(gu.reshape(B, nc, CHUNK, N_GMLP_HEADS, GMLP_HEAD_DIM),
                           gv.reshape(B, nc, CHUNK, N_GMLP_HEADS, GMLP_HEAD_DIM), w_s, b_s)
    else:
        attn, new_k, new_v = swa_sample(q, k, v, k_buf, v_buf, sinks)
        sgu = spatial_gate(gu, gv, w_s, b_s)
    sgu = sgu.reshape(B, T, GMLP_WIDTH)
    merged = jnp.concatenate([rms_norm(attn, g_attn_out), rms_norm(sgu, g_gmlp_out)], axis=-1)
    return merged @ w_out, new_k, new_v, gv


def memory_kv(mem, g_mem, w_mk, w_mv):
    B, M = mem.shape[0], mem.shape[1]
    mn = rms_norm(mem, g_mem)
    return ((mn @ w_mk).reshape(B, M, MEM_HEADS, MEM_HEAD_DIM),
            (mn @ w_mv).reshape(B, M, MEM_HEADS, MEM_HEAD_DIM))


def memory_attention(xn, mem_k, mem_v, w_cq, w_co):
    B, T = xn.shape[0], xn.shape[1]
    q = (xn @ w_cq).reshape(B, T, MEM_HEADS, MEM_HEAD_DIM)
    s = jnp.einsum('bqhd,bkhd->bhqk', q, mem_k).astype(jnp.float32) * (MEM_HEAD_DIM ** -0.5)
    p = jax.nn.softmax(s, axis=-1).astype(mem_v.dtype)
    o = jnp.einsum('bhqk,bkhd->bqhd', p, mem_v).reshape(B, T, MEM_WIDTH)
    return o @ w_co


def peer_ffn(xn, w_pq, sub_keys, u_tab, v_tab):
    lead = xn.shape[:-1]
    xf = xn.reshape(-1, D_MODEL)
    T = xf.shape[0]
    pad = (-T) % PEER_BLOCK
    xb = jnp.pad(xf, ((0, pad), (0, 0))).reshape(-1, PEER_BLOCK, D_MODEL)

    def block(xt):
        q = (xt @ w_pq).reshape(PEER_BLOCK, PEER_HEADS, 2, D_HALF)
        s = jnp.einsum('thcd,hckd->thck', q, sub_keys).astype(jnp.float32)
        s1, i1 = lax.top_k(s[:, :, 0], PEER_TOPK)
        s2, i2 = lax.top_k(s[:, :, 1], PEER_TOPK)
        cand = (s1[..., :, None] + s2[..., None, :]).reshape(PEER_BLOCK, PEER_HEADS, PEER_TOPK * PEER_TOPK)
        cidx = (i1[..., :, None] * N_KEYS + i2[..., None, :]).reshape(PEER_BLOCK, PEER_HEADS, PEER_TOPK * PEER_TOPK)
        top, sel = lax.top_k(cand, PEER_TOPK)
        idx = jnp.take_along_axis(cidx, sel, axis=-1)
        gate = jax.nn.softmax(top, axis=-1)
        h = jnp.einsum('td,thkd->thk', xt, u_tab[idx]).astype(jnp.float32)
        a = (jax.nn.gelu(h, approximate=False) * gate).astype(xt.dtype)
        return jnp.einsum('thk,thkd->td', a, v_tab[idx])

    out = lax.map(block, xb).reshape(-1, D_MODEL)[:T]
    return out.reshape(*lead, D_MODEL)


def setup_inputs(seed: int = 0) -> dict:
    key = jax.random.key(seed)
    ks = jax.random.split(key, 32)
    f32 = jnp.float32

    def nrm(k, shape, scale):
        return jax.random.normal(k, shape, f32) * scale

    def gain(k, shape):
        return 1.0 + 0.05 * jax.random.normal(k, shape, f32)

    swa_buf = min(WINDOW, PAST_LEN)
    L = DEPTH
    return {
        'x_prompt': nrm(ks[0], (BATCH, SEQ, D_MODEL), 1.0),
        'x_sample': nrm(ks[1], (DEC_BATCH, DEC_SEQ, D_MODEL), 1.0),
        'mem_prompt': nrm(ks[2], (BATCH, N_MEM, D_MODEL), 1.0),
        'cache_swa_k': nrm(ks[3], (L, DEC_BATCH, swa_buf, N_KV_HEADS, HEAD_DIM), 1.0),
        'cache_swa_v': nrm(ks[4], (L, DEC_BATCH, swa_buf, N_KV_HEADS, HEAD_DIM), 1.0),
        'cache_mem_k': nrm(ks[5], (L, DEC_BATCH, N_MEM, MEM_HEADS, MEM_HEAD_DIM), 1.0),
        'cache_mem_v': nrm(ks[6], (L, DEC_BATCH, N_MEM, MEM_HEADS, MEM_HEAD_DIM), 1.0),
        'g_mix': gain(ks[7], (L, D_MODEL)),
        'w_in': nrm(ks[8], (L, D_MODEL, IN_WIDTH), D_MODEL ** -0.5),
        'attn_sinks': nrm(ks[9], (L, N_Q_HEADS), 0.5),
        'g_sgu': gain(ks[10], (L, N_GMLP_HEADS, GMLP_HEAD_DIM)),
        'w_spatial': nrm(ks[11], (L, N_GMLP_HEADS, CHUNK, CHUNK), CHUNK ** -0.5),
        'b_spatial': gain(ks[12], (L, N_GMLP_HEADS, CHUNK)),
        'g_attn_out': gain(ks[13], (L, ATTN_WIDTH)),
        'g_gmlp_out': gain(ks[14], (L, GMLP_WIDTH)),
        'w_out': nrm(ks[15], (L, MIX_WIDTH, D_MODEL), MIX_WIDTH ** -0.5),
        'g_cross': gain(ks[16], (L, D_MODEL)),
        'g_mem': gain(ks[17], (L, D_MODEL)),
        'w_cq': nrm(ks[18], (L, D_MODEL, MEM_WIDTH), D_MODEL ** -0.5),
        'w_mk': nrm(ks[19], (L, D_MODEL, MEM_WIDTH), D_MODEL ** -0.5),
        'w_mv': nrm(ks[20], (L, D_MODEL, MEM_WIDTH), D_MODEL ** -0.5),
        'w_co': nrm(ks[21], (L, MEM_WIDTH, D_MODEL), MEM_WIDTH ** -0.5),
        'g_ffn': gain(ks[22], (L, D_MODEL)),
        'w_peer_q': nrm(ks[23], (L, D_MODEL, PEER_HEADS * D_KEY), D_MODEL ** -0.5),
        'peer_sub_keys': nrm(ks[24], (L, PEER_HEADS, 2, N_KEYS, D_HALF), D_HALF ** -0.5),
        'peer_u': nrm(ks[25], (L, N_EXPERTS, D_MODEL), D_MODEL ** -0.5),
        'peer_v': nrm(ks[26], (L, N_EXPERTS, D_MODEL), PEER_HEADS ** -0.5),
        'g_final': gain(ks[27], (D_MODEL,)),
    }


def reference(x_prompt, x_sample, mem_prompt, cache_swa_k, cache_swa_v, cache_mem_k, cache_mem_v,
              g_mix, w_in, attn_sinks, g_sgu, w_spatial, b_spatial, g_attn_out, g_gmlp_out, w_out,
              g_cross, g_mem, w_cq, w_mk, w_mv, w_co, g_ffn, w_peer_q, peer_sub_keys, peer_u, peer_v,
              g_final):
    pos_p = jnp.arange(x_prompt.shape[1], dtype=jnp.int32)
    pos_s = PAST_LEN + jnp.arange(x_sample.shape[1], dtype=jnp.int32)
    xp, xs = x_prompt, x_sample
    swa_k_p, swa_v_p, mem_k_p, mem_v_p = [], [], [], []
    swa_k_s, swa_v_s, sgu_v_s = [], [], []
    for l in range(DEPTH):
        mix_w = (w_in[l], attn_sinks[l], g_sgu[l], w_spatial[l], b_spatial[l],
                 g_attn_out[l], g_gmlp_out[l], w_out[l])
        mp, kp, vp, _ = token_mixer(rms_norm(xp, g_mix[l]), pos_p, *mix_w)
        ms, ks_, vs_, gvs = token_mixer(rms_norm(xs, g_mix[l]), pos_s, *mix_w,
                                        cache_swa_k[l], cache_swa_v[l])
        xp = xp + mp
        xs = xs + ms
        mk, mv = memory_kv(mem_prompt, g_mem[l], w_mk[l], w_mv[l])
        xp = xp + memory_attention(rms_norm(xp, g_cross[l]), mk, mv, w_cq[l], w_co[l])
        xs = xs + memory_attention(rms_norm(xs, g_cross[l]), cache_mem_k[l], cache_mem_v[l],
                                   w_cq[l], w_co[l])
        peer_w = (w_peer_q[l], peer_sub_keys[l], peer_u[l], peer_v[l])
        xp = xp + peer_ffn(rms_norm(xp, g_ffn[l]), *peer_w)
        xs = xs + peer_ffn(rms_norm(xs, g_ffn[l]), *peer_w)
        swa_k_p.append(kp)
        swa_v_p.append(vp)
        mem_k_p.append(mk)
        mem_v_p.append(mv)
        swa_k_s.append(ks_)
        swa_v_s.append(vs_)
        sgu_v_s.append(gvs)
    y_prompt = rms_norm(xp, g_final)
    y_sample = rms_norm(xs, g_final)
    return (y_prompt, y_sample,
            jnp.stack(swa_k_p), jnp.stack(swa_v_p), jnp.stack(mem_k_p), jnp.stack(mem_v_p),
            jnp.stack(swa_k_s), jnp.stack(swa_v_s), jnp.stack(sgu_v_s))
```

```python
import functools

import jax
import jax.numpy as jnp
import numpy as np
from jax import lax
from jax.experimental import pallas as pl
from jax.experimental.pallas import tpu as pltpu

F32 = jnp.float32
BF16 = jnp.bfloat16
I32 = jnp.int32

LANES = 128
SUBLANES = 8
VMEM_LIMIT_BYTES = 56 * 1024 * 1024

HEAD_DIM = 64
N_Q_HEADS = 8
N_KV_HEADS = 2
ATTN_WIDTH = N_Q_HEADS * HEAD_DIM
WINDOW = 128
ROPE_THETA = 10000.0
N_GMLP_HEADS = 8
GMLP_WIDTH = N_GMLP_HEADS * HEAD_DIM
CHUNK = 128
MEM_HEADS = 4
MEM_HEAD_DIM = 64
PEER_HEADS = 8
N_KEYS = 128
PEER_TOPK = 16
D_HALF = 128
EPS = 1e-6
NEG_INF = -1e30
SQRT_HALF = float(np.sqrt(0.5))

GATE_PITCH = N_KEYS + SUBLANES

NT_DIMS = (((1,), (1,)), ((), ()))


def _params(*semantics):
    return pltpu.CompilerParams(dimension_semantics=semantics, vmem_limit_bytes=VMEM_LIMIT_BYTES)


def _full(shape):
    zeros = (0,) * len(shape)
    return pl.BlockSpec(shape, lambda *_: zeros)


def _rms(x, g):
    return x * lax.rsqrt(jnp.mean(x * x, axis=-1, keepdims=True) + EPS) * g


def _gelu(x):
    return 0.5 * x * (1.0 + lax.erf(x * SQRT_HALF))


def _dot(a, b):
    return jnp.dot(a, b, preferred_element_type=F32)


def _dot_nt(a, b):
    return lax.dot_general(a, b, NT_DIMS, preferred_element_type=F32)


def _split(x):
    hi = x.astype(BF16)
    return hi, (x - hi.astype(F32)).astype(BF16)


def _half_masks():
    lane = lax.broadcasted_iota(I32, (1, LANES), 1)
    low = lane < HEAD_DIM
    return low, low.astype(F32), 1.0 - low.astype(F32)


def _mix_in_kernel(x_ref, g_ref, w_ref, cos_ref, sin_ref, gs_ref, seg_ref,
                   q_ref, k_ref, v_ref, gu_ref, gv_ref):
    xn = _rms(x_ref[...], g_ref[...]).astype(BF16)
    proj = _dot(xn, w_ref[...])
    cosf = cos_ref[...]
    sinf = sin_ref[...]
    lane = lax.broadcasted_iota(I32, (1, LANES), 1)
    first_half = (lane & (HEAD_DIM - 1)) < HEAD_DIM // 2

    def rope(c):
        partner = jnp.where(first_half, pltpu.roll(c, LANES - HEAD_DIM // 2, 1),
                            pltpu.roll(c, HEAD_DIM // 2, 1))
        return c * cosf + partner * sinf

    for c in range(ATTN_WIDTH // LANES):
        sl = slice(c * LANES, (c + 1) * LANES)
        q_ref[:, sl] = rope(proj[:, sl]) * (HEAD_DIM ** -0.5)
    k_ref[...] = rope(proj[:, 512:640])
    v_ref[...] = proj[:, 640:768]
    gu_ref[...] = _gelu(proj[:, 768:1280])
    gv = _gelu(proj[:, 1280:1792])
    hi, lo = _split(gv * gv)
    seg = seg_ref[...]
    ms = _dot(hi, seg) + _dot(lo, seg)
    gv_ref[...] = gv * lax.rsqrt(ms + EPS) * gs_ref[...]


def _mix_in(x, g, w_in, cosf, sinf, gs, seg, *, rows, table_blocks):
    t = x.shape[0]
    d = x.shape[1]
    row = lambda width: pl.BlockSpec((rows, width), lambda i: (i, 0))
    tab = pl.BlockSpec((rows, LANES), lambda i: (i % table_blocks, 0))
    return pl.pallas_call(
        _mix_in_kernel,
        grid=(t // rows,),
        in_specs=[row(d), _full(g.shape), _full(w_in.shape), tab, tab, _full(gs.shape), _full(seg.shape)],
        out_specs=[row(512), row(128), row(128), row(512), row(512)],
        out_shape=[jax.ShapeDtypeStruct((t, 512), F32), jax.ShapeDtypeStruct((t, 128), F32),
                   jax.ShapeDtypeStruct((t, 128), F32), jax.ShapeDtypeStruct((t, 512), F32),
                   jax.ShapeDtypeStruct((t, 512), F32)],
        compiler_params=_params("parallel"),
        name="mix_in",
    )(x, g, w_in, cosf, sinf, gs, seg)


def _stack_heads(q, lowf, highf):
    parts = [(q[:, c * LANES:(c + 1) * LANES] * m).astype(BF16)
             for m in (lowf, highf) for c in range(4)]
    return jnp.concatenate(parts, axis=0)


def _sink_softmax(s, sink):
    m = jnp.maximum(jnp.max(s, axis=-1, keepdims=True), sink)
    p = jnp.exp(s - m)
    den = jnp.sum(p, axis=-1, keepdims=True) + jnp.exp(sink - m)
    return (p / den).astype(BF16)


def _merge_out(attn, sgu, ga, gg, wo, x):
    merged = jnp.concatenate([_rms(attn, ga), _rms(sgu, gg)], axis=1).astype(BF16)
    return x + _dot(merged, wo)


def _mix_prompt_kernel(sinks_ref, q_ref, kp_ref, kc_ref, vp_ref, vc_ref, gu_ref, gv_ref, x_ref,
                       ws_ref, bias_ref, ga_ref, gg_ref, wo_ref, o_ref):
    n = pl.program_id(1)
    low, lowf, highf = _half_masks()
    blk = WINDOW
    qs = _stack_heads(q_ref[...], lowf, highf)
    kb = jnp.concatenate([kp_ref[...], kc_ref[...]], axis=0).astype(BF16)
    vb = jnp.concatenate([vp_ref[...], vc_ref[...]], axis=0).astype(BF16)
    s_all = _dot_nt(qs, kb)
    i = lax.broadcasted_iota(I32, (blk, 2 * blk), 0)
    j = lax.broadcasted_iota(I32, (blk, 2 * blk), 1)
    first_key = jnp.where(n > 0, 0, blk)
    valid = (j > i) & (j <= i + blk) & (j >= first_key)
    probs = []
    for h in range(N_Q_HEADS):
        s = jnp.where(valid, s_all[h * blk:(h + 1) * blk], NEG_INF)
        probs.append(_sink_softmax(s, sinks_ref[h]))
    o_all = _dot(jnp.concatenate(probs, axis=0), vb)
    attn = jnp.concatenate(
        [jnp.where(low, o_all[c * blk:(c + 1) * blk], o_all[(4 + c) * blk:(5 + c) * blk])
         for c in range(4)], axis=1)

    gvb = gv_ref[...].astype(BF16)
    r = lax.broadcasted_iota(I32, (CHUNK, CHUNK), 0)
    c_ = lax.broadcasted_iota(I32, (CHUNK, CHUNK), 1)
    tril = r >= c_
    mixed = []
    for c in range(4):
        g = gvb[:, c * LANES:(c + 1) * LANES]
        y0 = _dot(jnp.where(tril, ws_ref[2 * c], 0.0).astype(BF16), g)
        y1 = _dot(jnp.where(tril, ws_ref[2 * c + 1], 0.0).astype(BF16), g)
        mixed.append(jnp.where(low, y0, y1))
    sgu = gu_ref[...] * (jnp.concatenate(mixed, axis=1) + bias_ref[...])
    o_ref[...] = _merge_out(attn, sgu, ga_ref[...], gg_ref[...], wo_ref[...], x_ref[...])


def _mix_prompt(sinks, q, k, v, gu, gv, x, ws, bias, ga, gg, wo, *, batch, seq):
    nb = seq // WINDOW
    cur = lambda width: pl.BlockSpec((WINDOW, width), lambda b, n: (b * nb + n, 0))
    prev = lambda width: pl.BlockSpec((WINDOW, width), lambda b, n: (b * nb + jnp.maximum(n - 1, 0), 0))
    return pl.pallas_call(
        _mix_prompt_kernel,
        grid=(batch, nb),
        in_specs=[pl.BlockSpec(memory_space=pltpu.SMEM),
                  cur(512), prev(128), cur(128), prev(128), cur(128), cur(512), cur(512), cur(x.shape[1]),
                  _full(ws.shape), _full(bias.shape), _full(ga.shape), _full(gg.shape), _full(wo.shape)],
        out_specs=cur(x.shape[1]),
        out_shape=jax.ShapeDtypeStruct(x.shape, F32),
        compiler_params=_params("parallel", "parallel"),
        name="mix_prompt",
    )(sinks, q, k, k, v, v, gu, gv, x, ws, bias, ga, gg, wo)


def _mix_sample_kernel(q_ref, kn_ref, vn_ref, gu_ref, gv_ref, x_ref, ck_ref, cv_ref,
                       sink_ref, ctab_ref, bias_ref, ga_ref, gg_ref, wo_ref,
                       o_ref, nk_ref, nv_ref, attn_s, sgu_s):
    nseq = ck_ref.shape[0]
    t = q_ref.shape[0] // nseq
    w = ck_ref.shape[1]
    low, lowf, highf = _half_masks()
    rows = N_Q_HEADS * t
    band = 2 * w
    i = lax.broadcasted_iota(I32, (rows, band), 0) & (t - 1)
    j = lax.broadcasted_iota(I32, (rows, band), 1)
    valid = (j > i) & (j <= i + w)
    sink = sink_ref[:, 0:1]
    pad = jnp.zeros((band - w - t, LANES), F32)

    def body(b, carry):
        r0 = pl.multiple_of(b * t, t)
        rs = pl.ds(r0, t)
        qs = _stack_heads(q_ref[rs, :], lowf, highf)
        kc, kn = ck_ref[b], kn_ref[rs, :]
        vc, vn = cv_ref[b], vn_ref[rs, :]
        kb = jnp.concatenate([kc, kn, pad], axis=0).astype(BF16)
        vb = jnp.concatenate([vc, vn, pad], axis=0).astype(BF16)
        s = jnp.where(valid, _dot_nt(qs, kb), NEG_INF)
        o_all = _dot(_sink_softmax(s, sink), vb)
        for c in range(4):
            attn_s[rs, c * LANES:(c + 1) * LANES] = jnp.where(
                low, o_all[c * t:(c + 1) * t], o_all[(4 + c) * t:(5 + c) * t])
        nk_ref[b] = jnp.concatenate([kc[t:], kn], axis=0)
        nv_ref[b] = jnp.concatenate([vc[t:], vn], axis=0)
        gvb = gv_ref[rs, :]
        mixed = ctab_ref[0] * gvb[0:1, :]
        for r in range(1, t):
            mixed = mixed + ctab_ref[r] * gvb[r:r + 1, :]
        sgu_s[rs, :] = gu_ref[rs, :] * (mixed + bias_ref[...])
        return carry

    lax.fori_loop(0, nseq, body, 0)
    o_ref[...] = _merge_out(attn_s[...], sgu_s[...], ga_ref[...], gg_ref[...], wo_ref[...], x_ref[...])


def _mix_sample(q, k, v, gu, gv, x, ck, cv, sink_tab, ctab, bias, ga, gg, wo, *, group):
    nseq, w, _ = ck.shape
    t = x.shape[0] // nseq
    assert t == SUBLANES and w == WINDOW and nseq % group == 0
    rows = group * t
    row = lambda width: pl.BlockSpec((rows, width), lambda i: (i, 0))
    cache = pl.BlockSpec((group, w, LANES), lambda i: (i, 0, 0))
    return pl.pallas_call(
        _mix_sample_kernel,
        grid=(nseq // group,),
        in_specs=[row(512), row(128), row(128), row(512), row(512), row(x.shape[1]), cache, cache,
                  _full(sink_tab.shape), _full(ctab.shape), _full(bias.shape),
                  _full(ga.shape), _full(gg.shape), _full(wo.shape)],
        out_specs=[row(x.shape[1]), cache, cache],
        out_shape=[jax.ShapeDtypeStruct(x.shape, F32), jax.ShapeDtypeStruct(ck.shape, F32),
                   jax.ShapeDtypeStruct(cv.shape, F32)],
        scratch_shapes=[pltpu.VMEM((rows, 512), F32), pltpu.VMEM((rows, 512), F32)],
        compiler_params=_params("parallel"),
        name="mix_sample",
    )(q, k, v, gu, gv, x, ck, cv, sink_tab, ctab, bias, ga, gg, wo)


def _mem_kv_kernel(m_ref, g_ref, w_ref, k_ref, v_ref):
    mn = _rms(m_ref[...], g_ref[...]).astype(BF16)
    kv = _dot(mn, w_ref[...])
    half = kv.shape[1] // 2
    k_ref[...] = kv[:, :half]
    v_ref[...] = kv[:, half:]


def _mem_kv(mem, g, w_kv, *, rows):
    t, d = mem.shape
    width = w_kv.shape[1] // 2
    row = lambda wd: pl.BlockSpec((rows, wd), lambda i: (i, 0))
    return pl.pallas_call(
        _mem_kv_kernel,
        grid=(t // rows,),
        in_specs=[row(d), _full(g.shape), _full(w_kv.shape)],
        out_specs=[row(width), row(width)],
        out_shape=[jax.ShapeDtypeStruct((t, width), F32)] * 2,
        compiler_params=_params("parallel"),
        name="mem_kv",
    )(mem, g, w_kv)


def _softmax(s):
    p = jnp.exp(s - jnp.max(s, axis=-1, keepdims=True))
    return (p / jnp.sum(p, axis=-1, keepdims=True)).astype(BF16)


def _mem_heads(q, mk, mv, low, lowf, highf):
    rows = q.shape[0]
    outs = []
    for c in range(MEM_HEADS // 2):
        sl = slice(c * LANES, (c + 1) * LANES)
        qc = q[:, sl]
        q2 = jnp.concatenate([(qc * lowf).astype(BF16), (qc * highf).astype(BF16)], axis=0)
        o = _dot(_softmax(_dot_nt(q2, mk[:, sl].astype(BF16))), mv[:, sl].astype(BF16))
        outs.append(jnp.where(low, o[:rows], o[rows:]))
    return jnp.concatenate(outs, axis=1)


def _mem_attn_prompt_kernel(x_ref, g_ref, wq_ref, mk_ref, mv_ref, wo_ref, o_ref):
    low, lowf, highf = _half_masks()
    x = x_ref[...]
    q = _dot(_rms(x, g_ref[...]).astype(BF16), wq_ref[...]) * (MEM_HEAD_DIM ** -0.5)
    o = _mem_heads(q, mk_ref[...], mv_ref[...], low, lowf, highf)
    o_ref[...] = x + _dot(o.astype(BF16), wo_ref[...])


def _mem_attn_prompt(x, g, wq, mk, mv, wo, *, batch, seq, rows):
    n_mem = mk.shape[0] // batch
    nb = seq // rows
    row = pl.BlockSpec((rows, x.shape[1]), lambda b, n: (b * nb + n, 0))
    mem = pl.BlockSpec((n_mem, mk.shape[1]), lambda b, n: (b, 0))
    return pl.pallas_call(
        _mem_attn_prompt_kernel,
        grid=(batch, nb),
        in_specs=[row, _full(g.shape), _full(wq.shape), mem, mem, _full(wo.shape)],
        out_specs=row,
        out_shape=jax.ShapeDtypeStruct(x.shape, F32),
        compiler_params=_params("parallel", "parallel"),
        name="mem_attn_prompt",
    )(x, g, wq, mk, mv, wo)


def _mem_attn_sample_kernel(x_ref, g_ref, wq_ref, mk_ref, mv_ref, wo_ref, o_ref, q_s, a_s):
    nseq = mk_ref.shape[0]
    t = x_ref.shape[0] // nseq
    low, lowf, highf = _half_masks()
    x = x_ref[...]
    q_s[...] = _dot(_rms(x, g_ref[...]).astype(BF16), wq_ref[...]) * (MEM_HEAD_DIM ** -0.5)

    def body(b, carry):
        rs = pl.ds(pl.multiple_of(b * t, t), t)
        a_s[rs, :] = _mem_heads(q_s[rs, :], mk_ref[b], mv_ref[b], low, lowf, highf)
        return carry

    lax.fori_loop(0, nseq, body, 0)
    o_ref[...] = x + _dot(a_s[...].astype(BF16), wo_ref[...])


def _mem_attn_sample(x, g, wq, mk, mv, wo, *, group):
    nseq, n_mem, width = mk.shape
    t = x.shape[0] // nseq
    rows = group * t
    row = pl.BlockSpec((rows, x.shape[1]), lambda i: (i, 0))
    mem = pl.BlockSpec((group, n_mem, width), lambda i: (i, 0, 0))
    return pl.pallas_call(
        _mem_attn_sample_kernel,
        grid=(nseq // group,),
        in_specs=[row, _full(g.shape), _full(wq.shape), mem, mem, _full(wo.shape)],
        out_specs=row,
        out_shape=jax.ShapeDtypeStruct(x.shape, F32),
        scratch_shapes=[pltpu.VMEM((rows, width), F32), pltpu.VMEM((rows, width), F32)],
        compiler_params=_params("parallel"),
        name="mem_attn_sample",
    )(x, g, wq, mk, mv, wo)


def _top16(s):
    nrows, cols = s.shape
    r = lax.broadcasted_iota(I32, (nrows, cols), 0).astype(F32)
    slot = lax.broadcasted_iota(I32, (PEER_TOPK, cols), 0)
    vals = jnp.zeros((PEER_TOPK, cols), F32)
    idxs = jnp.zeros((PEER_TOPK, cols), F32)
    for p in range(PEER_TOPK):
        m = jnp.max(s, axis=0, keepdims=True)
        am = jnp.min(jnp.where(s == m, r, float(nrows)), axis=0, keepdims=True)
        vals = jnp.where(slot == p, m, vals)
        idxs = jnp.where(slot == p, am, idxs)
        s = jnp.where(r == am, -jnp.inf, s)
    return vals, idxs


def _peer_route_kernel(x_ref, g_ref, wh_ref, wl_ref, kh_ref, kl_ref,
                       xn_ref, i1_ref, i2_ref, gate_ref, q_s, i1_s, i2_s, gt_s):
    tb = x_ref.shape[0]
    xn = _rms(x_ref[...], g_ref[...])
    xh, xl = _split(xn)
    xn_ref[...] = xh
    wh = wh_ref[...]
    q = _dot(xh, wh) + _dot(xh, wl_ref[...]) + _dot(xl, wh)
    for hc in range(2 * PEER_HEADS):
        q_s[hc] = q[:, hc * D_HALF:(hc + 1) * D_HALF]

    def head(h, carry):
        tops = []
        for c in range(2):
            qh, ql = _split(q_s[2 * h + c])
            kh = kh_ref[2 * h + c]
            s = _dot_nt(kh, qh) + _dot_nt(kh, ql) + _dot_nt(kl_ref[2 * h + c], qh)
            tops.append(_top16(s))
        (s1, i1), (s2, i2) = tops
        cand = jnp.concatenate(
            [jnp.broadcast_to(s1[p:p + 1], (PEER_TOPK, tb)) + s2 for p in range(PEER_TOPK)], axis=0)
        top, cidx = _top16(cand)
        ci = cidx.astype(I32)
        pi = ci >> 4
        qi = ci & (PEER_TOPK - 1)
        e1 = jnp.zeros((PEER_TOPK, tb), F32)
        e2 = jnp.zeros((PEER_TOPK, tb), F32)
        for p in range(PEER_TOPK):
            e1 = jnp.where(pi == p, jnp.broadcast_to(i1[p:p + 1], (PEER_TOPK, tb)), e1)
            e2 = jnp.where(qi == p, jnp.broadcast_to(i2[p:p + 1], (PEER_TOPK, tb)), e2)
        e = jnp.exp(top - jnp.max(top, axis=0, keepdims=True))
        rs = pl.ds(pl.multiple_of(h * PEER_TOPK, PEER_TOPK), PEER_TOPK)
        i1_s[rs, :] = e1
        i2_s[rs, :] = e2
        gt_s[rs, :] = e / jnp.sum(e, axis=0, keepdims=True)
        return carry

    lax.fori_loop(0, PEER_HEADS, head, 0)
    i1_ref[...] = i1_s[...].T.astype(I32)
    i2_ref[...] = i2_s[...].T.astype(I32)
    gate_ref[...] = gt_s[...].T


def _peer_route(x, g, wh, wl, kh, kl):
    t, d = x.shape
    tb = LANES
    sel = PEER_HEADS * PEER_TOPK
    row = lambda width: pl.BlockSpec((tb, width), lambda i: (i, 0))
    return pl.pallas_call(
        _peer_route_kernel,
        grid=(t // tb,),
        in_specs=[row(d), _full(g.shape), _full(wh.shape), _full(wl.shape), _full(kh.shape), _full(kl.shape)],
        out_specs=[row(d), row(sel), row(sel), row(sel)],
        out_shape=[jax.ShapeDtypeStruct((t, d), BF16), jax.ShapeDtypeStruct((t, sel), I32),
                   jax.ShapeDtypeStruct((t, sel), I32), jax.ShapeDtypeStruct((t, sel), F32)],
        scratch_shapes=[pltpu.VMEM((2 * PEER_HEADS, tb, D_HALF), F32),
                        pltpu.VMEM((sel, tb), F32), pltpu.VMEM((sel, tb), F32), pltpu.VMEM((sel, tb), F32)],
        compiler_params=_params("parallel"),
        name="peer_route",
    )(x, g, wh, wl, kh, kl)


def _peer_gates_kernel(i1_ref, i2_ref, gate_ref, o_ref, g_s):
    tb = i1_ref.shape[0]
    sel = i1_ref.shape[1]
    key = lax.broadcasted_iota(I32, (N_KEYS, sel), 0)

    def per_group(gidx, carry):
        t0 = pl.multiple_of(gidx * SUBLANES, SUBLANES)
        i1g = i1_ref[pl.ds(t0, SUBLANES), :]
        i2g = i2_ref[pl.ds(t0, SUBLANES), :]
        gtg = gate_ref[pl.ds(t0, SUBLANES), :]
        for s in range(SUBLANES):
            w1 = jnp.where(key == i1g[s:s + 1, :], gtg[s:s + 1, :], 0.0).astype(BF16)
            w2 = jnp.where(key == i2g[s:s + 1, :], 1.0, 0.0).astype(BF16)
            r0 = pl.multiple_of((t0 + s) * GATE_PITCH, SUBLANES)
            g_s[pl.ds(r0, N_KEYS), :] = _dot_nt(w1, w2)
        return carry

    lax.fori_loop(0, tb // SUBLANES, per_group, 0)

    rows = 2 * SUBLANES

    def relayout(gidx, carry):
        t0 = pl.multiple_of(gidx * rows, rows)
        for a in range(N_KEYS):
            lo = g_s[pl.ds(t0 * GATE_PITCH + a, SUBLANES, stride=GATE_PITCH), :]
            hi = g_s[pl.ds((t0 + SUBLANES) * GATE_PITCH + a, SUBLANES, stride=GATE_PITCH), :]
            o_ref[pl.ds(t0, rows), a * N_KEYS:(a + 1) * N_KEYS] = jnp.concatenate([lo, hi], axis=0).astype(BF16)
        return carry

    lax.fori_loop(0, tb // rows, relayout, 0)


def _peer_gates(i1, i2, gate, *, rows):
    t, sel = i1.shape
    row = pl.BlockSpec((rows, sel), lambda i: (i, 0))
    return pl.pallas_call(
        _peer_gates_kernel,
        grid=(t // rows,),
        in_specs=[row, row, row],
        out_specs=pl.BlockSpec((rows, N_KEYS * N_KEYS), lambda i: (i, 0)),
        out_shape=jax.ShapeDtypeStruct((t, N_KEYS * N_KEYS), BF16),
        scratch_shapes=[pltpu.VMEM((rows * GATE_PITCH, N_KEYS), F32)],
        compiler_params=_params("parallel"),
        name="peer_gates",
    )(i1, i2, gate)


def _peer_dense_kernel(xn_ref, gates_ref, u_ref, v_ref, x_ref, g_ref, o_ref, acc):
    k = pl.program_id(1)

    @pl.when(k == 0)
    def _():
        acc[...] = jnp.zeros_like(acc)

    h = _dot_nt(xn_ref[...], u_ref[...])
    a = (_gelu(h) * gates_ref[...].astype(F32)).astype(BF16)
    acc[...] += _dot(a, v_ref[...])

    @pl.when(k == pl.num_programs(1) - 1)
    def _():
        o_ref[...] = _rms(x_ref[...] + acc[...], g_ref[...])


def _peer_dense(xn, gates, u, v, x, g, *, rows, experts):
    t, d = x.shape
    n_exp = u.shape[0]
    tok = pl.BlockSpec((rows, d), lambda i, k: (i, 0))
    tab = pl.BlockSpec((experts, d), lambda i, k: (k, 0))
    return pl.pallas_call(
        _peer_dense_kernel,
        grid=(t // rows, n_exp // experts),
        in_specs=[tok, pl.BlockSpec((rows, experts), lambda i, k: (i, k)), tab, tab, tok, _full(g.shape)],
        out_specs=tok,
        out_shape=jax.ShapeDtypeStruct((t, d), F32),
        scratch_shapes=[pltpu.VMEM((rows, d), F32)],
        compiler_params=_params("parallel", "arbitrary"),
        name="peer_dense",
    )(xn, gates, u, v, x, g)


def _peer(x, g_ffn, wh, wl, kh, kl, u, v, g_final):
    t = x.shape[0]
    xn, i1, i2, gate = _peer_route(x, g_ffn, wh, wl, kh, kl)
    gates = _peer_gates(i1, i2, gate, rows=LANES)
    return _peer_dense(xn, gates, u, v, x, g_final, rows=min(512, t), experts=2048)


def _head_perm():
    new = np.zeros(ATTN_WIDTH, np.int32)
    for h in range(N_Q_HEADS):
        dst = (h % 4) * LANES + (h // 4) * HEAD_DIM
        new[dst:dst + HEAD_DIM] = np.arange(h * HEAD_DIM, (h + 1) * HEAD_DIM)
    return new


def _rope_tables(pos):
    half = HEAD_DIM // 2
    inv = ROPE_THETA ** (-jnp.arange(half, dtype=F32) / half)
    ang = pos.astype(F32)[:, None] * inv[None, :]
    cos, sin = jnp.cos(ang), jnp.sin(ang)
    reps = LANES // HEAD_DIM
    return (jnp.tile(jnp.concatenate([cos, cos], axis=1), (1, reps)),
            jnp.tile(jnp.concatenate([-sin, sin], axis=1), (1, reps)))


def kernel(x_prompt, x_sample, mem_prompt, cache_swa_k, cache_swa_v, cache_mem_k, cache_mem_v, g_mix, w_in, attn_sinks, g_sgu, w_spatial, b_spatial, g_attn_out, g_gmlp_out, w_out, g_cross, g_mem, w_cq, w_mk, w_mv, w_co, g_ffn, w_peer_q, peer_sub_keys, peer_u, peer_v, g_final):
    batch, seq, d = x_prompt.shape
    dec_batch, dec_seq, _ = x_sample.shape
    depth = g_mix.shape[0]
    n_mem = mem_prompt.shape[1]
    past_len = 8192
    assert seq % WINDOW == 0 and dec_seq == SUBLANES and cache_swa_k.shape[2] == WINDOW

    perm = _head_perm()
    cos_p, sin_p = _rope_tables(jnp.arange(seq, dtype=I32))
    cos_s, sin_s = _rope_tables(past_len + (jnp.arange(dec_batch * dec_seq, dtype=I32) % dec_seq))
    seg = jnp.asarray(np.kron(np.eye(N_GMLP_HEADS), np.full((HEAD_DIM, HEAD_DIM), 1.0 / HEAD_DIM)), BF16)
    row2 = lambda a: a.reshape(1, -1)

    xp = x_prompt.reshape(batch * seq, d)
    xs = x_sample.reshape(dec_batch * dec_seq, d)
    outs = {name: [] for name in ("kp", "vp", "mk", "mv", "ks", "vs", "gvs")}
    for l in range(depth):
        w_in_l = jnp.concatenate([w_in[l][:, :ATTN_WIDTH][:, perm], w_in[l][:, ATTN_WIDTH:]], axis=1).astype(BF16)
        w_out_l = jnp.concatenate([w_out[l][:ATTN_WIDTH][perm], w_out[l][ATTN_WIDTH:]], axis=0).astype(BF16)
        ga = row2(g_attn_out[l][perm])
        gg = row2(g_gmlp_out[l])
        gs = row2(g_sgu[l])
        bias = jnp.repeat(b_spatial[l].T, HEAD_DIM, axis=1)
        wt = jnp.tril(w_spatial[l][:, :dec_seq, :dec_seq])
        ctab = jnp.repeat(jnp.transpose(wt, (2, 1, 0)), HEAD_DIM, axis=2)
        sink_tab = jnp.broadcast_to(jnp.repeat(attn_sinks[l], dec_seq)[:, None], (N_Q_HEADS * dec_seq, LANES))

        mix = functools.partial(_mix_in, g=row2(g_mix[l]), w_in=w_in_l, gs=gs, seg=seg)
        rows_p = 512 if seq % 512 == 0 else WINDOW
        qp, kp, vp, gup, gvp = mix(xp, cosf=cos_p, sinf=sin_p, rows=rows_p, table_blocks=seq // rows_p)
        rows_s = min(512, dec_batch * dec_seq)
        qs, ks, vs, gus, gvs = mix(xs, cosf=cos_s, sinf=sin_s, rows=rows_s,
                                   table_blocks=dec_batch * dec_seq // rows_s)

        xp = _mix_prompt(attn_sinks[l], qp, kp, vp, gup, gvp, xp, w_spatial[l], bias, ga, gg, w_out_l,
                         batch=batch, seq=seq)
        ck = cache_swa_k[l].reshape(dec_batch, WINDOW, N_KV_HEADS * HEAD_DIM)
        cv = cache_swa_v[l].reshape(dec_batch, WINDOW, N_KV_HEADS * HEAD_DIM)
        xs, nks, nvs = _mix_sample(qs, ks, vs, gus, gvs, xs, ck, cv, sink_tab, ctab, bias[:dec_seq], ga, gg,
                                   w_out_l, group=min(16, dec_batch))

        w_kv = jnp.concatenate([w_mk[l], w_mv[l]], axis=1).astype(BF16)
        mk, mv = _mem_kv(mem_prompt.reshape(batch * n_mem, d), row2(g_mem[l]), w_kv, rows=n_mem)
        wq = w_cq[l].astype(BF16)
        wo = w_co[l].astype(BF16)
        gc = row2(g_cross[l])
        xp = _mem_attn_prompt(xp, gc, wq, mk, mv, wo, batch=batch, seq=seq, rows=256 if seq % 256 == 0 else WINDOW)
        cmk = cache_mem_k[l].reshape(dec_batch, n_mem, MEM_HEADS * MEM_HEAD_DIM)
        cmv = cache_mem_v[l].reshape(dec_batch, n_mem, MEM_HEADS * MEM_HEAD_DIM)
        xs = _mem_attn_sample(xs, gc, wq, cmk, cmv, wo, group=min(8, dec_batch))

        wh, wl = _split(w_peer_q[l])
        kh, kl = _split(peer_sub_keys[l].reshape(2 * PEER_HEADS, N_KEYS, D_HALF))
        u = peer_u[l].astype(BF16)
        v = peer_v[l].astype(BF16)
        last = l == depth - 1
        assert last, "stacked layers need an un-normalised PEER output"
        peer = functools.partial(_peer, g_ffn=row2(g_ffn[l]), wh=wh, wl=wl, kh=kh, kl=kl, u=u, v=v,
                                 g_final=row2(g_final))
        xp = peer(xp)
        xs = peer(xs)

        outs["kp"].append(kp.reshape(batch, seq, N_KV_HEADS, HEAD_DIM)[:, seq - WINDOW:])
        outs["vp"].append(vp.reshape(batch, seq, N_KV_HEADS, HEAD_DIM)[:, seq - WINDOW:])
        outs["mk"].append(mk.reshape(batch, n_mem, MEM_HEADS, MEM_HEAD_DIM))
        outs["mv"].append(mv.reshape(batch, n_mem, MEM_HEADS, MEM_HEAD_DIM))
        outs["ks"].append(nks.reshape(dec_batch, WINDOW, N_KV_HEADS, HEAD_DIM))
        outs["vs"].append(nvs.reshape(dec_batch, WINDOW, N_KV_HEADS, HEAD_DIM))
        outs["gvs"].append(gvs.reshape(dec_batch, dec_seq, N_GMLP_HEADS, HEAD_DIM))

    stack = lambda name: jnp.stack(outs[name])
    return (xp.reshape(batch, seq, d), xs.reshape(dec_batch, dec_seq, d),
            stack("kp"), stack("vp"), stack("mk"), stack("mv"), stack("ks"), stack("vs"), stack("gvs"))
```

```python
import functools

import jax
import jax.numpy as jnp
import numpy as np
from jax import lax
from jax.experimental import pallas as pl
from jax.experimental.pallas import tpu as pltpu

F32 = jnp.float32
BF16 = jnp.bfloat16
I32 = jnp.int32

LANES = 128
SUBLANES = 8
VMEM_LIMIT_BYTES = 56 * 1024 * 1024

HEAD_DIM = 64
N_Q_HEADS = 8
N_KV_HEADS = 2
ATTN_WIDTH = N_Q_HEADS * HEAD_DIM
WINDOW = 128
PAST_LEN = 8192
ROPE_THETA = 10000.0
N_GMLP_HEADS = 8
GMLP_WIDTH = N_GMLP_HEADS * HEAD_DIM
CHUNK = 128
MEM_HEADS = 4
MEM_HEAD_DIM = 64
PEER_HEADS = 8
N_KEYS = 128
PEER_TOPK = 16
D_HALF = 128
EPS = 1e-6
NEG_INF = -1e30
SQRT_HALF = float(np.sqrt(0.5))

GATE_PITCH = N_KEYS + SUBLANES
GATE_TILES_PER_STEP = 2

NT_DIMS = (((1,), (1,)), ((), ()))


def _params(*semantics):
    return pltpu.CompilerParams(dimension_semantics=semantics, vmem_limit_bytes=VMEM_LIMIT_BYTES)


def _full(shape):
    zeros = (0,) * len(shape)
    return pl.BlockSpec(shape, lambda *_: zeros)


def _rms(x, g):
    return x * lax.rsqrt(jnp.mean(x * x, axis=-1, keepdims=True) + EPS) * g


def _gelu(x):
    return 0.5 * x * (1.0 + lax.erf(x * SQRT_HALF))


def _dot(a, b):
    return jnp.dot(a, b, preferred_element_type=F32)


def _dot_nt(a, b):
    return lax.dot_general(a, b, NT_DIMS, preferred_element_type=F32)


def _split(x):
    hi = x.astype(BF16)
    return hi, (x - hi.astype(F32)).astype(BF16)


def _half_masks():
    lane = lax.broadcasted_iota(I32, (1, LANES), 1)
    low = lane < HEAD_DIM
    return low, low.astype(F32), 1.0 - low.astype(F32)


def _mix_in_kernel(x_ref, g_ref, w_ref, cos_ref, sin_ref, gs_ref, seg_ref,
                   q_ref, k_ref, v_ref, gu_ref, gv_ref):
    xn = _rms(x_ref[...], g_ref[...]).astype(BF16)
    proj = _dot(xn, w_ref[...])
    cosf = cos_ref[...]
    sinf = sin_ref[...]
    lane = lax.broadcasted_iota(I32, (1, LANES), 1)
    first_half = (lane & (HEAD_DIM - 1)) < HEAD_DIM // 2

    def rope(c):
        partner = jnp.where(first_half, pltpu.roll(c, LANES - HEAD_DIM // 2, 1),
                            pltpu.roll(c, HEAD_DIM // 2, 1))
        return c * cosf + partner * sinf

    for c in range(ATTN_WIDTH // LANES):
        sl = slice(c * LANES, (c + 1) * LANES)
        q_ref[:, sl] = rope(proj[:, sl]) * (HEAD_DIM ** -0.5)
    k_ref[...] = rope(proj[:, 512:640])
    v_ref[...] = proj[:, 640:768]
    gu_ref[...] = _gelu(proj[:, 768:1280])
    gv = _gelu(proj[:, 1280:1792])
    hi, lo = _split(gv * gv)
    seg = seg_ref[...]
    ms = _dot(hi, seg) + _dot(lo, seg)
    gv_ref[...] = gv * lax.rsqrt(ms + EPS) * gs_ref[...]


def _mix_in(x, g, w_in, cosf, sinf, gs, seg, *, rows, table_blocks):
    t = x.shape[0]
    d = x.shape[1]
    row = lambda width: pl.BlockSpec((rows, width), lambda i: (i, 0))
    tab = pl.BlockSpec((rows, LANES), lambda i: (i % table_blocks, 0))
    return pl.pallas_call(
        _mix_in_kernel,
        grid=(t // rows,),
        in_specs=[row(d), _full(g.shape), _full(w_in.shape), tab, tab, _full(gs.shape), _full(seg.shape)],
        out_specs=[row(512), row(128), row(128), row(512), row(512)],
        out_shape=[jax.ShapeDtypeStruct((t, 512), F32), jax.ShapeDtypeStruct((t, 128), F32),
                   jax.ShapeDtypeStruct((t, 128), F32), jax.ShapeDtypeStruct((t, 512), F32),
                   jax.ShapeDtypeStruct((t, 512), F32)],
        compiler_params=_params("parallel"),
        name="mix_in",
    )(x, g, w_in, cosf, sinf, gs, seg)


def _stack_heads(q, lowf, highf):
    parts = [(q[:, c * LANES:(c + 1) * LANES] * m).astype(BF16)
             for m in (lowf, highf) for c in range(4)]
    return jnp.concatenate(parts, axis=0)


def _sink_softmax(s, sink):
    m = jnp.maximum(jnp.max(s, axis=-1, keepdims=True), sink)
    p = jnp.exp(s - m)
    den = jnp.sum(p, axis=-1, keepdims=True) + jnp.exp(sink - m)
    return (p / den).astype(BF16)


def _merge_out(attn, sgu, ga, gg, wo, x):
    merged = jnp.concatenate([_rms(attn, ga), _rms(sgu, gg)], axis=1).astype(BF16)
    return x + _dot(merged, wo)


def _mix_prompt_kernel(sinks_ref, q_ref, kp_ref, kc_ref, vp_ref, vc_ref, gu_ref, gv_ref, x_ref,
                       ws_ref, bias_ref, ga_ref, gg_ref, wo_ref, o_ref):
    n = pl.program_id(1)
    low, lowf, highf = _half_masks()
    blk = WINDOW
    qs = _stack_heads(q_ref[...], lowf, highf)
    kb = jnp.concatenate([kp_ref[...], kc_ref[...]], axis=0).astype(BF16)
    vb = jnp.concatenate([vp_ref[...], vc_ref[...]], axis=0).astype(BF16)
    s_all = _dot_nt(qs, kb)
    i = lax.broadcasted_iota(I32, (blk, 2 * blk), 0)
    j = lax.broadcasted_iota(I32, (blk, 2 * blk), 1)
    first_key = jnp.where(n > 0, 0, blk)
    valid = (j > i) & (j <= i + blk) & (j >= first_key)
    probs = []
    for h in range(N_Q_HEADS):
        s = jnp.where(valid, s_all[h * blk:(h + 1) * blk], NEG_INF)
        probs.append(_sink_softmax(s, sinks_ref[h]))
    o_all = _dot(jnp.concatenate(probs, axis=0), vb)
    attn = jnp.concatenate(
        [jnp.where(low, o_all[c * blk:(c + 1) * blk], o_all[(4 + c) * blk:(5 + c) * blk])
         for c in range(4)], axis=1)

    gvb = gv_ref[...].astype(BF16)
    r = lax.broadcasted_iota(I32, (CHUNK, CHUNK), 0)
    c_ = lax.broadcasted_iota(I32, (CHUNK, CHUNK), 1)
    tril = r >= c_
    mixed = []
    for c in range(4):
        g = gvb[:, c * LANES:(c + 1) * LANES]
        y0 = _dot(jnp.where(tril, ws_ref[2 * c], 0.0).astype(BF16), g)
        y1 = _dot(jnp.where(tril, ws_ref[2 * c + 1], 0.0).astype(BF16), g)
        mixed.append(jnp.where(low, y0, y1))
    sgu = gu_ref[...] * (jnp.concatenate(mixed, axis=1) + bias_ref[...])
    o_ref[...] = _merge_out(attn, sgu, ga_ref[...], gg_ref[...], wo_ref[...], x_ref[...])


def _mix_prompt(sinks, q, k, v, gu, gv, x, ws, bias, ga, gg, wo, *, batch, seq):
    nb = seq // WINDOW
    cur = lambda width: pl.BlockSpec((WINDOW, width), lambda b, n: (b * nb + n, 0))
    prev = lambda width: pl.BlockSpec((WINDOW, width), lambda b, n: (b * nb + jnp.maximum(n - 1, 0), 0))
    return pl.pallas_call(
        _mix_prompt_kernel,
        grid=(batch, nb),
        in_specs=[pl.BlockSpec(memory_space=pltpu.SMEM),
                  cur(512), prev(128), cur(128), prev(128), cur(128), cur(512), cur(512), cur(x.shape[1]),
                  _full(ws.shape), _full(bias.shape), _full(ga.shape), _full(gg.shape), _full(wo.shape)],
        out_specs=cur(x.shape[1]),
        out_shape=jax.ShapeDtypeStruct(x.shape, F32),
        compiler_params=_params("parallel", "parallel"),
        name="mix_prompt",
    )(sinks, q, k, k, v, v, gu, gv, x, ws, bias, ga, gg, wo)


def _mix_sample_kernel(q_ref, kn_ref, vn_ref, gu_ref, gv_ref, x_ref, ck_ref, cv_ref,
                       sink_ref, ctab_ref, bias_ref, ga_ref, gg_ref, wo_ref,
                       o_ref, nk_ref, nv_ref, attn_s, sgu_s):
    nseq = ck_ref.shape[0]
    t = q_ref.shape[0] // nseq
    w = ck_ref.shape[1]
    low, lowf, highf = _half_masks()
    rows = N_Q_HEADS * t
    band = 2 * w
    i = lax.broadcasted_iota(I32, (rows, band), 0) & (t - 1)
    j = lax.broadcasted_iota(I32, (rows, band), 1)
    valid = (j > i) & (j <= i + w)
    sink = sink_ref[:, 0:1]
    pad = jnp.zeros((band - w - t, LANES), F32)

    def body(b, carry):
        r0 = pl.multiple_of(b * t, t)
        rs = pl.ds(r0, t)
        qs = _stack_heads(q_ref[rs, :], lowf, highf)
        kc, kn = ck_ref[b], kn_ref[rs, :]
        vc, vn = cv_ref[b], vn_ref[rs, :]
        kb = jnp.concatenate([kc, kn, pad], axis=0).astype(BF16)
        vb = jnp.concatenate([vc, vn, pad], axis=0).astype(BF16)
        s = jnp.where(valid, _dot_nt(qs, kb), NEG_INF)
        o_all = _dot(_sink_softmax(s, sink), vb)
        for c in range(4):
            attn_s[rs, c * LANES:(c + 1) * LANES] = jnp.where(
                low, o_all[c * t:(c + 1) * t], o_all[(4 + c) * t:(5 + c) * t])
        nk_ref[b] = jnp.concatenate([kc[t:], kn], axis=0)
        nv_ref[b] = jnp.concatenate([vc[t:], vn], axis=0)
        gvb = gv_ref[rs, :]
        mixed = ctab_ref[0] * gvb[0:1, :]
        for r in range(1, t):
            mixed = mixed + ctab_ref[r] * gvb[r:r + 1, :]
        sgu_s[rs, :] = gu_ref[rs, :] * (mixed + bias_ref[...])
        return carry

    lax.fori_loop(0, nseq, body, 0)
    o_ref[...] = _merge_out(attn_s[...], sgu_s[...], ga_ref[...], gg_ref[...], wo_ref[...], x_ref[...])


def _mix_sample(q, k, v, gu, gv, x, ck, cv, sink_tab, ctab, bias, ga, gg, wo, *, group):
    nseq, w, _ = ck.shape
    t = x.shape[0] // nseq
    assert t == SUBLANES and w == WINDOW and nseq % group == 0
    rows = group * t
    row = lambda width: pl.BlockSpec((rows, width), lambda i: (i, 0))
    cache = pl.BlockSpec((group, w, LANES), lambda i: (i, 0, 0))
    return pl.pallas_call(
        _mix_sample_kernel,
        grid=(nseq // group,),
        in_specs=[row(512), row(128), row(128), row(512), row(512), row(x.shape[1]), cache, cache,
                  _full(sink_tab.shape), _full(ctab.shape), _full(bias.shape),
                  _full(ga.shape), _full(gg.shape), _full(wo.shape)],
        out_specs=[row(x.shape[1]), cache, cache],
        out_shape=[jax.ShapeDtypeStruct(x.shape, F32), jax.ShapeDtypeStruct(ck.shape, F32),
                   jax.ShapeDtypeStruct(cv.shape, F32)],
        scratch_shapes=[pltpu.VMEM((rows, 512), F32), pltpu.VMEM((rows, 512), F32)],
        compiler_params=_params("parallel"),
        name="mix_sample",
    )(q, k, v, gu, gv, x, ck, cv, sink_tab, ctab, bias, ga, gg, wo)


def _mem_kv_kernel(m_ref, g_ref, w_ref, k_ref, v_ref):
    mn = _rms(m_ref[...], g_ref[...]).astype(BF16)
    kv = _dot(mn, w_ref[...])
    half = kv.shape[1] // 2
    k_ref[...] = kv[:, :half]
    v_ref[...] = kv[:, half:]


def _mem_kv(mem, g, w_kv, *, rows):
    t, d = mem.shape
    width = w_kv.shape[1] // 2
    row = lambda wd: pl.BlockSpec((rows, wd), lambda i: (i, 0))
    return pl.pallas_call(
        _mem_kv_kernel,
        grid=(t // rows,),
        in_specs=[row(d), _full(g.shape), _full(w_kv.shape)],
        out_specs=[row(width), row(width)],
        out_shape=[jax.ShapeDtypeStruct((t, width), F32)] * 2,
        compiler_params=_params("parallel"),
        name="mem_kv",
    )(mem, g, w_kv)


def _softmax(s):
    p = jnp.exp(s - jnp.max(s, axis=-1, keepdims=True))
    return (p / jnp.sum(p, axis=-1, keepdims=True)).astype(BF16)


def _mem_heads(q, mk, mv, low, lowf, highf):
    rows = q.shape[0]
    outs = []
    for c in range(MEM_HEADS // 2):
        sl = slice(c * LANES, (c + 1) * LANES)
        qc = q[:, sl]
        q2 = jnp.concatenate([(qc * lowf).astype(BF16), (qc * highf).astype(BF16)], axis=0)
        o = _dot(_softmax(_dot_nt(q2, mk[:, sl].astype(BF16))), mv[:, sl].astype(BF16))
        outs.append(jnp.where(low, o[:rows], o[rows:]))
    return jnp.concatenate(outs, axis=1)


def _mem_attn_prompt_kernel(x_ref, g_ref, wq_ref, mk_ref, mv_ref, wo_ref, o_ref):
    low, lowf, highf = _half_masks()
    x = x_ref[...]
    q = _dot(_rms(x, g_ref[...]).astype(BF16), wq_ref[...]) * (MEM_HEAD_DIM ** -0.5)
    o = _mem_heads(q, mk_ref[...], mv_ref[...], low, lowf, highf)
    o_ref[...] = x + _dot(o.astype(BF16), wo_ref[...])


def _mem_attn_prompt(x, g, wq, mk, mv, wo, *, batch, seq, rows):
    n_mem = mk.shape[0] // batch
    nb = seq // rows
    row = pl.BlockSpec((rows, x.shape[1]), lambda b, n: (b * nb + n, 0))
    mem = pl.BlockSpec((n_mem, mk.shape[1]), lambda b, n: (b, 0))
    return pl.pallas_call(
        _mem_attn_prompt_kernel,
        grid=(batch, nb),
        in_specs=[row, _full(g.shape), _full(wq.shape), mem, mem, _full(wo.shape)],
        out_specs=row,
        out_shape=jax.ShapeDtypeStruct(x.shape, F32),
        compiler_params=_params("parallel", "parallel"),
        name="mem_attn_prompt",
    )(x, g, wq, mk, mv, wo)


def _mem_attn_sample_kernel(x_ref, g_ref, wq_ref, mk_ref, mv_ref, wo_ref, o_ref, q_s, a_s):
    nseq = mk_ref.shape[0]
    t = x_ref.shape[0] // nseq
    low, lowf, highf = _half_masks()
    x = x_ref[...]
    q_s[...] = _dot(_rms(x, g_ref[...]).astype(BF16), wq_ref[...]) * (MEM_HEAD_DIM ** -0.5)

    def body(b, carry):
        rs = pl.ds(pl.multiple_of(b * t, t), t)
        a_s[rs, :] = _mem_heads(q_s[rs, :], mk_ref[b], mv_ref[b], low, lowf, highf)
        return carry

    lax.fori_loop(0, nseq, body, 0)
    o_ref[...] = x + _dot(a_s[...].astype(BF16), wo_ref[...])


def _mem_attn_sample(x, g, wq, mk, mv, wo, *, group):
    nseq, n_mem, width = mk.shape
    t = x.shape[0] // nseq
    rows = group * t
    row = pl.BlockSpec((rows, x.shape[1]), lambda i: (i, 0))
    mem = pl.BlockSpec((group, n_mem, width), lambda i: (i, 0, 0))
    return pl.pallas_call(
        _mem_attn_sample_kernel,
        grid=(nseq // group,),
        in_specs=[row, _full(g.shape), _full(wq.shape), mem, mem, _full(wo.shape)],
        out_specs=row,
        out_shape=jax.ShapeDtypeStruct(x.shape, F32),
        scratch_shapes=[pltpu.VMEM((rows, width), F32), pltpu.VMEM((rows, width), F32)],
        compiler_params=_params("parallel"),
        name="mem_attn_sample",
    )(x, g, wq, mk, mv, wo)


def _top16(s):
    nrows, cols = s.shape
    r = lax.broadcasted_iota(I32, (nrows, cols), 0).astype(F32)
    slot = lax.broadcasted_iota(I32, (PEER_TOPK, cols), 0)
    vals = jnp.zeros((PEER_TOPK, cols), F32)
    idxs = jnp.zeros((PEER_TOPK, cols), F32)
    for p in range(PEER_TOPK):
        m = jnp.max(s, axis=0, keepdims=True)
        am = jnp.min(jnp.where(s == m, r, float(nrows)), axis=0, keepdims=True)
        vals = jnp.where(slot == p, m, vals)
        idxs = jnp.where(slot == p, am, idxs)
        s = jnp.where(r == am, -jnp.inf, s)
    return vals, idxs


def _peer_route_kernel(x_ref, g_ref, wh_ref, wl_ref, kh_ref, kl_ref,
                       xn_ref, i1_ref, i2_ref, gate_ref, q_s, i1_s, i2_s, gt_s):
    tb = x_ref.shape[0]
    xn = _rms(x_ref[...], g_ref[...])
    xh, xl = _split(xn)
    xn_ref[...] = xh
    wh = wh_ref[...]
    q = _dot(xh, wh) + _dot(xh, wl_ref[...]) + _dot(xl, wh)
    for hc in range(2 * PEER_HEADS):
        q_s[hc] = q[:, hc * D_HALF:(hc + 1) * D_HALF]

    def head(h, carry):
        tops = []
        for c in range(2):
            qh, ql = _split(q_s[2 * h + c])
            kh = kh_ref[2 * h + c]
            s = _dot_nt(kh, qh) + _dot_nt(kh, ql) + _dot_nt(kl_ref[2 * h + c], qh)
            tops.append(_top16(s))
        (s1, i1), (s2, i2) = tops
        half = PEER_TOPK // 2
        cand = jnp.concatenate(
            [jnp.broadcast_to(s1[0:1], (PEER_TOPK, tb)) + s2]
            + [jnp.broadcast_to(s1[p:p + 1], (half, tb)) + s2[:half] for p in range(1, half)]
            + [s1[half:] + jnp.broadcast_to(s2[0:1], (half, tb))], axis=0)
        top, cidx = _top16(cand)
        ci = cidx.astype(I32)
        mid = ci - PEER_TOPK
        tail = PEER_TOPK + half * (half - 1)
        pi = jnp.where(ci < PEER_TOPK, 0,
                       jnp.where(ci < tail, 1 + (mid >> (half.bit_length() - 1)), ci - tail + half))
        qi = jnp.where(ci < PEER_TOPK, ci, jnp.where(ci < tail, mid & (half - 1), 0))
        e1 = jnp.zeros((PEER_TOPK, tb), F32)
        e2 = jnp.zeros((PEER_TOPK, tb), F32)
        for p in range(PEER_TOPK):
            e1 = jnp.where(pi == p, jnp.broadcast_to(i1[p:p + 1], (PEER_TOPK, tb)), e1)
            e2 = jnp.where(qi == p, jnp.broadcast_to(i2[p:p + 1], (PEER_TOPK, tb)), e2)
        e = jnp.exp(top - jnp.max(top, axis=0, keepdims=True))
        rs = pl.ds(pl.multiple_of(h * PEER_TOPK, PEER_TOPK), PEER_TOPK)
        i1_s[rs, :] = e1
        i2_s[rs, :] = e2
        gt_s[rs, :] = e / jnp.sum(e, axis=0, keepdims=True)
        return carry

    lax.fori_loop(0, PEER_HEADS, head, 0)
    i1_ref[...] = i1_s[...].T.astype(I32)
    i2_ref[...] = i2_s[...].T.astype(I32)
    gate_ref[...] = gt_s[...].T


def _peer_route(x, g, wh, wl, kh, kl):
    t, d = x.shape
    tb = 2 * LANES if t % (2 * LANES) == 0 else LANES
    sel = PEER_HEADS * PEER_TOPK
    row = lambda width: pl.BlockSpec((tb, width), lambda i: (i, 0))
    return pl.pallas_call(
        _peer_route_kernel,
        grid=(t // tb,),
        in_specs=[row(d), _full(g.shape), _full(wh.shape), _full(wl.shape), _full(kh.shape), _full(kl.shape)],
        out_specs=[row(d), row(sel), row(sel), row(sel)],
        out_shape=[jax.ShapeDtypeStruct((t, d), BF16), jax.ShapeDtypeStruct((t, sel), I32),
                   jax.ShapeDtypeStruct((t, sel), I32), jax.ShapeDtypeStruct((t, sel), F32)],
        scratch_shapes=[pltpu.VMEM((2 * PEER_HEADS, tb, D_HALF), F32),
                        pltpu.VMEM((sel, tb), F32), pltpu.VMEM((sel, tb), F32), pltpu.VMEM((sel, tb), F32)],
        compiler_params=_params("parallel"),
        name="peer_route",
    )(x, g, wh, wl, kh, kl)


def _peer_gates_kernel(i1_ref, i2_ref, gate_ref, o_ref, g_s):
    tb = i1_ref.shape[0]
    sel = i1_ref.shape[1]
    key = lax.broadcasted_iota(I32, (N_KEYS, sel), 0)
    zero = jnp.zeros((N_KEYS, sel), BF16)

    tile = 2 * SUBLANES

    def build(tidx):
        for half in range(2):
            t0 = pl.multiple_of(tidx * tile + half * SUBLANES, SUBLANES)
            i1g = i1_ref[pl.ds(t0, SUBLANES), :]
            i2g = i2_ref[pl.ds(t0, SUBLANES), :]
            gtg = gate_ref[pl.ds(t0, SUBLANES), :]
            w1 = [jnp.where(key == i1g[s:s + 1, :], gtg[s:s + 1, :], 0.0).astype(BF16) for s in range(SUBLANES)]
            w2 = [jnp.where(key == i2g[s:s + 1, :], 1.0, 0.0).astype(BF16) for s in range(SUBLANES)]
            for s in range(0, SUBLANES, 2):
                lhs = jnp.concatenate([w1[s], w1[s + 1]], axis=1)
                rhs = jnp.concatenate([jnp.concatenate([w2[s], zero], axis=1),
                                       jnp.concatenate([zero, w2[s + 1]], axis=1)], axis=0)
                g = _dot_nt(lhs, rhs)
                word = pltpu.pack_elementwise([g[:, :N_KEYS], g[:, N_KEYS:]], packed_dtype=BF16)
                pair = tidx * SUBLANES + (half * SUBLANES + s) // 2
                g_s[pl.ds(pl.multiple_of(pair * GATE_PITCH, SUBLANES), N_KEYS), :] = word

    def relayout(tidx):
        rows = pl.ds(pl.multiple_of(tidx * tile, tile), tile)
        for a in range(N_KEYS):
            words = g_s[pl.ds(tidx * SUBLANES * GATE_PITCH + a, SUBLANES, stride=GATE_PITCH), :]
            o_ref[rows, a * N_KEYS:(a + 1) * N_KEYS] = pltpu.bitcast(words, BF16)

    group = GATE_TILES_PER_STEP
    n_groups = tb // (tile * group)
    for u in range(group):
        build(u)

    def step(gidx, carry):
        for u in range(group):
            relayout((gidx - 1) * group + u)
        for u in range(group):
            build(gidx * group + u)
        return carry

    lax.fori_loop(1, n_groups, step, 0)
    for u in range(group):
        relayout((n_groups - 1) * group + u)


def _peer_gates(i1, i2, gate, *, rows):
    t, sel = i1.shape
    row = pl.BlockSpec((rows, sel), lambda i: (i, 0))
    return pl.pallas_call(
        _peer_gates_kernel,
        grid=(t // rows,),
        in_specs=[row, row, row],
        out_specs=pl.BlockSpec((rows, N_KEYS * N_KEYS), lambda i: (i, 0)),
        out_shape=jax.ShapeDtypeStruct((t, N_KEYS * N_KEYS), BF16),
        scratch_shapes=[pltpu.VMEM((rows // 2 * GATE_PITCH, N_KEYS), jnp.uint32)],
        compiler_params=_params("parallel"),
        name="peer_gates",
    )(i1, i2, gate)


def _peer_dense_kernel(xn_ref, gates_ref, u_ref, v_ref, x_ref, g_ref, o_ref, acc):
    k = pl.program_id(1)

    @pl.when(k == 0)
    def _():
        acc[...] = jnp.zeros_like(acc)

    h = _dot_nt(xn_ref[...], u_ref[...])
    a = (_gelu(h) * gates_ref[...].astype(F32)).astype(BF16)
    acc[...] += _dot(a, v_ref[...])

    @pl.when(k == pl.num_programs(1) - 1)
    def _():
        o_ref[...] = _rms(x_ref[...] + acc[...], g_ref[...])


def _peer_dense(xn, gates, u, v, x, g, *, rows, experts):
    t, d = x.shape
    n_exp = u.shape[0]
    tok = pl.BlockSpec((rows, d), lambda i, k: (i, 0))
    tab = pl.BlockSpec((experts, d), lambda i, k: (k, 0))
    return pl.pallas_call(
        _peer_dense_kernel,
        grid=(t // rows, n_exp // experts),
        in_specs=[tok, pl.BlockSpec((rows, experts), lambda i, k: (i, k)), tab, tab, tok, _full(g.shape)],
        out_specs=tok,
        out_shape=jax.ShapeDtypeStruct((t, d), F32),
        scratch_shapes=[pltpu.VMEM((rows, d), F32)],
        compiler_params=_params("parallel", "arbitrary"),
        name="peer_dense",
    )(xn, gates, u, v, x, g)


def _peer(x, g_ffn, wh, wl, kh, kl, u, v, g_final):
    t = x.shape[0]
    xn, i1, i2, gate = _peer_route(x, g_ffn, wh, wl, kh, kl)
    gates = _peer_gates(i1, i2, gate, rows=LANES)
    return _peer_dense(xn, gates, u, v, x, g_final, rows=min(512, t), experts=2048)


def _head_perm():
    new = np.zeros(ATTN_WIDTH, np.int32)
    for h in range(N_Q_HEADS):
        dst = (h % 4) * LANES + (h // 4) * HEAD_DIM
        new[dst:dst + HEAD_DIM] = np.arange(h * HEAD_DIM, (h + 1) * HEAD_DIM)
    return new


def _rope_tables(pos):
    half = HEAD_DIM // 2
    inv = ROPE_THETA ** (-jnp.arange(half, dtype=F32) / half)
    ang = pos.astype(F32)[:, None] * inv[None, :]
    cos, sin = jnp.cos(ang), jnp.sin(ang)
    reps = LANES // HEAD_DIM
    return (jnp.tile(jnp.concatenate([cos, cos], axis=1), (1, reps)),
            jnp.tile(jnp.concatenate([-sin, sin], axis=1), (1, reps)))


def kernel(x_prompt, x_sample, mem_prompt, cache_swa_k, cache_swa_v, cache_mem_k, cache_mem_v, g_mix, w_in, attn_sinks, g_sgu, w_spatial, b_spatial, g_attn_out, g_gmlp_out, w_out, g_cross, g_mem, w_cq, w_mk, w_mv, w_co, g_ffn, w_peer_q, peer_sub_keys, peer_u, peer_v, g_final):
    batch, seq, d = x_prompt.shape
    dec_batch, dec_seq, _ = x_sample.shape
    depth = g_mix.shape[0]
    n_mem = mem_prompt.shape[1]
    assert seq % WINDOW == 0 and dec_seq == SUBLANES and cache_swa_k.shape[2] == WINDOW

    perm = _head_perm()
    cos_p, sin_p = _rope_tables(jnp.arange(seq, dtype=I32))
    cos_s, sin_s = _rope_tables(PAST_LEN +(jnp.arange(dec_batch * dec_seq, dtype=I32) % dec_seq))
    seg = jnp.asarray(np.kron(np.eye(N_GMLP_HEADS), np.full((HEAD_DIM, HEAD_DIM), 1.0 / HEAD_DIM)), BF16)
    row2 = lambda a: a.reshape(1, -1)

    xp = x_prompt.reshape(batch * seq, d)
    xs = x_sample.reshape(dec_batch * dec_seq, d)
    outs = {name: [] for name in ("kp", "vp", "mk", "mv", "ks", "vs", "gvs")}
    for l in range(depth):
        w_in_l = jnp.concatenate([w_in[l][:, :ATTN_WIDTH][:, perm], w_in[l][:, ATTN_WIDTH:]], axis=1).astype(BF16)
        w_out_l = jnp.concatenate([w_out[l][:ATTN_WIDTH][perm], w_out[l][ATTN_WIDTH:]], axis=0).astype(BF16)
        ga = row2(g_attn_out[l][perm])
        gg = row2(g_gmlp_out[l])
        gs = row2(g_sgu[l])
        bias = jnp.repeat(b_spatial[l].T, HEAD_DIM, axis=1)
        wt = jnp.tril(w_spatial[l][:, :dec_seq, :dec_seq])
        ctab = jnp.repeat(jnp.transpose(wt, (2, 1, 0)), HEAD_DIM, axis=2)
        sink_tab = jnp.broadcast_to(jnp.repeat(attn_sinks[l], dec_seq)[:, None], (N_Q_HEADS * dec_seq, LANES))

        mix = functools.partial(_mix_in, g=row2(g_mix[l]), w_in=w_in_l, gs=gs, seg=seg)
        rows_p = 512 if seq % 512 == 0 else WINDOW
        qp, kp, vp, gup, gvp = mix(xp, cosf=cos_p, sinf=sin_p, rows=rows_p, table_blocks=seq // rows_p)
        rows_s = min(512, dec_batch * dec_seq)
        qs, ks, vs, gus, gvs = mix(xs, cosf=cos_s, sinf=sin_s, rows=rows_s,
                                   table_blocks=dec_batch * dec_seq // rows_s)

        xp = _mix_prompt(attn_sinks[l], qp, kp, vp, gup, gvp, xp, w_spatial[l], bias, ga, gg, w_out_l,
                         batch=batch, seq=seq)
        ck = cache_swa_k[l].reshape(dec_batch, WINDOW, N_KV_HEADS * HEAD_DIM)
        cv = cache_swa_v[l].reshape(dec_batch, WINDOW, N_KV_HEADS * HEAD_DIM)
        xs, nks, nvs = _mix_sample(qs, ks, vs, gus, gvs, xs, ck, cv, sink_tab, ctab, bias[:dec_seq], ga, gg,
                                   w_out_l, group=min(16, dec_batch))

        w_kv = jnp.concatenate([w_mk[l], w_mv[l]], axis=1).astype(BF16)
        mk, mv = _mem_kv(mem_prompt.reshape(batch * n_mem, d), row2(g_mem[l]), w_kv, rows=n_mem)
        wq = w_cq[l].astype(BF16)
        wo = w_co[l].astype(BF16)
        gc = row2(g_cross[l])
        xp = _mem_attn_prompt(xp, gc, wq, mk, mv, wo, batch=batch, seq=seq, rows=256 if seq % 256 == 0 else WINDOW)
        cmk = cache_mem_k[l].reshape(dec_batch, n_mem, MEM_HEADS * MEM_HEAD_DIM)
        cmv = cache_mem_v[l].reshape(dec_batch, n_mem, MEM_HEADS * MEM_HEAD_DIM)
        xs = _mem_attn_sample(xs, gc, wq, cmk, cmv, wo, group=min(8, dec_batch))

        wh, wl = _split(w_peer_q[l])
        kh, kl = _split(peer_sub_keys[l].reshape(2 * PEER_HEADS, N_KEYS, D_HALF))
        u = peer_u[l].astype(BF16)
        v = peer_v[l].astype(BF16)
        last = l == depth - 1
        assert last, "stacked layers need an un-normalised PEER output"
        peer = functools.partial(_peer, g_ffn=row2(g_ffn[l]), wh=wh, wl=wl, kh=kh, kl=kl, u=u, v=v,
                                 g_final=row2(g_final))
        xp = peer(xp)
        xs = peer(xs)

        outs["kp"].append(kp.reshape(batch, seq, N_KV_HEADS, HEAD_DIM)[:, seq - WINDOW:])
        outs["vp"].append(vp.reshape(batch, seq, N_KV_HEADS, HEAD_DIM)[:, seq - WINDOW:])
        outs["mk"].append(mk.reshape(batch, n_mem, MEM_HEADS, MEM_HEAD_DIM))
        outs["mv"].append(mv.reshape(batch, n_mem, MEM_HEADS, MEM_HEAD_DIM))
        outs["ks"].append(nks.reshape(dec_batch, WINDOW, N_KV_HEADS, HEAD_DIM))
        outs["vs"].append(nvs.reshape(dec_batch, WINDOW, N_KV_HEADS, HEAD_DIM))
        outs["gvs"].append(gvs.reshape(dec_batch, dec_seq, N_GMLP_HEADS, HEAD_DIM))

    stack = lambda name: jnp.stack(outs[name])
    return (xp.reshape(batch, seq, d), xs.reshape(dec_batch, dec_seq, d),
            stack("kp"), stack("vp"), stack("mk"), stack("mv"), stack("ks"), stack("vs"), stack("gvs"))
```

```python
import functools

import jax
import jax.numpy as jnp
import numpy as np
from jax import lax
from jax.experimental import pallas as pl
from jax.experimental.pallas import tpu as pltpu

F32 = jnp.float32
BF16 = jnp.bfloat16
I32 = jnp.int32

LANES = 128
SUBLANES = 8
VMEM_LIMIT_BYTES = 56 * 1024 * 1024

HEAD_DIM = 64
N_Q_HEADS = 8
N_KV_HEADS = 2
ATTN_WIDTH = N_Q_HEADS * HEAD_DIM
WINDOW = 128
PAST_LEN = 8192
ROPE_THETA = 10000.0
N_GMLP_HEADS = 8
GMLP_WIDTH = N_GMLP_HEADS * HEAD_DIM
CHUNK = 128
MEM_HEADS = 4
MEM_HEAD_DIM = 64
PEER_HEADS = 8
N_KEYS = 128
PEER_TOPK = 16
D_HALF = 128
EPS = 1e-6
NEG_INF = -1e30
SQRT_HALF = float(np.sqrt(0.5))

GATE_PITCH = N_KEYS + SUBLANES
GATE_TILES_PER_STEP = 2

NT_DIMS = (((1,), (1,)), ((), ()))


def _params(*semantics):
    return pltpu.CompilerParams(dimension_semantics=semantics, vmem_limit_bytes=VMEM_LIMIT_BYTES)


def _full(shape):
    zeros = (0,) * len(shape)
    return pl.BlockSpec(shape, lambda *_: zeros)


def _rms(x, g):
    return x * lax.rsqrt(jnp.mean(x * x, axis=-1, keepdims=True) + EPS) * g


def _gelu(x):
    return 0.5 * x * (1.0 + lax.erf(x * SQRT_HALF))


def _dot(a, b):
    return jnp.dot(a, b, preferred_element_type=F32)


def _dot_nt(a, b):
    return lax.dot_general(a, b, NT_DIMS, preferred_element_type=F32)


def _split(x):
    hi = x.astype(BF16)
    return hi, (x - hi.astype(F32)).astype(BF16)


def _half_masks():
    lane = lax.broadcasted_iota(I32, (1, LANES), 1)
    low = lane < HEAD_DIM
    return low, low.astype(F32), 1.0 - low.astype(F32)


def _mix_in_kernel(x_ref, g_ref, w_ref, cos_ref, sin_ref, gs_ref, seg_ref,
                   q_ref, k_ref, v_ref, gu_ref, gv_ref):
    xn = _rms(x_ref[...], g_ref[...]).astype(BF16)
    proj = _dot(xn, w_ref[...])
    cosf = cos_ref[...]
    sinf = sin_ref[...]
    lane = lax.broadcasted_iota(I32, (1, LANES), 1)
    first_half = (lane & (HEAD_DIM - 1)) < HEAD_DIM // 2

    def rope(c):
        partner = jnp.where(first_half, pltpu.roll(c, LANES - HEAD_DIM // 2, 1),
                            pltpu.roll(c, HEAD_DIM // 2, 1))
        return c * cosf + partner * sinf

    for c in range(ATTN_WIDTH // LANES):
        sl = slice(c * LANES, (c + 1) * LANES)
        q_ref[:, sl] = rope(proj[:, sl]) * (HEAD_DIM ** -0.5)
    k_ref[...] = rope(proj[:, 512:640])
    v_ref[...] = proj[:, 640:768]
    gu_ref[...] = _gelu(proj[:, 768:1280])
    gv = _gelu(proj[:, 1280:1792])
    hi, lo = _split(gv * gv)
    seg = seg_ref[...]
    ms = _dot(hi, seg) + _dot(lo, seg)
    gv_ref[...] = gv * lax.rsqrt(ms + EPS) * gs_ref[...]


def _mix_in(x, g, w_in, cosf, sinf, gs, seg, *, rows, table_blocks):
    t = x.shape[0]
    d = x.shape[1]
    row = lambda width: pl.BlockSpec((rows, width), lambda i: (i, 0))
    tab = pl.BlockSpec((rows, LANES), lambda i: (i % table_blocks, 0))
    return pl.pallas_call(
        _mix_in_kernel,
        grid=(t // rows,),
        in_specs=[row(d), _full(g.shape), _full(w_in.shape), tab, tab, _full(gs.shape), _full(seg.shape)],
        out_specs=[row(512), row(128), row(128), row(512), row(512)],
        out_shape=[jax.ShapeDtypeStruct((t, 512), F32), jax.ShapeDtypeStruct((t, 128), F32),
                   jax.ShapeDtypeStruct((t, 128), F32), jax.ShapeDtypeStruct((t, 512), F32),
                   jax.ShapeDtypeStruct((t, 512), F32)],
        compiler_params=_params("parallel"),
        name="mix_in",
    )(x, g, w_in, cosf, sinf, gs, seg)


def _stack_heads(q, lowf, highf):
    parts = [(q[:, c * LANES:(c + 1) * LANES] * m).astype(BF16)
             for m in (lowf, highf) for c in range(4)]
    return jnp.concatenate(parts, axis=0)


def _sink_softmax(s, sink):
    m = jnp.maximum(jnp.max(s, axis=-1, keepdims=True), sink)
    p = jnp.exp(s - m)
    den = jnp.sum(p, axis=-1, keepdims=True) + jnp.exp(sink - m)
    return (p / den).astype(BF16)


def _merge_out(attn, sgu, ga, gg, wo, x):
    merged = jnp.concatenate([_rms(attn, ga), _rms(sgu, gg)], axis=1).astype(BF16)
    return x + _dot(merged, wo)


def _mix_prompt_kernel(sinks_ref, q_ref, kp_ref, kc_ref, vp_ref, vc_ref, gu_ref, gv_ref, x_ref,
                       ws_ref, bias_ref, ga_ref, gg_ref, wo_ref, o_ref):
    n = pl.program_id(1)
    low, lowf, highf = _half_masks()
    blk = WINDOW
    qs = _stack_heads(q_ref[...], lowf, highf)
    kb = jnp.concatenate([kp_ref[...], kc_ref[...]], axis=0).astype(BF16)
    vb = jnp.concatenate([vp_ref[...], vc_ref[...]], axis=0).astype(BF16)
    s_all = _dot_nt(qs, kb)
    i = lax.broadcasted_iota(I32, (blk, 2 * blk), 0)
    j = lax.broadcasted_iota(I32, (blk, 2 * blk), 1)
    first_key = jnp.where(n > 0, 0, blk)
    valid = (j > i) & (j <= i + blk) & (j >= first_key)
    probs = []
    for h in range(N_Q_HEADS):
        s = jnp.where(valid, s_all[h * blk:(h + 1) * blk], NEG_INF)
        probs.append(_sink_softmax(s, sinks_ref[h]))
    o_all = _dot(jnp.concatenate(probs, axis=0), vb)
    attn = jnp.concatenate(
        [jnp.where(low, o_all[c * blk:(c + 1) * blk], o_all[(4 + c) * blk:(5 + c) * blk])
         for c in range(4)], axis=1)

    gvb = gv_ref[...].astype(BF16)
    r = lax.broadcasted_iota(I32, (CHUNK, CHUNK), 0)
    c_ = lax.broadcasted_iota(I32, (CHUNK, CHUNK), 1)
    tril = r >= c_
    mixed = []
    for c in range(4):
        g = gvb[:, c * LANES:(c + 1) * LANES]
        y0 = _dot(jnp.where(tril, ws_ref[2 * c], 0.0).astype(BF16), g)
        y1 = _dot(jnp.where(tril, ws_ref[2 * c + 1], 0.0).astype(BF16), g)
        mixed.append(jnp.where(low, y0, y1))
    sgu = gu_ref[...] * (jnp.concatenate(mixed, axis=1) + bias_ref[...])
    o_ref[...] = _merge_out(attn, sgu, ga_ref[...], gg_ref[...], wo_ref[...], x_ref[...])


def _mix_prompt(sinks, q, k, v, gu, gv, x, ws, bias, ga, gg, wo, *, batch, seq):
    nb = seq // WINDOW
    cur = lambda width: pl.BlockSpec((WINDOW, width), lambda b, n: (b * nb + n, 0))
    prev = lambda width: pl.BlockSpec((WINDOW, width), lambda b, n: (b * nb + jnp.maximum(n - 1, 0), 0))
    return pl.pallas_call(
        _mix_prompt_kernel,
        grid=(batch, nb),
        in_specs=[pl.BlockSpec(memory_space=pltpu.SMEM),
                  cur(512), prev(128), cur(128), prev(128), cur(128), cur(512), cur(512), cur(x.shape[1]),
                  _full(ws.shape), _full(bias.shape), _full(ga.shape), _full(gg.shape), _full(wo.shape)],
        out_specs=cur(x.shape[1]),
        out_shape=jax.ShapeDtypeStruct(x.shape, F32),
        compiler_params=_params("parallel", "parallel"),
        name="mix_prompt",
    )(sinks, q, k, k, v, v, gu, gv, x, ws, bias, ga, gg, wo)


def _mix_sample_kernel(q_ref, kn_ref, vn_ref, gu_ref, gv_ref, x_ref, ck_ref, cv_ref,
                       sink_ref, ctab_ref, bias_ref, ga_ref, gg_ref, wo_ref,
                       o_ref, nk_ref, nv_ref, attn_s, sgu_s):
    nseq = ck_ref.shape[0]
    t = q_ref.shape[0] // nseq
    w = ck_ref.shape[1]
    low, lowf, highf = _half_masks()
    rows = N_Q_HEADS * t
    band = 2 * w
    i = lax.broadcasted_iota(I32, (rows, band), 0) & (t - 1)
    j = lax.broadcasted_iota(I32, (rows, band), 1)
    valid = (j > i) & (j <= i + w)
    sink = sink_ref[:, 0:1]
    pad = jnp.zeros((band - w - t, LANES), F32)

    def body(b, carry):
        r0 = pl.multiple_of(b * t, t)
        rs = pl.ds(r0, t)
        qs = _stack_heads(q_ref[rs, :], lowf, highf)
        kc, kn = ck_ref[b], kn_ref[rs, :]
        vc, vn = cv_ref[b], vn_ref[rs, :]
        kb = jnp.concatenate([kc, kn, pad], axis=0).astype(BF16)
        vb = jnp.concatenate([vc, vn, pad], axis=0).astype(BF16)
        s = jnp.where(valid, _dot_nt(qs, kb), NEG_INF)
        o_all = _dot(_sink_softmax(s, sink), vb)
        for c in range(4):
            attn_s[rs, c * LANES:(c + 1) * LANES] = jnp.where(
                low, o_all[c * t:(c + 1) * t], o_all[(4 + c) * t:(5 + c) * t])
        nk_ref[b] = jnp.concatenate([kc[t:], kn], axis=0)
        nv_ref[b] = jnp.concatenate([vc[t:], vn], axis=0)
        gvb = gv_ref[rs, :]
        mixed = ctab_ref[0] * gvb[0:1, :]
        for r in range(1, t):
            mixed = mixed + ctab_ref[r] * gvb[r:r + 1, :]
        sgu_s[rs, :] = gu_ref[rs, :] * (mixed + bias_ref[...])
        return carry

    lax.fori_loop(0, nseq, body, 0)
    o_ref[...] = _merge_out(attn_s[...], sgu_s[...], ga_ref[...], gg_ref[...], wo_ref[...], x_ref[...])


def _mix_sample(q, k, v, gu, gv, x, ck, cv, sink_tab, ctab, bias, ga, gg, wo, *, group):
    nseq, w, _ = ck.shape
    t = x.shape[0] // nseq
    assert t == SUBLANES and w == WINDOW and nseq % group == 0
    rows = group * t
    row = lambda width: pl.BlockSpec((rows, width), lambda i: (i, 0))
    cache = pl.BlockSpec((group, w, LANES), lambda i: (i, 0, 0))
    return pl.pallas_call(
        _mix_sample_kernel,
        grid=(nseq // group,),
        in_specs=[row(512), row(128), row(128), row(512), row(512), row(x.shape[1]), cache, cache,
                  _full(sink_tab.shape), _full(ctab.shape), _full(bias.shape),
                  _full(ga.shape), _full(gg.shape), _full(wo.shape)],
        out_specs=[row(x.shape[1]), cache, cache],
        out_shape=[jax.ShapeDtypeStruct(x.shape, F32), jax.ShapeDtypeStruct(ck.shape, F32),
                   jax.ShapeDtypeStruct(cv.shape, F32)],
        scratch_shapes=[pltpu.VMEM((rows, 512), F32), pltpu.VMEM((rows, 512), F32)],
        compiler_params=_params("parallel"),
        name="mix_sample",
    )(q, k, v, gu, gv, x, ck, cv, sink_tab, ctab, bias, ga, gg, wo)


def _mem_kv_kernel(m_ref, g_ref, w_ref, k_ref, v_ref):
    mn = _rms(m_ref[...], g_ref[...]).astype(BF16)
    kv = _dot(mn, w_ref[...])
    half = kv.shape[1] // 2
    k_ref[...] = kv[:, :half]
    v_ref[...] = kv[:, half:]


def _mem_kv(mem, g, w_kv, *, rows):
    t, d = mem.shape
    width = w_kv.shape[1] // 2
    row = lambda wd: pl.BlockSpec((rows, wd), lambda i: (i, 0))
    return pl.pallas_call(
        _mem_kv_kernel,
        grid=(t // rows,),
        in_specs=[row(d), _full(g.shape), _full(w_kv.shape)],
        out_specs=[row(width), row(width)],
        out_shape=[jax.ShapeDtypeStruct((t, width), F32)] * 2,
        compiler_params=_params("parallel"),
        name="mem_kv",
    )(mem, g, w_kv)


def _softmax(s):
    p = jnp.exp(s - jnp.max(s, axis=-1, keepdims=True))
    return (p / jnp.sum(p, axis=-1, keepdims=True)).astype(BF16)


def _mem_heads(q, mk, mv, low, lowf, highf):
    rows = q.shape[0]
    outs = []
    for c in range(MEM_HEADS // 2):
        sl = slice(c * LANES, (c + 1) * LANES)
        qc = q[:, sl]
        q2 = jnp.concatenate([(qc * lowf).astype(BF16), (qc * highf).astype(BF16)], axis=0)
        o = _dot(_softmax(_dot_nt(q2, mk[:, sl].astype(BF16))), mv[:, sl].astype(BF16))
        outs.append(jnp.where(low, o[:rows], o[rows:]))
    return jnp.concatenate(outs, axis=1)


def _mem_attn_prompt_kernel(x_ref, g_ref, wq_ref, mk_ref, mv_ref, wo_ref, o_ref):
    low, lowf, highf = _half_masks()
    x = x_ref[...]
    q = _dot(_rms(x, g_ref[...]).astype(BF16), wq_ref[...]) * (MEM_HEAD_DIM ** -0.5)
    o = _mem_heads(q, mk_ref[...], mv_ref[...], low, lowf, highf)
    o_ref[...] = x + _dot(o.astype(BF16), wo_ref[...])


def _mem_attn_prompt(x, g, wq, mk, mv, wo, *, batch, seq, rows):
    n_mem = mk.shape[0] // batch
    nb = seq // rows
    row = pl.BlockSpec((rows, x.shape[1]), lambda b, n: (b * nb + n, 0))
    mem = pl.BlockSpec((n_mem, mk.shape[1]), lambda b, n: (b, 0))
    return pl.pallas_call(
        _mem_attn_prompt_kernel,
        grid=(batch, nb),
        in_specs=[row, _full(g.shape), _full(wq.shape), mem, mem, _full(wo.shape)],
        out_specs=row,
        out_shape=jax.ShapeDtypeStruct(x.shape, F32),
        compiler_params=_params("parallel", "parallel"),
        name="mem_attn_prompt",
    )(x, g, wq, mk, mv, wo)


def _mem_attn_sample_kernel(x_ref, g_ref, wq_ref, mk_ref, mv_ref, wo_ref, o_ref, q_s, a_s):
    nseq = mk_ref.shape[0]
    t = x_ref.shape[0] // nseq
    low, lowf, highf = _half_masks()
    x = x_ref[...]
    q_s[...] = _dot(_rms(x, g_ref[...]).astype(BF16), wq_ref[...]) * (MEM_HEAD_DIM ** -0.5)

    def body(b, carry):
        rs = pl.ds(pl.multiple_of(b * t, t), t)
        a_s[rs, :] = _mem_heads(q_s[rs, :], mk_ref[b], mv_ref[b], low, lowf, highf)
        return carry

    lax.fori_loop(0, nseq, body, 0)
    o_ref[...] = x + _dot(a_s[...].astype(BF16), wo_ref[...])


def _mem_attn_sample(x, g, wq, mk, mv, wo, *, group):
    nseq, n_mem, width = mk.shape
    t = x.shape[0] // nseq
    rows = group * t
    row = pl.BlockSpec((rows, x.shape[1]), lambda i: (i, 0))
    mem = pl.BlockSpec((group, n_mem, width), lambda i: (i, 0, 0))
    return pl.pallas_call(
        _mem_attn_sample_kernel,
        grid=(nseq // group,),
        in_specs=[row, _full(g.shape), _full(wq.shape), mem, mem, _full(wo.shape)],
        out_specs=row,
        out_shape=jax.ShapeDtypeStruct(x.shape, F32),
        scratch_shapes=[pltpu.VMEM((rows, width), F32), pltpu.VMEM((rows, width), F32)],
        compiler_params=_params("parallel"),
        name="mem_attn_sample",
    )(x, g, wq, mk, mv, wo)


def _top16(s):
    nrows, cols = s.shape
    r = lax.broadcasted_iota(I32, (nrows, cols), 0).astype(F32)
    slot = lax.broadcasted_iota(I32, (PEER_TOPK, cols), 0)
    vals = jnp.zeros((PEER_TOPK, cols), F32)
    idxs = jnp.zeros((PEER_TOPK, cols), F32)
    for p in range(PEER_TOPK):
        m = jnp.max(s, axis=0, keepdims=True)
        am = jnp.min(jnp.where(s == m, r, float(nrows)), axis=0, keepdims=True)
        vals = jnp.where(slot == p, m, vals)
        idxs = jnp.where(slot == p, am, idxs)
        s = jnp.where(r == am, -jnp.inf, s)
    return vals, idxs


def _sort16_pairs():
    def merge(lo, hi, r):
        step = r * 2
        if step < hi - lo:
            yield from merge(lo, hi, step)
            yield from merge(lo + r, hi, step)
            yield from [(i, i + r) for i in range(lo + r, hi - r, step)]
        else:
            yield (lo, lo + r)

    def sort(lo, hi):
        if hi > lo:
            mid = lo + (hi - lo) // 2
            yield from sort(lo, mid)
            yield from sort(mid + 1, hi)
            yield from merge(lo, hi, 1)

    return tuple(sort(0, PEER_TOPK - 1))


SORT16 = _sort16_pairs()


def _cmpx(v, pays, i, j):
    swap = v[j] > v[i]
    hi, lo = jnp.maximum(v[i], v[j]), jnp.minimum(v[i], v[j])
    for p in pays:
        p[i], p[j] = jnp.where(swap, p[j], p[i]), jnp.where(swap, p[i], p[j])
    v[i], v[j] = hi, lo


def _merge16(va, pa, vb, pb):
    n = PEER_TOPK
    v, pays = [], [[] for _ in pa]
    for i in range(n):
        other = vb[n - 1 - i]
        if other is None:
            v.append(va[i])
            for k in range(len(pa)):
                pays[k].append(pa[k][i])
            continue
        take = other > va[i]
        v.append(jnp.maximum(va[i], other))
        for k in range(len(pa)):
            pays[k].append(jnp.where(take, pb[k][n - 1 - i], pa[k][i]))
    d = n // 2
    while d:
        for i in range(n):
            if not i & d:
                _cmpx(v, pays, i, i + d)
        d //= 2
    return v, pays


def _xor_rows(x, d, sub):
    if d == SUBLANES // 2:
        return pltpu.roll(x, d, 0)
    return jnp.where((sub & d) == 0, pltpu.roll(x, SUBLANES - d, 0), pltpu.roll(x, d, 0))


def _row(x, k):
    return jnp.broadcast_to(x[k:k + 1, :], x.shape)


def _route_sorted(s_s, rows, i1_s, i2_s, gt_s):
    n = PEER_TOPK
    sub = lax.broadcasted_iota(I32, (SUBLANES, LANES), 0)
    subf = sub.astype(F32)
    bit2 = (sub & 4) == 0
    bit1 = (sub & 2) == 0
    halves = [slice(0, LANES), slice(LANES, 2 * LANES)]
    bad = jnp.zeros((SUBLANES, LANES), F32)

    def packed_merge(mask, d, xa, xb):
        va, pa = xa
        vb, pb = xb
        lv = [jnp.where(mask, a, b) for a, b in zip(va, vb)]
        rv = [_xor_rows(jnp.where(mask, b, a), d, sub) for a, b in zip(va, vb)]
        lp = [[jnp.where(mask, a, b) for a, b in zip(qa, qb)] for qa, qb in zip(pa, pb)]
        rp = [[_xor_rows(jnp.where(mask, b, a), d, sub) for a, b in zip(qa, qb)] for qa, qb in zip(pa, pb)]
        return _merge16(lv, lp, rv, rp)

    def self_merge(d, x):
        v, p = x
        return _merge16(v, p, [_xor_rows(a, d, sub) for a in v], [[_xor_rows(a, d, sub) for a in q] for q in p])

    def decreasing(v):
        ok = v[0] > v[1]
        flag = jnp.where(ok, 0.0, 1.0)
        for r in range(1, n - 1):
            flag = jnp.where(v[r] > v[r + 1], flag, 1.0)
        return flag

    sorted_lists = {}
    for c in range(2):
        for part in range(2):
            v = [s_s[c, SUBLANES * g:SUBLANES * (g + 1), halves[part]] for g in range(N_KEYS // SUBLANES)]
            ix = [subf + float(SUBLANES * g) for g in range(N_KEYS // SUBLANES)]
            pays = [ix]
            for i, j in SORT16:
                _cmpx(v, pays, i, j)
            sorted_lists[c, part] = (v, pays)
    x1 = [packed_merge(bit2, 4, sorted_lists[c, 0], sorted_lists[c, 1]) for c in range(2)]
    x2 = packed_merge(bit1, 2, x1[0], x1[1])
    tv, (ti,) = self_merge(1, x2)
    bad = jnp.maximum(bad, decreasing(tv))
    for c in range(2):
        for part in range(2):
            thr = _row(tv[n - 1], 4 * part + 2 * c)
            cnt = jnp.zeros((SUBLANES, LANES), F32)
            for g in range(N_KEYS // SUBLANES):
                cnt = cnt + jnp.where(s_s[c, SUBLANES * g:SUBLANES * (g + 1), halves[part]] >= thr, 1.0, 0.0)
            total = jnp.sum(cnt, axis=0, keepdims=True)
            bad = jnp.maximum(bad, jnp.broadcast_to(jnp.where(total == float(n), 0.0, 1.0), bad.shape))

    half = n // 2
    lists, tails = [], []
    for part in range(2):
        base = 4 * part
        s1 = [_row(tv[r], base) for r in range(n)]
        a1 = [_row(ti[r], base) for r in range(n)]
        s2 = [_row(tv[r], base + 2) for r in range(n)]
        a2 = [_row(ti[r], base + 2) for r in range(n)]
        p1, ip1 = s1[0], a1[0]
        for p in range(1, half):
            p1 = jnp.where(sub == p, s1[p], p1)
            ip1 = jnp.where(sub == p, a1[p], ip1)
        lists.append(([p1 + s2[q] for q in range(n)], [[ip1] * n, list(a2)]))
        tails.append(([s1[half + i] + s2[0] for i in range(half)],
                      [[a1[half + i] for i in range(half)], [a2[0]] * half]))
    y = packed_merge(bit2, 4, lists[0], lists[1])
    y = self_merge(2, y)
    y = self_merge(1, y)
    dv = [jnp.where(bit2, a, b) for a, b in zip(tails[0][0], tails[1][0])] + [None] * half
    dp = [[jnp.where(bit2, a, b) for a, b in zip(qa, qb)] + [None] * half
          for qa, qb in zip(tails[0][1], tails[1][1])]
    top, (sel1, sel2) = _merge16(y[0], y[1], dv, dp)
    bad = jnp.maximum(bad, decreasing(top))
    for part in range(2):
        thr = _row(top[n - 1], 4 * part)
        cnt = jnp.zeros((SUBLANES, LANES), F32)
        for q in range(n):
            cnt = cnt + jnp.where(lists[part][0][q] >= thr, 1.0, 0.0)
        total = jnp.sum(cnt, axis=0, keepdims=True)
        for i in range(half):
            total = total + jnp.where(tails[part][0][i][0:1, :] >= thr[0:1, :], 1.0, 0.0)
        bad = jnp.maximum(bad, jnp.broadcast_to(jnp.where(total == float(n), 0.0, 1.0), bad.shape))

    e = [jnp.exp(t - top[0]) for t in top]
    z = e[0]
    for k in range(1, n):
        z = z + e[k]
    gate = [ek / z for ek in e]
    for src, dst in ((sel1, i1_s), (sel2, i2_s), (gate, gt_s)):
        for grp in range(2):
            for part in range(2):
                tile = None
                for k in range(SUBLANES):
                    slab = src[grp * SUBLANES + k]
                    if (k < 4) != (part == 0):
                        slab = pltpu.roll(slab, 4, 0)
                    tile = slab if tile is None else jnp.where(sub == k, slab, tile)
                dst[rows[grp], halves[part]] = tile
    return jnp.max(bad)


def _peer_route_kernel(x_ref, g_ref, wh_ref, wl_ref, kh_ref, kl_ref,
                       xn_ref, i1_ref, i2_ref, gate_ref, q_s, s_s, i1_s, i2_s, gt_s):
    tb = x_ref.shape[0]
    xn = _rms(x_ref[...], g_ref[...])
    xh, xl = _split(xn)
    xn_ref[...] = xh
    wh = wh_ref[...]
    q = _dot(xh, wh) + _dot(xh, wl_ref[...]) + _dot(xl, wh)
    for hc in range(2 * PEER_HEADS):
        q_s[hc] = q[:, hc * D_HALF:(hc + 1) * D_HALF]

    def head(h, carry):
        for c in range(2):
            qh, ql = _split(q_s[2 * h + c])
            kh = kh_ref[2 * h + c]
            s_s[c] = _dot_nt(kh, qh) + _dot_nt(kh, ql) + _dot_nt(kl_ref[2 * h + c], qh)
        rows = [pl.ds(pl.multiple_of(h * PEER_TOPK + k, SUBLANES), SUBLANES) for k in (0, SUBLANES)]
        tied = _route_sorted(s_s, rows, i1_s, i2_s, gt_s)

        @pl.when(tied > 0.0)
        def _():
            for part in range(tb // LANES):
                head_part(h, part)

        return carry

    def head_part(h, part):
        cols = slice(part * LANES, (part + 1) * LANES)
        (s1, i1), (s2, i2) = [_top16(s_s[c, :, cols]) for c in range(2)]
        tb = LANES
        half = PEER_TOPK // 2
        cand = jnp.concatenate(
            [jnp.broadcast_to(s1[0:1], (PEER_TOPK, tb)) + s2]
            + [jnp.broadcast_to(s1[p:p + 1], (half, tb)) + s2[:half] for p in range(1, half)]
            + [s1[half:] + jnp.broadcast_to(s2[0:1], (half, tb))], axis=0)
        top, cidx = _top16(cand)
        ci = cidx.astype(I32)
        mid = ci - PEER_TOPK
        tail = PEER_TOPK + half * (half - 1)
        pi = jnp.where(ci < PEER_TOPK, 0,
                       jnp.where(ci < tail, 1 + (mid >> (half.bit_length() - 1)), ci - tail + half))
        qi = jnp.where(ci < PEER_TOPK, ci, jnp.where(ci < tail, mid & (half - 1), 0))
        e1 = jnp.zeros((PEER_TOPK, tb), F32)
        e2 = jnp.zeros((PEER_TOPK, tb), F32)
        for p in range(PEER_TOPK):
            e1 = jnp.where(pi == p, jnp.broadcast_to(i1[p:p + 1], (PEER_TOPK, tb)), e1)
            e2 = jnp.where(qi == p, jnp.broadcast_to(i2[p:p + 1], (PEER_TOPK, tb)), e2)
        e = jnp.exp(top - jnp.max(top, axis=0, keepdims=True))
        rs = pl.ds(pl.multiple_of(h * PEER_TOPK, PEER_TOPK), PEER_TOPK)
        i1_s[rs, cols] = e1
        i2_s[rs, cols] = e2
        gt_s[rs, cols] = e / jnp.sum(e, axis=0, keepdims=True)

    lax.fori_loop(0, PEER_HEADS, head, 0)
    i1_ref[...] = i1_s[...].T.astype(I32)
    i2_ref[...] = i2_s[...].T.astype(I32)
    gate_ref[...] = gt_s[...].T


def _peer_route(x, g, wh, wl, kh, kl):
    t, d = x.shape
    tb = 2 * LANES
    assert t % tb == 0
    sel = PEER_HEADS * PEER_TOPK
    row = lambda width: pl.BlockSpec((tb, width), lambda i: (i, 0))
    return pl.pallas_call(
        _peer_route_kernel,
        grid=(t // tb,),
        in_specs=[row(d), _full(g.shape), _full(wh.shape), _full(wl.shape), _full(kh.shape), _full(kl.shape)],
        out_specs=[row(d), row(sel), row(sel), row(sel)],
        out_shape=[jax.ShapeDtypeStruct((t, d), BF16), jax.ShapeDtypeStruct((t, sel), I32),
                   jax.ShapeDtypeStruct((t, sel), I32), jax.ShapeDtypeStruct((t, sel), F32)],
        scratch_shapes=[pltpu.VMEM((2 * PEER_HEADS, tb, D_HALF), F32), pltpu.VMEM((2, N_KEYS, tb), F32),
                        pltpu.VMEM((sel, tb), F32), pltpu.VMEM((sel, tb), F32), pltpu.VMEM((sel, tb), F32)],
        compiler_params=_params("parallel"),
        name="peer_route",
    )(x, g, wh, wl, kh, kl)


def _peer_gates_kernel(i1_ref, i2_ref, gate_ref, o_ref, g_s):
    tb = i1_ref.shape[0]
    sel = i1_ref.shape[1]
    key = lax.broadcasted_iota(I32, (N_KEYS, sel), 0)
    zero = jnp.zeros((N_KEYS, sel), BF16)

    tile = 2 * SUBLANES

    def build(tidx):
        for half in range(2):
            t0 = pl.multiple_of(tidx * tile + half * SUBLANES, SUBLANES)
            i1g = i1_ref[pl.ds(t0, SUBLANES), :]
            i2g = i2_ref[pl.ds(t0, SUBLANES), :]
            gtg = gate_ref[pl.ds(t0, SUBLANES), :]
            w1 = [jnp.where(key == i1g[s:s + 1, :], gtg[s:s + 1, :], 0.0).astype(BF16) for s in range(SUBLANES)]
            w2 = [jnp.where(key == i2g[s:s + 1, :], 1.0, 0.0).astype(BF16) for s in range(SUBLANES)]
            for s in range(0, SUBLANES, 2):
                lhs = jnp.concatenate([w1[s], w1[s + 1]], axis=1)
                rhs = jnp.concatenate([jnp.concatenate([w2[s], zero], axis=1),
                                       jnp.concatenate([zero, w2[s + 1]], axis=1)], axis=0)
                g = _dot_nt(lhs, rhs)
                word = pltpu.pack_elementwise([g[:, :N_KEYS], g[:, N_KEYS:]], packed_dtype=BF16)
                pair = tidx * SUBLANES + (half * SUBLANES + s) // 2
                g_s[pl.ds(pl.multiple_of(pair * GATE_PITCH, SUBLANES), N_KEYS), :] = word

    def relayout(tidx):
        rows = pl.ds(pl.multiple_of(tidx * tile, tile), tile)
        for a in range(N_KEYS):
            words = g_s[pl.ds(tidx * SUBLANES * GATE_PITCH + a, SUBLANES, stride=GATE_PITCH), :]
            o_ref[rows, a * N_KEYS:(a + 1) * N_KEYS] = pltpu.bitcast(words, BF16)

    group = GATE_TILES_PER_STEP
    n_groups = tb // (tile * group)
    for u in range(group):
        build(u)

    def step(gidx, carry):
        for u in range(group):
            relayout((gidx - 1) * group + u)
        for u in range(group):
            build(gidx * group + u)
        return carry

    lax.fori_loop(1, n_groups, step, 0)
    for u in range(group):
        relayout((n_groups - 1) * group + u)


def _peer_gates(i1, i2, gate, *, rows):
    t, sel = i1.shape
    row = pl.BlockSpec((rows, sel), lambda i: (i, 0))
    return pl.pallas_call(
        _peer_gates_kernel,
        grid=(t // rows,),
        in_specs=[row, row, row],
        out_specs=pl.BlockSpec((rows, N_KEYS * N_KEYS), lambda i: (i, 0)),
        out_shape=jax.ShapeDtypeStruct((t, N_KEYS * N_KEYS), BF16),
        scratch_shapes=[pltpu.VMEM((rows // 2 * GATE_PITCH, N_KEYS), jnp.uint32)],
        compiler_params=_params("parallel"),
        name="peer_gates",
    )(i1, i2, gate)


def _peer_dense_kernel(xn_ref, gates_ref, u_ref, v_ref, x_ref, g_ref, o_ref, acc):
    k = pl.program_id(1)

    @pl.when(k == 0)
    def _():
        acc[...] = jnp.zeros_like(acc)

    h = _dot_nt(xn_ref[...], u_ref[...])
    a = (_gelu(h) * gates_ref[...].astype(F32)).astype(BF16)
    acc[...] += _dot(a, v_ref[...])

    @pl.when(k == pl.num_programs(1) - 1)
    def _():
        o_ref[...] = _rms(x_ref[...] + acc[...], g_ref[...])


def _peer_dense(xn, gates, u, v, x, g, *, rows, experts):
    t, d = x.shape
    n_exp = u.shape[0]
    tok = pl.BlockSpec((rows, d), lambda i, k: (i, 0))
    tab = pl.BlockSpec((experts, d), lambda i, k: (k, 0))
    return pl.pallas_call(
        _peer_dense_kernel,
        grid=(t // rows, n_exp // experts),
        in_specs=[tok, pl.BlockSpec((rows, experts), lambda i, k: (i, k)), tab, tab, tok, _full(g.shape)],
        out_specs=tok,
        out_shape=jax.ShapeDtypeStruct((t, d), F32),
        scratch_shapes=[pltpu.VMEM((rows, d), F32)],
        compiler_params=_params("parallel", "arbitrary"),
        name="peer_dense",
    )(xn, gates, u, v, x, g)


def _peer(x, g_ffn, wh, wl, kh, kl, u, v, g_final):
    t = x.shape[0]
    xn, i1, i2, gate = _peer_route(x, g_ffn, wh, wl, kh, kl)
    gates = _peer_gates(i1, i2, gate, rows=LANES)
    return _peer_dense(xn, gates, u, v, x, g_final, rows=min(512, t), experts=2048)


def _head_perm():
    new = np.zeros(ATTN_WIDTH, np.int32)
    for h in range(N_Q_HEADS):
        dst = (h % 4) * LANES + (h // 4) * HEAD_DIM
        new[dst:dst + HEAD_DIM] = np.arange(h * HEAD_DIM, (h + 1) * HEAD_DIM)
    return new


def _rope_tables(pos):
    half = HEAD_DIM // 2
    inv = ROPE_THETA ** (-jnp.arange(half, dtype=F32) / half)
    ang = pos.astype(F32)[:, None] * inv[None, :]
    cos, sin = jnp.cos(ang), jnp.sin(ang)
    reps = LANES // HEAD_DIM
    return (jnp.tile(jnp.concatenate([cos, cos], axis=1), (1, reps)),
            jnp.tile(jnp.concatenate([-sin, sin], axis=1), (1, reps)))


def kernel(x_prompt, x_sample, mem_prompt, cache_swa_k, cache_swa_v, cache_mem_k, cache_mem_v, g_mix, w_in, attn_sinks, g_sgu, w_spatial, b_spatial, g_attn_out, g_gmlp_out, w_out, g_cross, g_mem, w_cq, w_mk, w_mv, w_co, g_ffn, w_peer_q, peer_sub_keys, peer_u, peer_v, g_final):
    batch, seq, d = x_prompt.shape
    dec_batch, dec_seq, _ = x_sample.shape
    depth = g_mix.shape[0]
    n_mem = mem_prompt.shape[1]
    assert seq % WINDOW == 0 and dec_seq == SUBLANES and cache_swa_k.shape[2] == WINDOW

    perm = _head_perm()
    cos_p, sin_p = _rope_tables(jnp.arange(seq, dtype=I32))
    cos_s, sin_s = _rope_tables(PAST_LEN +(jnp.arange(dec_batch * dec_seq, dtype=I32) % dec_seq))
    seg = jnp.asarray(np.kron(np.eye(N_GMLP_HEADS), np.full((HEAD_DIM, HEAD_DIM), 1.0 / HEAD_DIM)), BF16)
    row2 = lambda a: a.reshape(1, -1)

    xp = x_prompt.reshape(batch * seq, d)
    xs = x_sample.reshape(dec_batch * dec_seq, d)
    outs = {name: [] for name in ("kp", "vp", "mk", "mv", "ks", "vs", "gvs")}
    for l in range(depth):
        w_in_l = jnp.concatenate([w_in[l][:, :ATTN_WIDTH][:, perm], w_in[l][:, ATTN_WIDTH:]], axis=1).astype(BF16)
        w_out_l = jnp.concatenate([w_out[l][:ATTN_WIDTH][perm], w_out[l][ATTN_WIDTH:]], axis=0).astype(BF16)
        ga = row2(g_attn_out[l][perm])
        gg = row2(g_gmlp_out[l])
        gs = row2(g_sgu[l])
        bias = jnp.repeat(b_spatial[l].T, HEAD_DIM, axis=1)
        wt = jnp.tril(w_spatial[l][:, :dec_seq, :dec_seq])
        ctab = jnp.repeat(jnp.transpose(wt, (2, 1, 0)), HEAD_DIM, axis=2)
        sink_tab = jnp.broadcast_to(jnp.repeat(attn_sinks[l], dec_seq)[:, None], (N_Q_HEADS * dec_seq, LANES))

        mix = functools.partial(_mix_in, g=row2(g_mix[l]), w_in=w_in_l, gs=gs, seg=seg)
        rows_p = 512 if seq % 512 == 0 else WINDOW
        qp, kp, vp, gup, gvp = mix(xp, cosf=cos_p, sinf=sin_p, rows=rows_p, table_blocks=seq // rows_p)
        rows_s = min(512, dec_batch * dec_seq)
        qs, ks, vs, gus, gvs = mix(xs, cosf=cos_s, sinf=sin_s, rows=rows_s,
                                   table_blocks=dec_batch * dec_seq // rows_s)

        xp = _mix_prompt(attn_sinks[l], qp, kp, vp, gup, gvp, xp, w_spatial[l], bias, ga, gg, w_out_l,
                         batch=batch, seq=seq)
        ck = cache_swa_k[l].reshape(dec_batch, WINDOW, N_KV_HEADS * HEAD_DIM)
        cv = cache_swa_v[l].reshape(dec_batch, WINDOW, N_KV_HEADS * HEAD_DIM)
        xs, nks, nvs = _mix_sample(qs, ks, vs, gus, gvs, xs, ck, cv, sink_tab, ctab, bias[:dec_seq], ga, gg,
                                   w_out_l, group=min(16, dec_batch))

        w_kv = jnp.concatenate([w_mk[l], w_mv[l]], axis=1).astype(BF16)
        mk, mv = _mem_kv(mem_prompt.reshape(batch * n_mem, d), row2(g_mem[l]), w_kv, rows=n_mem)
        wq = w_cq[l].astype(BF16)
        wo = w_co[l].astype(BF16)
        gc = row2(g_cross[l])
        xp = _mem_attn_prompt(xp, gc, wq, mk, mv, wo, batch=batch, seq=seq, rows=256 if seq % 256 == 0 else WINDOW)
        cmk = cache_mem_k[l].reshape(dec_batch, n_mem, MEM_HEADS * MEM_HEAD_DIM)
        cmv = cache_mem_v[l].reshape(dec_batch, n_mem, MEM_HEADS * MEM_HEAD_DIM)
        xs = _mem_attn_sample(xs, gc, wq, cmk, cmv, wo, group=min(8, dec_batch))

        wh, wl = _split(w_peer_q[l])
        kh, kl = _split(peer_sub_keys[l].reshape(2 * PEER_HEADS, N_KEYS, D_HALF))
        u = peer_u[l].astype(BF16)
        v = peer_v[l].astype(BF16)
        last = l == depth - 1
        assert last, "stacked layers need an un-normalised PEER output"
        peer = functools.partial(_peer, g_ffn=row2(g_ffn[l]), wh=wh, wl=wl, kh=kh, kl=kl, u=u, v=v,
                                 g_final=row2(g_final))
        xp = peer(xp)
        xs = peer(xs)

        outs["kp"].append(kp.reshape(batch, seq, N_KV_HEADS, HEAD_DIM)[:, seq - WINDOW:])
        outs["vp"].append(vp.reshape(batch, seq, N_KV_HEADS, HEAD_DIM)[:, seq - WINDOW:])
        outs["mk"].append(mk.reshape(batch, n_mem, MEM_HEADS, MEM_HEAD_DIM))
        outs["mv"].append(mv.reshape(batch, n_mem, MEM_HEADS, MEM_HEAD_DIM))
        outs["ks"].append(nks.reshape(dec_batch, WINDOW, N_KV_HEADS, HEAD_DIM))
        outs["vs"].append(nvs.reshape(dec_batch, WINDOW, N_KV_HEADS, HEAD_DIM))
        outs["gvs"].append(gvs.reshape(dec_batch, dec_seq, N_GMLP_HEADS, HEAD_DIM))

    stack = lambda name: jnp.stack(outs[name])
    return (xp.reshape(batch, seq, d), xs.reshape(dec_batch, dec_seq, d),
            stack("kp"), stack("vp"), stack("mk"), stack("mv"), stack("ks"), stack("vs"), stack("gvs"))
```

```python
import functools

import jax
import jax.numpy as jnp
import numpy as np
from jax import lax
from jax.experimental import pallas as pl
from jax.experimental.pallas import tpu as pltpu

F32 = jnp.float32
BF16 = jnp.bfloat16
I32 = jnp.int32

LANES = 128
SUBLANES = 8
VMEM_LIMIT_BYTES = 56 * 1024 * 1024

HEAD_DIM = 64
N_Q_HEADS = 8
N_KV_HEADS = 2
ATTN_WIDTH = N_Q_HEADS * HEAD_DIM
WINDOW = 128
PAST_LEN = 8192
ROPE_THETA = 10000.0
N_GMLP_HEADS = 8
GMLP_WIDTH = N_GMLP_HEADS * HEAD_DIM
CHUNK = 128
MEM_HEADS = 4
MEM_HEAD_DIM = 64
PEER_HEADS = 8
N_KEYS = 128
PEER_TOPK = 16
D_HALF = 128
EPS = 1e-6
NEG_INF = -1e30
SQRT_HALF = float(np.sqrt(0.5))

GATE_PITCH = N_KEYS + SUBLANES
GATE_TILES_PER_STEP = 4

NT_DIMS = (((1,), (1,)), ((), ()))


def _params(*semantics):
    return pltpu.CompilerParams(dimension_semantics=semantics, vmem_limit_bytes=VMEM_LIMIT_BYTES)


def _full(shape):
    zeros = (0,) * len(shape)
    return pl.BlockSpec(shape, lambda *_: zeros)


def _rms(x, g):
    return x * lax.rsqrt(jnp.mean(x * x, axis=-1, keepdims=True) + EPS) * g


def _gelu(x):
    return 0.5 * x * (1.0 + lax.erf(x * SQRT_HALF))


def _dot(a, b):
    return jnp.dot(a, b, preferred_element_type=F32)


def _dot_nt(a, b):
    return lax.dot_general(a, b, NT_DIMS, preferred_element_type=F32)


def _split(x):
    hi = x.astype(BF16)
    return hi, (x - hi.astype(F32)).astype(BF16)


def _half_masks():
    lane = lax.broadcasted_iota(I32, (1, LANES), 1)
    low = lane < HEAD_DIM
    return low, low.astype(F32), 1.0 - low.astype(F32)


def _mix_in_kernel(x_ref, g_ref, w_ref, cos_ref, sin_ref, gs_ref, seg_ref,
                   q_ref, k_ref, v_ref, gu_ref, gv_ref):
    xn = _rms(x_ref[...], g_ref[...]).astype(BF16)
    proj = _dot(xn, w_ref[...])
    cosf = cos_ref[...]
    sinf = sin_ref[...]
    lane = lax.broadcasted_iota(I32, (1, LANES), 1)
    first_half = (lane & (HEAD_DIM - 1)) < HEAD_DIM // 2

    def rope(c):
        partner = jnp.where(first_half, pltpu.roll(c, LANES - HEAD_DIM // 2, 1),
                            pltpu.roll(c, HEAD_DIM // 2, 1))
        return c * cosf + partner * sinf

    for c in range(ATTN_WIDTH // LANES):
        sl = slice(c * LANES, (c + 1) * LANES)
        q_ref[:, sl] = rope(proj[:, sl]) * (HEAD_DIM ** -0.5)
    k_ref[...] = rope(proj[:, 512:640])
    v_ref[...] = proj[:, 640:768]
    gu_ref[...] = _gelu(proj[:, 768:1280])
    gv = _gelu(proj[:, 1280:1792])
    hi, lo = _split(gv * gv)
    seg = seg_ref[...]
    ms = _dot(hi, seg) + _dot(lo, seg)
    gv_ref[...] = gv * lax.rsqrt(ms + EPS) * gs_ref[...]


def _mix_in(x, g, w_in, cosf, sinf, gs, seg, *, rows, table_blocks):
    t = x.shape[0]
    d = x.shape[1]
    row = lambda width: pl.BlockSpec((rows, width), lambda i: (i, 0))
    tab = pl.BlockSpec((rows, LANES), lambda i: (i % table_blocks, 0))
    return pl.pallas_call(
        _mix_in_kernel,
        grid=(t // rows,),
        in_specs=[row(d), _full(g.shape), _full(w_in.shape), tab, tab, _full(gs.shape), _full(seg.shape)],
        out_specs=[row(512), row(128), row(128), row(512), row(512)],
        out_shape=[jax.ShapeDtypeStruct((t, 512), F32), jax.ShapeDtypeStruct((t, 128), F32),
                   jax.ShapeDtypeStruct((t, 128), F32), jax.ShapeDtypeStruct((t, 512), F32),
                   jax.ShapeDtypeStruct((t, 512), F32)],
        compiler_params=_params("parallel"),
        name="mix_in",
    )(x, g, w_in, cosf, sinf, gs, seg)


def _stack_heads(q, lowf, highf):
    parts = [(q[:, c * LANES:(c + 1) * LANES] * m).astype(BF16)
             for m in (lowf, highf) for c in range(4)]
    return jnp.concatenate(parts, axis=0)


def _sink_softmax(s, sink):
    m = jnp.maximum(jnp.max(s, axis=-1, keepdims=True), sink)
    p = jnp.exp(s - m)
    den = jnp.sum(p, axis=-1, keepdims=True) + jnp.exp(sink - m)
    return (p / den).astype(BF16)


def _merge_out(attn, sgu, ga, gg, wo, x):
    merged = jnp.concatenate([_rms(attn, ga), _rms(sgu, gg)], axis=1).astype(BF16)
    return x + _dot(merged, wo)


def _mix_prompt_kernel(sinks_ref, q_ref, kp_ref, kc_ref, vp_ref, vc_ref, gu_ref, gv_ref, x_ref,
                       ws_ref, bias_ref, ga_ref, gg_ref, wo_ref, o_ref):
    n = pl.program_id(1)
    low, lowf, highf = _half_masks()
    blk = WINDOW
    nsub = q_ref.shape[0] // blk
    sub = [slice(s * blk, (s + 1) * blk) for s in range(nsub)]
    band = [slice(s * blk, (s + 2) * blk) for s in range(nsub)]
    kall = jnp.concatenate([kp_ref[...], kc_ref[...]], axis=0).astype(BF16)
    vall = jnp.concatenate([vp_ref[...], vc_ref[...]], axis=0).astype(BF16)
    q = q_ref[...]
    i = lax.broadcasted_iota(I32, (blk, 2 * blk), 0)
    j = lax.broadcasted_iota(I32, (blk, 2 * blk), 1)
    in_window = (j > i) & (j <= i + blk)
    first_key = jnp.where(n > 0, 0, blk)
    valid = [in_window & (j >= first_key)] + [in_window] * (nsub - 1)
    scores = [_dot_nt(_stack_heads(q[sub[s]], lowf, highf), kall[band[s]]) for s in range(nsub)]
    probs = [jnp.concatenate(
        [_sink_softmax(jnp.where(valid[s], scores[s][h * blk:(h + 1) * blk], NEG_INF), sinks_ref[h])
         for h in range(N_Q_HEADS)], axis=0) for s in range(nsub)]
    outs = [_dot(probs[s], vall[band[s]]) for s in range(nsub)]
    attn = jnp.concatenate([jnp.concatenate(
        [jnp.where(low, o[c * blk:(c + 1) * blk], o[(4 + c) * blk:(5 + c) * blk]) for c in range(4)], axis=1)
        for o in outs], axis=0)

    gvb = gv_ref[...].astype(BF16)
    r = lax.broadcasted_iota(I32, (CHUNK, CHUNK), 0)
    c_ = lax.broadcasted_iota(I32, (CHUNK, CHUNK), 1)
    tril = r >= c_
    w = [jnp.where(tril, ws_ref[h], 0.0).astype(BF16) for h in range(N_GMLP_HEADS)]
    bias = bias_ref[...]
    mixed = jnp.concatenate([jnp.concatenate(
        [jnp.where(low, _dot(w[2 * c], gvb[sub[s], c * LANES:(c + 1) * LANES]),
                   _dot(w[2 * c + 1], gvb[sub[s], c * LANES:(c + 1) * LANES])) for c in range(4)], axis=1) + bias
        for s in range(nsub)], axis=0)
    sgu = gu_ref[...] * mixed
    o_ref[...] = _merge_out(attn, sgu, ga_ref[...], gg_ref[...], wo_ref[...], x_ref[...])


def _mix_prompt(sinks, q, k, v, gu, gv, x, ws, bias, ga, gg, wo, *, batch, seq, rows):
    nb = seq // rows
    per = rows // WINDOW
    cur = lambda width: pl.BlockSpec((rows, width), lambda b, n: (b * nb + n, 0))
    prev = lambda width: pl.BlockSpec(
        (WINDOW, width), lambda b, n: (b * nb * per + jnp.maximum(n * per - 1, 0), 0))
    return pl.pallas_call(
        _mix_prompt_kernel,
        grid=(batch, nb),
        in_specs=[pl.BlockSpec(memory_space=pltpu.SMEM),
                  cur(512), prev(128), cur(128), prev(128), cur(128), cur(512), cur(512), cur(x.shape[1]),
                  _full(ws.shape), _full(bias.shape), _full(ga.shape), _full(gg.shape), _full(wo.shape)],
        out_specs=cur(x.shape[1]),
        out_shape=jax.ShapeDtypeStruct(x.shape, F32),
        compiler_params=_params("parallel", "parallel"),
        name="mix_prompt",
    )(sinks, q, k, k, v, v, gu, gv, x, ws, bias, ga, gg, wo)


def _mix_sample_kernel(q_ref, kn_ref, vn_ref, gu_ref, gv_ref, x_ref, ck_ref, cv_ref,
                       sink_ref, ctab_ref, bias_ref, ga_ref, gg_ref, wo_ref,
                       o_ref, nk_ref, nv_ref, attn_s, sgu_s):
    nseq = ck_ref.shape[0]
    t = q_ref.shape[0] // nseq
    w = ck_ref.shape[1]
    low, lowf, highf = _half_masks()
    rows = N_Q_HEADS * t
    band = 2 * w
    i = lax.broadcasted_iota(I32, (rows, band), 0) & (t - 1)
    j = lax.broadcasted_iota(I32, (rows, band), 1)
    valid = (j > i) & (j <= i + w)
    sink = sink_ref[:, 0:1]
    pad = jnp.zeros((band - w - t, LANES), F32)

    seq_rows = [slice(b * t, (b + 1) * t) for b in range(nseq)]
    scores = []
    for b, rs in enumerate(seq_rows):
        qs = _stack_heads(q_ref[rs, :], lowf, highf)
        kb = jnp.concatenate([ck_ref[b], kn_ref[rs, :], pad], axis=0).astype(BF16)
        scores.append(jnp.where(valid, _dot_nt(qs, kb), NEG_INF))
    probs = [_sink_softmax(s, sink) for s in scores]
    for b, rs in enumerate(seq_rows):
        vc, vn = cv_ref[b], vn_ref[rs, :]
        o_all = _dot(probs[b], jnp.concatenate([vc, vn, pad], axis=0).astype(BF16))
        for c in range(4):
            attn_s[rs, c * LANES:(c + 1) * LANES] = jnp.where(
                low, o_all[c * t:(c + 1) * t], o_all[(4 + c) * t:(5 + c) * t])
        nk_ref[b] = jnp.concatenate([ck_ref[b, t:, :], kn_ref[rs, :]], axis=0)
        nv_ref[b] = jnp.concatenate([vc[t:], vn], axis=0)
    for b, rs in enumerate(seq_rows):
        gvb = gv_ref[rs, :]
        mixed = ctab_ref[0] * gvb[0:1, :]
        for r in range(1, t):
            mixed = mixed + ctab_ref[r] * gvb[r:r + 1, :]
        sgu_s[rs, :] = gu_ref[rs, :] * (mixed + bias_ref[...])
    o_ref[...] = _merge_out(attn_s[...], sgu_s[...], ga_ref[...], gg_ref[...], wo_ref[...], x_ref[...])


def _mix_sample(q, k, v, gu, gv, x, ck, cv, sink_tab, ctab, bias, ga, gg, wo, *, group):
    nseq, w, _ = ck.shape
    t = x.shape[0] // nseq
    assert t == SUBLANES and w == WINDOW and nseq % group == 0
    rows = group * t
    row = lambda width: pl.BlockSpec((rows, width), lambda i: (i, 0))
    cache = pl.BlockSpec((group, w, LANES), lambda i: (i, 0, 0))
    return pl.pallas_call(
        _mix_sample_kernel,
        grid=(nseq // group,),
        in_specs=[row(512), row(128), row(128), row(512), row(512), row(x.shape[1]), cache, cache,
                  _full(sink_tab.shape), _full(ctab.shape), _full(bias.shape),
                  _full(ga.shape), _full(gg.shape), _full(wo.shape)],
        out_specs=[row(x.shape[1]), cache, cache],
        out_shape=[jax.ShapeDtypeStruct(x.shape, F32), jax.ShapeDtypeStruct(ck.shape, F32),
                   jax.ShapeDtypeStruct(cv.shape, F32)],
        scratch_shapes=[pltpu.VMEM((rows, 512), F32), pltpu.VMEM((rows, 512), F32)],
        compiler_params=_params("parallel"),
        name="mix_sample",
    )(q, k, v, gu, gv, x, ck, cv, sink_tab, ctab, bias, ga, gg, wo)


def _mem_kv_kernel(m_ref, g_ref, w_ref, k_ref, v_ref):
    mn = _rms(m_ref[...], g_ref[...]).astype(BF16)
    kv = _dot(mn, w_ref[...])
    half = kv.shape[1] // 2
    k_ref[...] = kv[:, :half]
    v_ref[...] = kv[:, half:]


def _mem_kv(mem, g, w_kv, *, rows):
    t, d = mem.shape
    width = w_kv.shape[1] // 2
    row = lambda wd: pl.BlockSpec((rows, wd), lambda i: (i, 0))
    return pl.pallas_call(
        _mem_kv_kernel,
        grid=(t // rows,),
        in_specs=[row(d), _full(g.shape), _full(w_kv.shape)],
        out_specs=[row(width), row(width)],
        out_shape=[jax.ShapeDtypeStruct((t, width), F32)] * 2,
        compiler_params=_params("parallel"),
        name="mem_kv",
    )(mem, g, w_kv)


def _softmax(s):
    p = jnp.exp(s - jnp.max(s, axis=-1, keepdims=True))
    return (p / jnp.sum(p, axis=-1, keepdims=True)).astype(BF16)


def _mem_heads(q, mk, mv, low, lowf, highf):
    rows = q.shape[0]
    outs = []
    for c in range(MEM_HEADS // 2):
        sl = slice(c * LANES, (c + 1) * LANES)
        qc = q[:, sl]
        q2 = jnp.concatenate([(qc * lowf).astype(BF16), (qc * highf).astype(BF16)], axis=0)
        o = _dot(_softmax(_dot_nt(q2, mk[:, sl].astype(BF16))), mv[:, sl].astype(BF16))
        outs.append(jnp.where(low, o[:rows], o[rows:]))
    return jnp.concatenate(outs, axis=1)


def _mem_attn_prompt_kernel(x_ref, g_ref, wq_ref, mk_ref, mv_ref, wo_ref, o_ref):
    low, lowf, highf = _half_masks()
    x = x_ref[...]
    q = _dot(_rms(x, g_ref[...]).astype(BF16), wq_ref[...]) * (MEM_HEAD_DIM ** -0.5)
    o = _mem_heads(q, mk_ref[...], mv_ref[...], low, lowf, highf)
    o_ref[...] = x + _dot(o.astype(BF16), wo_ref[...])


def _mem_attn_prompt(x, g, wq, mk, mv, wo, *, batch, seq, rows):
    n_mem = mk.shape[0] // batch
    nb = seq // rows
    row = pl.BlockSpec((rows, x.shape[1]), lambda b, n: (b * nb + n, 0))
    mem = pl.BlockSpec((n_mem, mk.shape[1]), lambda b, n: (b, 0))
    return pl.pallas_call(
        _mem_attn_prompt_kernel,
        grid=(batch, nb),
        in_specs=[row, _full(g.shape), _full(wq.shape), mem, mem, _full(wo.shape)],
        out_specs=row,
        out_shape=jax.ShapeDtypeStruct(x.shape, F32),
        compiler_params=_params("parallel", "parallel"),
        name="mem_attn_prompt",
    )(x, g, wq, mk, mv, wo)


def _mem_attn_sample_kernel(x_ref, g_ref, wq_ref, mk_ref, mv_ref, wo_ref, o_ref, q_s, a_s):
    nseq = mk_ref.shape[0]
    t = x_ref.shape[0] // nseq
    low, lowf, highf = _half_masks()
    x = x_ref[...]
    q_s[...] = _dot(_rms(x, g_ref[...]).astype(BF16), wq_ref[...]) * (MEM_HEAD_DIM ** -0.5)
    for c in range(MEM_HEADS // 2):
        sl = slice(c * LANES, (c + 1) * LANES)
        scores = []
        for b in range(nseq):
            qc = q_s[b * t:(b + 1) * t, sl]
            q2 = jnp.concatenate([(qc * lowf).astype(BF16), (qc * highf).astype(BF16)], axis=0)
            scores.append(_dot_nt(q2, mk_ref[b, :, sl].astype(BF16)))
        p = _softmax(jnp.concatenate(scores, axis=0))
        for b in range(nseq):
            o = _dot(p[2 * t * b:2 * t * (b + 1)], mv_ref[b, :, sl].astype(BF16))
            a_s[b * t:(b + 1) * t, sl] = jnp.where(low, o[:t], o[t:])
    o_ref[...] = x + _dot(a_s[...].astype(BF16), wo_ref[...])


def _mem_attn_sample(x, g, wq, mk, mv, wo, *, group):
    nseq, n_mem, width = mk.shape
    t = x.shape[0] // nseq
    rows = group * t
    row = pl.BlockSpec((rows, x.shape[1]), lambda i: (i, 0))
    mem = pl.BlockSpec((group, n_mem, width), lambda i: (i, 0, 0))
    return pl.pallas_call(
        _mem_attn_sample_kernel,
        grid=(nseq // group,),
        in_specs=[row, _full(g.shape), _full(wq.shape), mem, mem, _full(wo.shape)],
        out_specs=row,
        out_shape=jax.ShapeDtypeStruct(x.shape, F32),
        scratch_shapes=[pltpu.VMEM((rows, width), F32), pltpu.VMEM((rows, width), F32)],
        compiler_params=_params("parallel"),
        name="mem_attn_sample",
    )(x, g, wq, mk, mv, wo)


def _top16(s):
    nrows, cols = s.shape
    r = lax.broadcasted_iota(I32, (nrows, cols), 0).astype(F32)
    slot = lax.broadcasted_iota(I32, (PEER_TOPK, cols), 0)
    vals = jnp.zeros((PEER_TOPK, cols), F32)
    idxs = jnp.zeros((PEER_TOPK, cols), F32)
    for p in range(PEER_TOPK):
        m = jnp.max(s, axis=0, keepdims=True)
        am = jnp.min(jnp.where(s == m, r, float(nrows)), axis=0, keepdims=True)
        vals = jnp.where(slot == p, m, vals)
        idxs = jnp.where(slot == p, am, idxs)
        s = jnp.where(r == am, -jnp.inf, s)
    return vals, idxs


def _sort16_pairs():
    def merge(lo, hi, r):
        step = r * 2
        if step < hi - lo:
            yield from merge(lo, hi, step)
            yield from merge(lo + r, hi, step)
            yield from [(i, i + r) for i in range(lo + r, hi - r, step)]
        else:
            yield (lo, lo + r)

    def sort(lo, hi):
        if hi > lo:
            mid = lo + (hi - lo) // 2
            yield from sort(lo, mid)
            yield from sort(mid + 1, hi)
            yield from merge(lo, hi, 1)

    return tuple(sort(0, PEER_TOPK - 1))


SORT16 = _sort16_pairs()


def _cmpx(v, pays, i, j):
    swap = v[j] > v[i]
    hi, lo = jnp.maximum(v[i], v[j]), jnp.minimum(v[i], v[j])
    for p in pays:
        p[i], p[j] = jnp.where(swap, p[j], p[i]), jnp.where(swap, p[i], p[j])
    v[i], v[j] = hi, lo


def _merge16(va, pa, vb, pb):
    n = PEER_TOPK
    v, pays = [], [[] for _ in pa]
    for i in range(n):
        other = vb[n - 1 - i]
        if other is None:
            v.append(va[i])
            for k in range(len(pa)):
                pays[k].append(pa[k][i])
            continue
        take = other > va[i]
        v.append(jnp.maximum(va[i], other))
        for k in range(len(pa)):
            pays[k].append(jnp.where(take, pb[k][n - 1 - i], pa[k][i]))
    d = n // 2
    while d:
        for i in range(n):
            if not i & d:
                _cmpx(v, pays, i, i + d)
        d //= 2
    return v, pays


def _xor_rows(x, d, sub):
    if d == SUBLANES // 2:
        return pltpu.roll(x, d, 0)
    return jnp.where((sub & d) == 0, pltpu.roll(x, SUBLANES - d, 0), pltpu.roll(x, d, 0))


def _row(x, k):
    return jnp.broadcast_to(x[k:k + 1, :], x.shape)


def _route_sorted(s_s, rows, i1_s, i2_s, gt_s):
    n = PEER_TOPK
    sub = lax.broadcasted_iota(I32, (SUBLANES, LANES), 0)
    subf = sub.astype(F32)
    bit2 = (sub & 4) == 0
    bit1 = (sub & 2) == 0
    halves = [slice(0, LANES), slice(LANES, 2 * LANES)]
    bad = jnp.zeros((SUBLANES, LANES), F32)

    def packed_merge(mask, d, xa, xb):
        va, pa = xa
        vb, pb = xb
        lv = [jnp.where(mask, a, b) for a, b in zip(va, vb)]
        rv = [_xor_rows(jnp.where(mask, b, a), d, sub) for a, b in zip(va, vb)]
        lp = [[jnp.where(mask, a, b) for a, b in zip(qa, qb)] for qa, qb in zip(pa, pb)]
        rp = [[_xor_rows(jnp.where(mask, b, a), d, sub) for a, b in zip(qa, qb)] for qa, qb in zip(pa, pb)]
        return _merge16(lv, lp, rv, rp)

    def self_merge(d, x):
        v, p = x
        return _merge16(v, p, [_xor_rows(a, d, sub) for a in v], [[_xor_rows(a, d, sub) for a in q] for q in p])

    def decreasing(v):
        ok = v[0] > v[1]
        flag = jnp.where(ok, 0.0, 1.0)
        for r in range(1, n - 1):
            flag = jnp.where(v[r] > v[r + 1], flag, 1.0)
        return flag

    sorted_lists = {}
    for c in range(2):
        for part in range(2):
            v = [s_s[c, SUBLANES * g:SUBLANES * (g + 1), halves[part]] for g in range(N_KEYS // SUBLANES)]
            ix = [subf + float(SUBLANES * g) for g in range(N_KEYS // SUBLANES)]
            pays = [ix]
            for i, j in SORT16:
                _cmpx(v, pays, i, j)
            sorted_lists[c, part] = (v, pays)
    x1 = [packed_merge(bit2, 4, sorted_lists[c, 0], sorted_lists[c, 1]) for c in range(2)]
    x2 = packed_merge(bit1, 2, x1[0], x1[1])
    tv, (ti,) = self_merge(1, x2)
    bad = jnp.maximum(bad, decreasing(tv))
    for c in range(2):
        for part in range(2):
            thr = _row(tv[n - 1], 4 * part + 2 * c)
            cnt = jnp.zeros((SUBLANES, LANES), F32)
            for g in range(N_KEYS // SUBLANES):
                cnt = cnt + jnp.where(s_s[c, SUBLANES * g:SUBLANES * (g + 1), halves[part]] >= thr, 1.0, 0.0)
            total = jnp.sum(cnt, axis=0, keepdims=True)
            bad = jnp.maximum(bad, jnp.broadcast_to(jnp.where(total == float(n), 0.0, 1.0), bad.shape))

    half = n // 2
    lists, tails = [], []
    for part in range(2):
        base = 4 * part
        s1 = [_row(tv[r], base) for r in range(n)]
        a1 = [_row(ti[r], base) for r in range(n)]
        s2 = [_row(tv[r], base + 2) for r in range(n)]
        a2 = [_row(ti[r], base + 2) for r in range(n)]
        p1, ip1 = s1[0], a1[0]
        for p in range(1, half):
            p1 = jnp.where(sub == p, s1[p], p1)
            ip1 = jnp.where(sub == p, a1[p], ip1)
        lists.append(([p1 + s2[q] for q in range(n)], [[ip1] * n, list(a2)]))
        tails.append(([s1[half + i] + s2[0] for i in range(half)],
                      [[a1[half + i] for i in range(half)], [a2[0]] * half]))
    y = packed_merge(bit2, 4, lists[0], lists[1])
    y = self_merge(2, y)
    y = self_merge(1, y)
    dv = [jnp.where(bit2, a, b) for a, b in zip(tails[0][0], tails[1][0])] + [None] * half
    dp = [[jnp.where(bit2, a, b) for a, b in zip(qa, qb)] + [None] * half
          for qa, qb in zip(tails[0][1], tails[1][1])]
    top, (sel1, sel2) = _merge16(y[0], y[1], dv, dp)
    bad = jnp.maximum(bad, decreasing(top))
    for part in range(2):
        thr = _row(top[n - 1], 4 * part)
        cnt = jnp.zeros((SUBLANES, LANES), F32)
        for q in range(n):
            cnt = cnt + jnp.where(lists[part][0][q] >= thr, 1.0, 0.0)
        total = jnp.sum(cnt, axis=0, keepdims=True)
        for i in range(half):
            total = total + jnp.where(tails[part][0][i][0:1, :] >= thr[0:1, :], 1.0, 0.0)
        bad = jnp.maximum(bad, jnp.broadcast_to(jnp.where(total == float(n), 0.0, 1.0), bad.shape))

    e = [jnp.exp(t - top[0]) for t in top]
    z = e[0]
    for k in range(1, n):
        z = z + e[k]
    gate = [ek / z for ek in e]
    for src, dst in ((sel1, i1_s), (sel2, i2_s), (gate, gt_s)):
        for grp in range(2):
            for part in range(2):
                tile = None
                for k in range(SUBLANES):
                    slab = src[grp * SUBLANES + k]
                    if (k < 4) != (part == 0):
                        slab = pltpu.roll(slab, 4, 0)
                    tile = slab if tile is None else jnp.where(sub == k, slab, tile)
                dst[rows[grp], halves[part]] = tile
    return jnp.max(bad)


def _peer_route_kernel(x_ref, g_ref, wh_ref, wl_ref, kh_ref, kl_ref,
                       xn_ref, i1_ref, i2_ref, gate_ref, q_s, s_s, i1_s, i2_s, gt_s):
    tb = x_ref.shape[0]
    xn = _rms(x_ref[...], g_ref[...])
    xh, xl = _split(xn)
    xn_ref[...] = xh
    wh = wh_ref[...]
    q = _dot(xh, wh) + _dot(xh, wl_ref[...]) + _dot(xl, wh)
    for hc in range(2 * PEER_HEADS):
        q_s[hc] = q[:, hc * D_HALF:(hc + 1) * D_HALF]

    def head(h, carry):
        for c in range(2):
            qh, ql = _split(q_s[2 * h + c])
            kh = kh_ref[2 * h + c]
            s_s[c] = _dot_nt(kh, qh) + _dot_nt(kh, ql) + _dot_nt(kl_ref[2 * h + c], qh)
        rows = [pl.ds(pl.multiple_of(h * PEER_TOPK + k, SUBLANES), SUBLANES) for k in (0, SUBLANES)]
        tied = _route_sorted(s_s, rows, i1_s, i2_s, gt_s)

        @pl.when(tied > 0.0)
        def _():
            for part in range(tb // LANES):
                head_part(h, part)

        return carry

    def head_part(h, part):
        cols = slice(part * LANES, (part + 1) * LANES)
        (s1, i1), (s2, i2) = [_top16(s_s[c, :, cols]) for c in range(2)]
        tb = LANES
        half = PEER_TOPK // 2
        cand = jnp.concatenate(
            [jnp.broadcast_to(s1[0:1], (PEER_TOPK, tb)) + s2]
            + [jnp.broadcast_to(s1[p:p + 1], (half, tb)) + s2[:half] for p in range(1, half)]
            + [s1[half:] + jnp.broadcast_to(s2[0:1], (half, tb))], axis=0)
        top, cidx = _top16(cand)
        ci = cidx.astype(I32)
        mid = ci - PEER_TOPK
        tail = PEER_TOPK + half * (half - 1)
        pi = jnp.where(ci < PEER_TOPK, 0,
                       jnp.where(ci < tail, 1 + (mid >> (half.bit_length() - 1)), ci - tail + half))
        qi = jnp.where(ci < PEER_TOPK, ci, jnp.where(ci < tail, mid & (half - 1), 0))
        e1 = jnp.zeros((PEER_TOPK, tb), F32)
        e2 = jnp.zeros((PEER_TOPK, tb), F32)
        for p in range(PEER_TOPK):
            e1 = jnp.where(pi == p, jnp.broadcast_to(i1[p:p + 1], (PEER_TOPK, tb)), e1)
            e2 = jnp.where(qi == p, jnp.broadcast_to(i2[p:p + 1], (PEER_TOPK, tb)), e2)
        e = jnp.exp(top - jnp.max(top, axis=0, keepdims=True))
        rs = pl.ds(pl.multiple_of(h * PEER_TOPK, PEER_TOPK), PEER_TOPK)
        i1_s[rs, cols] = e1
        i2_s[rs, cols] = e2
        gt_s[rs, cols] = e / jnp.sum(e, axis=0, keepdims=True)

    lax.fori_loop(0, PEER_HEADS, head, 0)
    i1_ref[...] = i1_s[...].T.astype(I32)
    i2_ref[...] = i2_s[...].T.astype(I32)
    gate_ref[...] = gt_s[...].T


def _peer_route(x, g, wh, wl, kh, kl):
    t, d = x.shape
    tb = 2 * LANES
    assert t % tb == 0
    sel = PEER_HEADS * PEER_TOPK
    row = lambda width: pl.BlockSpec((tb, width), lambda i: (i, 0))
    return pl.pallas_call(
        _peer_route_kernel,
        grid=(t // tb,),
        in_specs=[row(d), _full(g.shape), _full(wh.shape), _full(wl.shape), _full(kh.shape), _full(kl.shape)],
        out_specs=[row(d), row(sel), row(sel), row(sel)],
        out_shape=[jax.ShapeDtypeStruct((t, d), BF16), jax.ShapeDtypeStruct((t, sel), I32),
                   jax.ShapeDtypeStruct((t, sel), I32), jax.ShapeDtypeStruct((t, sel), F32)],
        scratch_shapes=[pltpu.VMEM((2 * PEER_HEADS, tb, D_HALF), F32), pltpu.VMEM((2, N_KEYS, tb), F32),
                        pltpu.VMEM((sel, tb), F32), pltpu.VMEM((sel, tb), F32), pltpu.VMEM((sel, tb), F32)],
        compiler_params=_params("parallel"),
        name="peer_route",
    )(x, g, wh, wl, kh, kl)


def _peer_gates_kernel(i1_ref, i2_ref, gate_ref, o_ref, g_s):
    tb = i1_ref.shape[0]
    sel = i1_ref.shape[1]
    key = lax.broadcasted_iota(I32, (N_KEYS, sel), 0)
    zero = jnp.zeros((N_KEYS, sel), BF16)

    tile = 2 * SUBLANES

    def build(tidx):
        w1, w2 = [], []
        for half in range(2):
            t0 = pl.multiple_of(tidx * tile + half * SUBLANES, SUBLANES)
            i1g = i1_ref[pl.ds(t0, SUBLANES), :]
            i2g = i2_ref[pl.ds(t0, SUBLANES), :]
            gtg = gate_ref[pl.ds(t0, SUBLANES), :]
            w1 += [jnp.where(key == i1g[s:s + 1, :], gtg[s:s + 1, :], 0.0).astype(BF16) for s in range(SUBLANES)]
            w2 += [jnp.where(key == i2g[s:s + 1, :], 1.0, 0.0).astype(BF16) for s in range(SUBLANES)]
        prods = []
        for s in range(0, tile, 2):
            lhs = jnp.concatenate([w1[s], w1[s + 1]], axis=1)
            rhs = jnp.concatenate([jnp.concatenate([w2[s], zero], axis=1),
                                   jnp.concatenate([zero, w2[s + 1]], axis=1)], axis=0)
            prods.append(_dot_nt(lhs, rhs))
        for s, g in enumerate(prods):
            word = pltpu.pack_elementwise([g[:, :N_KEYS], g[:, N_KEYS:]], packed_dtype=BF16)
            pair = tidx * SUBLANES + s
            g_s[pl.ds(pl.multiple_of(pair * GATE_PITCH, SUBLANES), N_KEYS), :] = word

    def relayout(tidx):
        rows = pl.ds(pl.multiple_of(tidx * tile, tile), tile)
        for a in range(N_KEYS):
            words = g_s[pl.ds(tidx * SUBLANES * GATE_PITCH + a, SUBLANES, stride=GATE_PITCH), :]
            o_ref[rows, a * N_KEYS:(a + 1) * N_KEYS] = pltpu.bitcast(words, BF16)

    group = GATE_TILES_PER_STEP
    n_groups = tb // (tile * group)
    for u in range(group):
        build(u)

    def step(gidx, carry):
        for u in range(group):
            relayout((gidx - 1) * group + u)
        for u in range(group):
            build(gidx * group + u)
        return carry

    lax.fori_loop(1, n_groups, step, 0)
    for u in range(group):
        relayout((n_groups - 1) * group + u)


def _peer_gates(i1, i2, gate, *, rows):
    t, sel = i1.shape
    row = pl.BlockSpec((rows, sel), lambda i: (i, 0))
    return pl.pallas_call(
        _peer_gates_kernel,
        grid=(t // rows,),
        in_specs=[row, row, row],
        out_specs=pl.BlockSpec((rows, N_KEYS * N_KEYS), lambda i: (i, 0)),
        out_shape=jax.ShapeDtypeStruct((t, N_KEYS * N_KEYS), BF16),
        scratch_shapes=[pltpu.VMEM((rows // 2 * GATE_PITCH, N_KEYS), jnp.uint32)],
        compiler_params=_params("parallel"),
        name="peer_gates",
    )(i1, i2, gate)


def _peer_dense_kernel(xn_ref, gates_ref, u_ref, v_ref, x_ref, g_ref, o_ref, acc):
    k = pl.program_id(1)

    @pl.when(k == 0)
    def _():
        acc[...] = jnp.zeros_like(acc)

    h = _dot_nt(xn_ref[...], u_ref[...])
    a = (_gelu(h) * gates_ref[...].astype(F32)).astype(BF16)
    acc[...] += _dot(a, v_ref[...])

    @pl.when(k == pl.num_programs(1) - 1)
    def _():
        o_ref[...] = _rms(x_ref[...] + acc[...], g_ref[...])


def _peer_dense(xn, gates, u, v, x, g, *, rows, experts):
    t, d = x.shape
    n_exp = u.shape[0]
    tok = pl.BlockSpec((rows, d), lambda i, k: (i, 0))
    tab = pl.BlockSpec((experts, d), lambda i, k: (k, 0))
    return pl.pallas_call(
        _peer_dense_kernel,
        grid=(t // rows, n_exp // experts),
        in_specs=[tok, pl.BlockSpec((rows, experts), lambda i, k: (i, k)), tab, tab, tok, _full(g.shape)],
        out_specs=tok,
        out_shape=jax.ShapeDtypeStruct((t, d), F32),
        scratch_shapes=[pltpu.VMEM((rows, d), F32)],
        compiler_params=_params("parallel", "arbitrary"),
        name="peer_dense",
    )(xn, gates, u, v, x, g)


def _peer(x, g_ffn, wh, wl, kh, kl, u, v, g_final):
    t = x.shape[0]
    xn, i1, i2, gate = _peer_route(x, g_ffn, wh, wl, kh, kl)
    gates = _peer_gates(i1, i2, gate, rows=2 * LANES)
    return _peer_dense(xn, gates, u, v, x, g_final, rows=min(512, t), experts=2048)


def _head_perm():
    new = np.zeros(ATTN_WIDTH, np.int32)
    for h in range(N_Q_HEADS):
        dst = (h % 4) * LANES + (h // 4) * HEAD_DIM
        new[dst:dst + HEAD_DIM] = np.arange(h * HEAD_DIM, (h + 1) * HEAD_DIM)
    return new


def _rope_tables(pos):
    half = HEAD_DIM // 2
    inv = ROPE_THETA ** (-jnp.arange(half, dtype=F32) / half)
    ang = pos.astype(F32)[:, None] * inv[None, :]
    cos, sin = jnp.cos(ang), jnp.sin(ang)
    reps = LANES // HEAD_DIM
    return (jnp.tile(jnp.concatenate([cos, cos], axis=1), (1, reps)),
            jnp.tile(jnp.concatenate([-sin, sin], axis=1), (1, reps)))


def kernel(x_prompt, x_sample, mem_prompt, cache_swa_k, cache_swa_v, cache_mem_k, cache_mem_v, g_mix, w_in, attn_sinks, g_sgu, w_spatial, b_spatial, g_attn_out, g_gmlp_out, w_out, g_cross, g_mem, w_cq, w_mk, w_mv, w_co, g_ffn, w_peer_q, peer_sub_keys, peer_u, peer_v, g_final):
    batch, seq, d = x_prompt.shape
    dec_batch, dec_seq, _ = x_sample.shape
    depth = g_mix.shape[0]
    n_mem = mem_prompt.shape[1]
    assert seq % WINDOW == 0 and dec_seq == SUBLANES and cache_swa_k.shape[2] == WINDOW

    perm = _head_perm()
    cos_p, sin_p = _rope_tables(jnp.arange(seq, dtype=I32))
    cos_s, sin_s = _rope_tables(PAST_LEN +(jnp.arange(dec_batch * dec_seq, dtype=I32) % dec_seq))
    seg = jnp.asarray(np.kron(np.eye(N_GMLP_HEADS), np.full((HEAD_DIM, HEAD_DIM), 1.0 / HEAD_DIM)), BF16)
    row2 = lambda a: a.reshape(1, -1)

    xp = x_prompt.reshape(batch * seq, d)
    xs = x_sample.reshape(dec_batch * dec_seq, d)
    outs = {name: [] for name in ("kp", "vp", "mk", "mv", "ks", "vs", "gvs")}
    for l in range(depth):
        w_in_l = jnp.concatenate([w_in[l][:, :ATTN_WIDTH][:, perm], w_in[l][:, ATTN_WIDTH:]], axis=1).astype(BF16)
        w_out_l = jnp.concatenate([w_out[l][:ATTN_WIDTH][perm], w_out[l][ATTN_WIDTH:]], axis=0).astype(BF16)
        ga = row2(g_attn_out[l][perm])
        gg = row2(g_gmlp_out[l])
        gs = row2(g_sgu[l])
        bias = jnp.repeat(b_spatial[l].T, HEAD_DIM, axis=1)
        wt = jnp.tril(w_spatial[l][:, :dec_seq, :dec_seq])
        ctab = jnp.repeat(jnp.transpose(wt, (2, 1, 0)), HEAD_DIM, axis=2)
        sink_tab = jnp.broadcast_to(jnp.repeat(attn_sinks[l], dec_seq)[:, None], (N_Q_HEADS * dec_seq, LANES))

        mix = functools.partial(_mix_in, g=row2(g_mix[l]), w_in=w_in_l, gs=gs, seg=seg)
        rows_p = 512 if seq % 512 == 0 else WINDOW
        qp, kp, vp, gup, gvp = mix(xp, cosf=cos_p, sinf=sin_p, rows=rows_p, table_blocks=seq // rows_p)
        rows_s = min(512, dec_batch * dec_seq)
        qs, ks, vs, gus, gvs = mix(xs, cosf=cos_s, sinf=sin_s, rows=rows_s,
                                   table_blocks=dec_batch * dec_seq // rows_s)

        xp = _mix_prompt(attn_sinks[l], qp, kp, vp, gup, gvp, xp, w_spatial[l], bias, ga, gg, w_out_l,
                         batch=batch, seq=seq, rows=2 * WINDOW if seq % (2 * WINDOW) == 0 else WINDOW)
        ck = cache_swa_k[l].reshape(dec_batch, WINDOW, N_KV_HEADS * HEAD_DIM)
        cv = cache_swa_v[l].reshape(dec_batch, WINDOW, N_KV_HEADS * HEAD_DIM)
        xs, nks, nvs = _mix_sample(qs, ks, vs, gus, gvs, xs, ck, cv, sink_tab, ctab, bias[:dec_seq], ga, gg,
                                   w_out_l, group=min(16, dec_batch))

        w_kv = jnp.concatenate([w_mk[l], w_mv[l]], axis=1).astype(BF16)
        mk, mv = _mem_kv(mem_prompt.reshape(batch * n_mem, d), row2(g_mem[l]), w_kv, rows=n_mem)
        wq = w_cq[l].astype(BF16)
        wo = w_co[l].astype(BF16)
        gc = row2(g_cross[l])
        xp = _mem_attn_prompt(xp, gc, wq, mk, mv, wo, batch=batch, seq=seq, rows=512 if seq % 512 == 0 else 256)
        cmk = cache_mem_k[l].reshape(dec_batch, n_mem, MEM_HEADS * MEM_HEAD_DIM)
        cmv = cache_mem_v[l].reshape(dec_batch, n_mem, MEM_HEADS * MEM_HEAD_DIM)
        xs = _mem_attn_sample(xs, gc, wq, cmk, cmv, wo, group=min(8, dec_batch))

        wh, wl = _split(w_peer_q[l])
        kh, kl = _split(peer_sub_keys[l].reshape(2 * PEER_HEADS, N_KEYS, D_HALF))
        u = peer_u[l].astype(BF16)
        v = peer_v[l].astype(BF16)
        last = l == depth - 1
        assert last, "stacked layers need an un-normalised PEER output"
        peer = functools.partial(_peer, g_ffn=row2(g_ffn[l]), wh=wh, wl=wl, kh=kh, kl=kl, u=u, v=v,
                                 g_final=row2(g_final))
        xp = peer(xp)
        xs = peer(xs)

        outs["kp"].append(kp.reshape(batch, seq, N_KV_HEADS, HEAD_DIM)[:, seq - WINDOW:])
        outs["vp"].append(vp.reshape(batch, seq, N_KV_HEADS, HEAD_DIM)[:, seq - WINDOW:])
        outs["mk"].append(mk.reshape(batch, n_mem, MEM_HEADS, MEM_HEAD_DIM))
        outs["mv"].append(mv.reshape(batch, n_mem, MEM_HEADS, MEM_HEAD_DIM))
        outs["ks"].append(nks.reshape(dec_batch, WINDOW, N_KV_HEADS, HEAD_DIM))
        outs["vs"].append(nvs.reshape(dec_batch, WINDOW, N_KV_HEADS, HEAD_DIM))
        outs["gvs"].append(gvs.reshape(dec_batch, dec_seq, N_GMLP_HEADS, HEAD_DIM))

    stack = lambda name: jnp.stack(outs[name])
    return (xp.reshape(batch, seq, d), xs.reshape(dec_batch, dec_seq, d),
            stack("kp"), stack("vp"), stack("mk"), stack("mv"), stack("ks"), stack("vs"), stack("gvs"))
```

```python
import functools

import jax
import jax.numpy as jnp
import numpy as np
from jax import lax
from jax.experimental import pallas as pl
from jax.experimental.pallas import tpu as pltpu

F32 = jnp.float32
BF16 = jnp.bfloat16
I32 = jnp.int32

LANES = 128
SUBLANES = 8
VMEM_LIMIT_BYTES = 56 * 1024 * 1024

HEAD_DIM = 64
N_Q_HEADS = 8
N_KV_HEADS = 2
ATTN_WIDTH = N_Q_HEADS * HEAD_DIM
WINDOW = 128
PAST_LEN = 8192
ROPE_THETA = 10000.0
N_GMLP_HEADS = 8
GMLP_WIDTH = N_GMLP_HEADS * HEAD_DIM
CHUNK = 128
MEM_HEADS = 4
MEM_HEAD_DIM = 64
PEER_HEADS = 8
N_KEYS = 128
PEER_TOPK = 16
D_HALF = 128
EPS = 1e-6
NEG_INF = -1e30
SQRT_HALF = float(np.sqrt(0.5))

GATE_PITCH = N_KEYS + SUBLANES
GATE_TILES_PER_STEP = 4

NT_DIMS = (((1,), (1,)), ((), ()))


def _params(*semantics):
    return pltpu.CompilerParams(dimension_semantics=semantics, vmem_limit_bytes=VMEM_LIMIT_BYTES)


def _full(shape):
    zeros = (0,) * len(shape)
    return pl.BlockSpec(shape, lambda *_: zeros)


def _rms(x, g):
    return x * lax.rsqrt(jnp.mean(x * x, axis=-1, keepdims=True) + EPS) * g


def _gelu(x):
    return 0.5 * x * (1.0 + lax.erf(x * SQRT_HALF))


def _dot(a, b):
    return jnp.dot(a, b, preferred_element_type=F32)


def _dot_nt(a, b):
    return lax.dot_general(a, b, NT_DIMS, preferred_element_type=F32)


def _split(x):
    hi = x.astype(BF16)
    return hi, (x - hi.astype(F32)).astype(BF16)


def _half_masks():
    lane = lax.broadcasted_iota(I32, (1, LANES), 1)
    low = lane < HEAD_DIM
    return low, low.astype(F32), 1.0 - low.astype(F32)


def _mix_in_kernel(x_ref, g_ref, w_ref, cos_ref, sin_ref, gs_ref, seg_ref,
                   q_ref, k_ref, v_ref, gu_ref, gv_ref):
    xn = _rms(x_ref[...], g_ref[...]).astype(BF16)
    proj = _dot(xn, w_ref[...])
    cosf = cos_ref[...]
    sinf = sin_ref[...]
    lane = lax.broadcasted_iota(I32, (1, LANES), 1)
    first_half = (lane & (HEAD_DIM - 1)) < HEAD_DIM // 2

    def rope(c):
        partner = jnp.where(first_half, pltpu.roll(c, LANES - HEAD_DIM // 2, 1),
                            pltpu.roll(c, HEAD_DIM // 2, 1))
        return c * cosf + partner * sinf

    for c in range(ATTN_WIDTH // LANES):
        sl = slice(c * LANES, (c + 1) * LANES)
        q_ref[:, sl] = rope(proj[:, sl]) * (HEAD_DIM ** -0.5)
    k_ref[...] = rope(proj[:, 512:640])
    v_ref[...] = proj[:, 640:768]
    gu_ref[...] = _gelu(proj[:, 768:1280])
    gv = _gelu(proj[:, 1280:1792])
    hi, lo = _split(gv * gv)
    seg = seg_ref[...]
    ms = _dot(hi, seg) + _dot(lo, seg)
    gv_ref[...] = gv * lax.rsqrt(ms + EPS) * gs_ref[...]


def _mix_in(x, g, w_in, cosf, sinf, gs, seg, *, rows, table_blocks):
    t = x.shape[0]
    d = x.shape[1]
    row = lambda width: pl.BlockSpec((rows, width), lambda i: (i, 0))
    tab = pl.BlockSpec((rows, LANES), lambda i: (i % table_blocks, 0))
    return pl.pallas_call(
        _mix_in_kernel,
        grid=(t // rows,),
        in_specs=[row(d), _full(g.shape), _full(w_in.shape), tab, tab, _full(gs.shape), _full(seg.shape)],
        out_specs=[row(512), row(128), row(128), row(512), row(512)],
        out_shape=[jax.ShapeDtypeStruct((t, 512), F32), jax.ShapeDtypeStruct((t, 128), F32),
                   jax.ShapeDtypeStruct((t, 128), F32), jax.ShapeDtypeStruct((t, 512), F32),
                   jax.ShapeDtypeStruct((t, 512), F32)],
        compiler_params=_params("parallel"),
        name="mix_in",
    )(x, g, w_in, cosf, sinf, gs, seg)


def _stack_heads(q, lowf, highf):
    parts = [(q[:, c * LANES:(c + 1) * LANES] * m).astype(BF16)
             for m in (lowf, highf) for c in range(4)]
    return jnp.concatenate(parts, axis=0)


def _sink_softmax(s, sink):
    m = jnp.maximum(jnp.max(s, axis=-1, keepdims=True), sink)
    p = jnp.exp(s - m)
    den = jnp.sum(p, axis=-1, keepdims=True) + jnp.exp(sink - m)
    return (p / den).astype(BF16)


def _merge_out(attn, sgu, ga, gg, wo, x):
    merged = jnp.concatenate([_rms(attn, ga), _rms(sgu, gg)], axis=1).astype(BF16)
    return x + _dot(merged, wo)


def _mix_prompt_kernel(sinks_ref, q_ref, kp_ref, kc_ref, vp_ref, vc_ref, gu_ref, gv_ref, x_ref,
                       ws_ref, bias_ref, ga_ref, gg_ref, wo_ref, o_ref):
    n = pl.program_id(1)
    low, lowf, highf = _half_masks()
    blk = WINDOW
    nsub = q_ref.shape[0] // blk
    sub = [slice(s * blk, (s + 1) * blk) for s in range(nsub)]
    band = [slice(s * blk, (s + 2) * blk) for s in range(nsub)]
    kall = jnp.concatenate([kp_ref[...], kc_ref[...]], axis=0).astype(BF16)
    vall = jnp.concatenate([vp_ref[...], vc_ref[...]], axis=0).astype(BF16)
    q = q_ref[...]
    i = lax.broadcasted_iota(I32, (blk, 2 * blk), 0)
    j = lax.broadcasted_iota(I32, (blk, 2 * blk), 1)
    in_window = (j > i) & (j <= i + blk)
    first_key = jnp.where(n > 0, 0, blk)
    valid = [in_window & (j >= first_key)] + [in_window] * (nsub - 1)
    scores = [_dot_nt(_stack_heads(q[sub[s]], lowf, highf), kall[band[s]]) for s in range(nsub)]
    probs = [jnp.concatenate(
        [_sink_softmax(jnp.where(valid[s], scores[s][h * blk:(h + 1) * blk], NEG_INF), sinks_ref[h])
         for h in range(N_Q_HEADS)], axis=0) for s in range(nsub)]
    outs = [_dot(probs[s], vall[band[s]]) for s in range(nsub)]
    attn = jnp.concatenate([jnp.concatenate(
        [jnp.where(low, o[c * blk:(c + 1) * blk], o[(4 + c) * blk:(5 + c) * blk]) for c in range(4)], axis=1)
        for o in outs], axis=0)

    gvb = gv_ref[...].astype(BF16)
    r = lax.broadcasted_iota(I32, (CHUNK, CHUNK), 0)
    c_ = lax.broadcasted_iota(I32, (CHUNK, CHUNK), 1)
    tril = r >= c_
    w = [jnp.where(tril, ws_ref[h], 0.0).astype(BF16) for h in range(N_GMLP_HEADS)]
    bias = bias_ref[...]
    mixed = jnp.concatenate([jnp.concatenate(
        [jnp.where(low, _dot(w[2 * c], gvb[sub[s], c * LANES:(c + 1) * LANES]),
                   _dot(w[2 * c + 1], gvb[sub[s], c * LANES:(c + 1) * LANES])) for c in range(4)], axis=1) + bias
        for s in range(nsub)], axis=0)
    sgu = gu_ref[...] * mixed
    o_ref[...] = _merge_out(attn, sgu, ga_ref[...], gg_ref[...], wo_ref[...], x_ref[...])


def _mix_prompt(sinks, q, k, v, gu, gv, x, ws, bias, ga, gg, wo, *, batch, seq, rows):
    nb = seq // rows
    per = rows // WINDOW
    cur = lambda width: pl.BlockSpec((rows, width), lambda b, n: (b * nb + n, 0))
    prev = lambda width: pl.BlockSpec(
        (WINDOW, width), lambda b, n: (b * nb * per + jnp.maximum(n * per - 1, 0), 0))
    return pl.pallas_call(
        _mix_prompt_kernel,
        grid=(batch, nb),
        in_specs=[pl.BlockSpec(memory_space=pltpu.SMEM),
                  cur(512), prev(128), cur(128), prev(128), cur(128), cur(512), cur(512), cur(x.shape[1]),
                  _full(ws.shape), _full(bias.shape), _full(ga.shape), _full(gg.shape), _full(wo.shape)],
        out_specs=cur(x.shape[1]),
        out_shape=jax.ShapeDtypeStruct(x.shape, F32),
        compiler_params=_params("parallel", "parallel"),
        name="mix_prompt",
    )(sinks, q, k, k, v, v, gu, gv, x, ws, bias, ga, gg, wo)


def _mix_sample_kernel(q_ref, kn_ref, vn_ref, gu_ref, gv_ref, x_ref, ck_ref, cv_ref,
                       sink_ref, ctab_ref, bias_ref, ga_ref, gg_ref, wo_ref,
                       o_ref, nk_ref, nv_ref, attn_s, sgu_s):
    nseq = ck_ref.shape[0]
    t = q_ref.shape[0] // nseq
    w = ck_ref.shape[1]
    low, lowf, highf = _half_masks()
    rows = N_Q_HEADS * t
    band = 2 * w
    i = lax.broadcasted_iota(I32, (rows, band), 0) & (t - 1)
    j = lax.broadcasted_iota(I32, (rows, band), 1)
    valid = (j > i) & (j <= i + w)
    sink = sink_ref[:, 0:1]
    pad = jnp.zeros((band - w - t, LANES), F32)

    seq_rows = [slice(b * t, (b + 1) * t) for b in range(nseq)]
    scores = []
    for b, rs in enumerate(seq_rows):
        qs = _stack_heads(q_ref[rs, :], lowf, highf)
        kb = jnp.concatenate([ck_ref[b], kn_ref[rs, :], pad], axis=0).astype(BF16)
        scores.append(jnp.where(valid, _dot_nt(qs, kb), NEG_INF))
    probs = [_sink_softmax(s, sink) for s in scores]
    for b, rs in enumerate(seq_rows):
        vc, vn = cv_ref[b], vn_ref[rs, :]
        o_all = _dot(probs[b], jnp.concatenate([vc, vn, pad], axis=0).astype(BF16))
        for c in range(4):
            attn_s[rs, c * LANES:(c + 1) * LANES] = jnp.where(
                low, o_all[c * t:(c + 1) * t], o_all[(4 + c) * t:(5 + c) * t])
        nk_ref[b] = jnp.concatenate([ck_ref[b, t:, :], kn_ref[rs, :]], axis=0)
        nv_ref[b] = jnp.concatenate([vc[t:], vn], axis=0)
    for b, rs in enumerate(seq_rows):
        gvb = gv_ref[rs, :]
        mixed = ctab_ref[0] * gvb[0:1, :]
        for r in range(1, t):
            mixed = mixed + ctab_ref[r] * gvb[r:r + 1, :]
        sgu_s[rs, :] = gu_ref[rs, :] * (mixed + bias_ref[...])
    o_ref[...] = _merge_out(attn_s[...], sgu_s[...], ga_ref[...], gg_ref[...], wo_ref[...], x_ref[...])


def _mix_sample(q, k, v, gu, gv, x, ck, cv, sink_tab, ctab, bias, ga, gg, wo, *, group):
    nseq, w, _ = ck.shape
    t = x.shape[0] // nseq
    assert t == SUBLANES and w == WINDOW and nseq % group == 0
    rows = group * t
    row = lambda width: pl.BlockSpec((rows, width), lambda i: (i, 0))
    cache = pl.BlockSpec((group, w, LANES), lambda i: (i, 0, 0))
    return pl.pallas_call(
        _mix_sample_kernel,
        grid=(nseq // group,),
        in_specs=[row(512), row(128), row(128), row(512), row(512), row(x.shape[1]), cache, cache,
                  _full(sink_tab.shape), _full(ctab.shape), _full(bias.shape),
                  _full(ga.shape), _full(gg.shape), _full(wo.shape)],
        out_specs=[row(x.shape[1]), cache, cache],
        out_shape=[jax.ShapeDtypeStruct(x.shape, F32), jax.ShapeDtypeStruct(ck.shape, F32),
                   jax.ShapeDtypeStruct(cv.shape, F32)],
        scratch_shapes=[pltpu.VMEM((rows, 512), F32), pltpu.VMEM((rows, 512), F32)],
        compiler_params=_params("parallel"),
        name="mix_sample",
    )(q, k, v, gu, gv, x, ck, cv, sink_tab, ctab, bias, ga, gg, wo)


def _mem_kv_kernel(m_ref, g_ref, w_ref, k_ref, v_ref):
    mn = _rms(m_ref[...], g_ref[...]).astype(BF16)
    kv = _dot(mn, w_ref[...])
    half = kv.shape[1] // 2
    k_ref[...] = kv[:, :half]
    v_ref[...] = kv[:, half:]


def _mem_kv(mem, g, w_kv, *, rows):
    t, d = mem.shape
    width = w_kv.shape[1] // 2
    row = lambda wd: pl.BlockSpec((rows, wd), lambda i: (i, 0))
    return pl.pallas_call(
        _mem_kv_kernel,
        grid=(t // rows,),
        in_specs=[row(d), _full(g.shape), _full(w_kv.shape)],
        out_specs=[row(width), row(width)],
        out_shape=[jax.ShapeDtypeStruct((t, width), F32)] * 2,
        compiler_params=_params("parallel"),
        name="mem_kv",
    )(mem, g, w_kv)


def _softmax(s):
    p = jnp.exp(s - jnp.max(s, axis=-1, keepdims=True))
    return (p / jnp.sum(p, axis=-1, keepdims=True)).astype(BF16)


def _mem_heads(q, mk, mv, low, lowf, highf):
    rows = q.shape[0]
    outs = []
    for c in range(MEM_HEADS // 2):
        sl = slice(c * LANES, (c + 1) * LANES)
        qc = q[:, sl]
        q2 = jnp.concatenate([(qc * lowf).astype(BF16), (qc * highf).astype(BF16)], axis=0)
        o = _dot(_softmax(_dot_nt(q2, mk[:, sl].astype(BF16))), mv[:, sl].astype(BF16))
        outs.append(jnp.where(low, o[:rows], o[rows:]))
    return jnp.concatenate(outs, axis=1)


def _mem_attn_prompt_kernel(x_ref, g_ref, wq_ref, mk_ref, mv_ref, wo_ref, o_ref):
    low, lowf, highf = _half_masks()
    x = x_ref[...]
    q = _dot(_rms(x, g_ref[...]).astype(BF16), wq_ref[...]) * (MEM_HEAD_DIM ** -0.5)
    o = _mem_heads(q, mk_ref[...], mv_ref[...], low, lowf, highf)
    o_ref[...] = x + _dot(o.astype(BF16), wo_ref[...])


def _mem_attn_prompt(x, g, wq, mk, mv, wo, *, batch, seq, rows):
    n_mem = mk.shape[0] // batch
    nb = seq // rows
    row = pl.BlockSpec((rows, x.shape[1]), lambda b, n: (b * nb + n, 0))
    mem = pl.BlockSpec((n_mem, mk.shape[1]), lambda b, n: (b, 0))
    return pl.pallas_call(
        _mem_attn_prompt_kernel,
        grid=(batch, nb),
        in_specs=[row, _full(g.shape), _full(wq.shape), mem, mem, _full(wo.shape)],
        out_specs=row,
        out_shape=jax.ShapeDtypeStruct(x.shape, F32),
        compiler_params=_params("parallel", "parallel"),
        name="mem_attn_prompt",
    )(x, g, wq, mk, mv, wo)


def _mem_attn_sample_kernel(x_ref, g_ref, wq_ref, mk_ref, mv_ref, wo_ref, o_ref, q_s, a_s):
    nseq = mk_ref.shape[0]
    t = x_ref.shape[0] // nseq
    low, lowf, highf = _half_masks()
    x = x_ref[...]
    q_s[...] = _dot(_rms(x, g_ref[...]).astype(BF16), wq_ref[...]) * (MEM_HEAD_DIM ** -0.5)
    for c in range(MEM_HEADS // 2):
        sl = slice(c * LANES, (c + 1) * LANES)
        scores = []
        for b in range(nseq):
            qc = q_s[b * t:(b + 1) * t, sl]
            q2 = jnp.concatenate([(qc * lowf).astype(BF16), (qc * highf).astype(BF16)], axis=0)
            scores.append(_dot_nt(q2, mk_ref[b, :, sl].astype(BF16)))
        p = _softmax(jnp.concatenate(scores, axis=0))
        for b in range(nseq):
            o = _dot(p[2 * t * b:2 * t * (b + 1)], mv_ref[b, :, sl].astype(BF16))
            a_s[b * t:(b + 1) * t, sl] = jnp.where(low, o[:t], o[t:])
    o_ref[...] = x + _dot(a_s[...].astype(BF16), wo_ref[...])


def _mem_attn_sample(x, g, wq, mk, mv, wo, *, group):
    nseq, n_mem, width = mk.shape
    t = x.shape[0] // nseq
    rows = group * t
    row = pl.BlockSpec((rows, x.shape[1]), lambda i: (i, 0))
    mem = pl.BlockSpec((group, n_mem, width), lambda i: (i, 0, 0))
    return pl.pallas_call(
        _mem_attn_sample_kernel,
        grid=(nseq // group,),
        in_specs=[row, _full(g.shape), _full(wq.shape), mem, mem, _full(wo.shape)],
        out_specs=row,
        out_shape=jax.ShapeDtypeStruct(x.shape, F32),
        scratch_shapes=[pltpu.VMEM((rows, width), F32), pltpu.VMEM((rows, width), F32)],
        compiler_params=_params("parallel"),
        name="mem_attn_sample",
    )(x, g, wq, mk, mv, wo)


def _top16(s):
    nrows, cols = s.shape
    r = lax.broadcasted_iota(I32, (nrows, cols), 0).astype(F32)
    slot = lax.broadcasted_iota(I32, (PEER_TOPK, cols), 0)
    vals = jnp.zeros((PEER_TOPK, cols), F32)
    idxs = jnp.zeros((PEER_TOPK, cols), F32)
    for p in range(PEER_TOPK):
        m = jnp.max(s, axis=0, keepdims=True)
        am = jnp.min(jnp.where(s == m, r, float(nrows)), axis=0, keepdims=True)
        vals = jnp.where(slot == p, m, vals)
        idxs = jnp.where(slot == p, am, idxs)
        s = jnp.where(r == am, -jnp.inf, s)
    return vals, idxs


def _sort16_pairs():
    def merge(lo, hi, r):
        step = r * 2
        if step < hi - lo:
            yield from merge(lo, hi, step)
            yield from merge(lo + r, hi, step)
            yield from [(i, i + r) for i in range(lo + r, hi - r, step)]
        else:
            yield (lo, lo + r)

    def sort(lo, hi):
        if hi > lo:
            mid = lo + (hi - lo) // 2
            yield from sort(lo, mid)
            yield from sort(mid + 1, hi)
            yield from merge(lo, hi, 1)

    return tuple(sort(0, PEER_TOPK - 1))


SORT16 = _sort16_pairs()


def _cmpx(v, pays, i, j):
    swap = v[j] > v[i]
    hi, lo = jnp.maximum(v[i], v[j]), jnp.minimum(v[i], v[j])
    for p in pays:
        p[i], p[j] = jnp.where(swap, p[j], p[i]), jnp.where(swap, p[i], p[j])
    v[i], v[j] = hi, lo


def _merge16(va, pa, vb, pb):
    n = PEER_TOPK
    v, pays = [], [[] for _ in pa]
    for i in range(n):
        other = vb[n - 1 - i]
        if other is None:
            v.append(va[i])
            for k in range(len(pa)):
                pays[k].append(pa[k][i])
            continue
        take = other > va[i]
        v.append(jnp.maximum(va[i], other))
        for k in range(len(pa)):
            pays[k].append(jnp.where(take, pb[k][n - 1 - i], pa[k][i]))
    d = n // 2
    while d:
        for i in range(n):
            if not i & d:
                _cmpx(v, pays, i, i + d)
        d //= 2
    return v, pays


def _xor_rows(x, d, sub):
    if d == SUBLANES // 2:
        return pltpu.roll(x, d, 0)
    return jnp.where((sub & d) == 0, pltpu.roll(x, SUBLANES - d, 0), pltpu.roll(x, d, 0))


def _row(x, k):
    return jnp.broadcast_to(x[k:k + 1, :], x.shape)


def _route_sorted(s_s, rows, i1_s, i2_s, gt_s):
    n = PEER_TOPK
    sub = lax.broadcasted_iota(I32, (SUBLANES, LANES), 0)
    subf = sub.astype(F32)
    bit2 = (sub & 4) == 0
    bit1 = (sub & 2) == 0
    halves = [slice(0, LANES), slice(LANES, 2 * LANES)]
    bad = jnp.zeros((SUBLANES, LANES), F32)

    def packed_merge(mask, d, xa, xb):
        va, pa = xa
        vb, pb = xb
        lv = [jnp.where(mask, a, b) for a, b in zip(va, vb)]
        rv = [_xor_rows(jnp.where(mask, b, a), d, sub) for a, b in zip(va, vb)]
        lp = [[jnp.where(mask, a, b) for a, b in zip(qa, qb)] for qa, qb in zip(pa, pb)]
        rp = [[_xor_rows(jnp.where(mask, b, a), d, sub) for a, b in zip(qa, qb)] for qa, qb in zip(pa, pb)]
        return _merge16(lv, lp, rv, rp)

    def self_merge(d, x):
        v, p = x
        return _merge16(v, p, [_xor_rows(a, d, sub) for a in v], [[_xor_rows(a, d, sub) for a in q] for q in p])

    def decreasing(v):
        ok = v[0] > v[1]
        flag = jnp.where(ok, 0.0, 1.0)
        for r in range(1, n - 1):
            flag = jnp.where(v[r] > v[r + 1], flag, 1.0)
        return flag

    sorted_lists = {}
    for c in range(2):
        for part in range(2):
            v = [s_s[c, SUBLANES * g:SUBLANES * (g + 1), halves[part]] for g in range(N_KEYS // SUBLANES)]
            ix = [subf + float(SUBLANES * g) for g in range(N_KEYS // SUBLANES)]
            pays = [ix]
            for i, j in SORT16:
                _cmpx(v, pays, i, j)
            sorted_lists[c, part] = (v, pays)
    x1 = [packed_merge(bit2, 4, sorted_lists[c, 0], sorted_lists[c, 1]) for c in range(2)]
    x2 = packed_merge(bit1, 2, x1[0], x1[1])
    tv, (ti,) = self_merge(1, x2)
    bad = jnp.maximum(bad, decreasing(tv))
    for c in range(2):
        for part in range(2):
            thr = _row(tv[n - 1], 4 * part + 2 * c)
            cnt = jnp.zeros((SUBLANES, LANES), F32)
            for g in range(N_KEYS // SUBLANES):
                cnt = cnt + jnp.where(s_s[c, SUBLANES * g:SUBLANES * (g + 1), halves[part]] >= thr, 1.0, 0.0)
            total = jnp.sum(cnt, axis=0, keepdims=True)
            bad = jnp.maximum(bad, jnp.broadcast_to(jnp.where(total == float(n), 0.0, 1.0), bad.shape))

    half = n // 2
    lists, tails = [], []
    for part in range(2):
        base = 4 * part
        s1 = [_row(tv[r], base) for r in range(n)]
        a1 = [_row(ti[r], base) for r in range(n)]
        s2 = [_row(tv[r], base + 2) for r in range(n)]
        a2 = [_row(ti[r], base + 2) for r in range(n)]
        p1, ip1 = s1[0], a1[0]
        for p in range(1, half):
            p1 = jnp.where(sub == p, s1[p], p1)
            ip1 = jnp.where(sub == p, a1[p], ip1)
        lists.append(([p1 + s2[q] for q in range(n)], [[ip1] * n, list(a2)]))
        tails.append(([s1[half + i] + s2[0] for i in range(half)],
                      [[a1[half + i] for i in range(half)], [a2[0]] * half]))
    y = packed_merge(bit2, 4, lists[0], lists[1])
    y = self_merge(2, y)
    y = self_merge(1, y)
    dv = [jnp.where(bit2, a, b) for a, b in zip(tails[0][0], tails[1][0])] + [None] * half
    dp = [[jnp.where(bit2, a, b) for a, b in zip(qa, qb)] + [None] * half
          for qa, qb in zip(tails[0][1], tails[1][1])]
    top, (sel1, sel2) = _merge16(y[0], y[1], dv, dp)
    bad = jnp.maximum(bad, decreasing(top))
    for part in range(2):
        thr = _row(top[n - 1], 4 * part)
        cnt = jnp.zeros((SUBLANES, LANES), F32)
        for q in range(n):
            cnt = cnt + jnp.where(lists[part][0][q] >= thr, 1.0, 0.0)
        total = jnp.sum(cnt, axis=0, keepdims=True)
        for i in range(half):
            total = total + jnp.where(tails[part][0][i][0:1, :] >= thr[0:1, :], 1.0, 0.0)
        bad = jnp.maximum(bad, jnp.broadcast_to(jnp.where(total == float(n), 0.0, 1.0), bad.shape))

    e = [jnp.exp(t - top[0]) for t in top]
    z = e[0]
    for k in range(1, n):
        z = z + e[k]
    gate = [ek / z for ek in e]
    for src, dst in ((sel1, i1_s), (sel2, i2_s), (gate, gt_s)):
        for grp in range(2):
            for part in range(2):
                tile = None
                for k in range(SUBLANES):
                    slab = src[grp * SUBLANES + k]
                    if (k < 4) != (part == 0):
                        slab = pltpu.roll(slab, 4, 0)
                    tile = slab if tile is None else jnp.where(sub == k, slab, tile)
                dst[rows[grp], halves[part]] = tile
    return jnp.max(bad)


def _peer_route_kernel(x_ref, g_ref, wh_ref, wl_ref, kh_ref, kl_ref,
                       xn_ref, i1_ref, i2_ref, gate_ref, q_s, s_s, i1_s, i2_s, gt_s):
    tb = x_ref.shape[0]
    xn = _rms(x_ref[...], g_ref[...])
    xh, xl = _split(xn)
    xn_ref[...] = xh
    wh = wh_ref[...]
    q = _dot(xh, wh) + _dot(xh, wl_ref[...]) + _dot(xl, wh)
    for hc in range(2 * PEER_HEADS):
        q_s[hc] = q[:, hc * D_HALF:(hc + 1) * D_HALF]

    def head_pair(k, carry):
        tied = []
        for hh in range(2):
            h = 2 * k + hh
            for c in range(2):
                qh, ql = _split(q_s[2 * h + c])
                kh = kh_ref[2 * h + c]
                s_s[hh, c] = _dot_nt(kh, qh) + _dot_nt(kh, ql) + _dot_nt(kl_ref[2 * h + c], qh)
            rows = [pl.ds(pl.multiple_of(h * PEER_TOPK + r, SUBLANES), SUBLANES) for r in (0, SUBLANES)]
            tied.append(_route_sorted(s_s.at[hh], rows, i1_s, i2_s, gt_s))

        for hh in range(2):
            @pl.when(tied[hh] > 0.0)
            def _():
                for part in range(tb // LANES):
                    head_part(2 * k + hh, hh, part)

        return carry

    def head_part(h, hh, part):
        cols = slice(part * LANES, (part + 1) * LANES)
        (s1, i1), (s2, i2) = [_top16(s_s[hh, c, :, cols]) for c in range(2)]
        tb = LANES
        half = PEER_TOPK // 2
        cand = jnp.concatenate(
            [jnp.broadcast_to(s1[0:1], (PEER_TOPK, tb)) + s2]
            + [jnp.broadcast_to(s1[p:p + 1], (half, tb)) + s2[:half] for p in range(1, half)]
            + [s1[half:] + jnp.broadcast_to(s2[0:1], (half, tb))], axis=0)
        top, cidx = _top16(cand)
        ci = cidx.astype(I32)
        mid = ci - PEER_TOPK
        tail = PEER_TOPK + half * (half - 1)
        pi = jnp.where(ci < PEER_TOPK, 0,
                       jnp.where(ci < tail, 1 + (mid >> (half.bit_length() - 1)), ci - tail + half))
        qi = jnp.where(ci < PEER_TOPK, ci, jnp.where(ci < tail, mid & (half - 1), 0))
        e1 = jnp.zeros((PEER_TOPK, tb), F32)
        e2 = jnp.zeros((PEER_TOPK, tb), F32)
        for p in range(PEER_TOPK):
            e1 = jnp.where(pi == p, jnp.broadcast_to(i1[p:p + 1], (PEER_TOPK, tb)), e1)
            e2 = jnp.where(qi == p, jnp.broadcast_to(i2[p:p + 1], (PEER_TOPK, tb)), e2)
        e = jnp.exp(top - jnp.max(top, axis=0, keepdims=True))
        rs = pl.ds(pl.multiple_of(h * PEER_TOPK, PEER_TOPK), PEER_TOPK)
        i1_s[rs, cols] = e1
        i2_s[rs, cols] = e2
        gt_s[rs, cols] = e / jnp.sum(e, axis=0, keepdims=True)

    lax.fori_loop(0, PEER_HEADS // 2, head_pair, 0)
    i1_ref[...] = i1_s[...].T.astype(I32)
    i2_ref[...] = i2_s[...].T.astype(I32)
    gate_ref[...] = gt_s[...].T


def _peer_route(x, g, wh, wl, kh, kl):
    t, d = x.shape
    tb = 2 * LANES
    assert t % tb == 0
    sel = PEER_HEADS * PEER_TOPK
    row = lambda width: pl.BlockSpec((tb, width), lambda i: (i, 0))
    return pl.pallas_call(
        _peer_route_kernel,
        grid=(t // tb,),
        in_specs=[row(d), _full(g.shape), _full(wh.shape), _full(wl.shape), _full(kh.shape), _full(kl.shape)],
        out_specs=[row(d), row(sel), row(sel), row(sel)],
        out_shape=[jax.ShapeDtypeStruct((t, d), BF16), jax.ShapeDtypeStruct((t, sel), I32),
                   jax.ShapeDtypeStruct((t, sel), I32), jax.ShapeDtypeStruct((t, sel), F32)],
        scratch_shapes=[pltpu.VMEM((2 * PEER_HEADS, tb, D_HALF), F32), pltpu.VMEM((2, 2, N_KEYS, tb), F32),
                        pltpu.VMEM((sel, tb), F32), pltpu.VMEM((sel, tb), F32), pltpu.VMEM((sel, tb), F32)],
        compiler_params=_params("parallel"),
        name="peer_route",
    )(x, g, wh, wl, kh, kl)


def _peer_gates_kernel(i1_ref, i2_ref, gate_ref, o_ref, g_s):
    tb = i1_ref.shape[0]
    sel = i1_ref.shape[1]
    key = lax.broadcasted_iota(I32, (N_KEYS, sel), 0)
    zero = jnp.zeros((N_KEYS, sel), BF16)

    tile = 2 * SUBLANES

    def build(tidx):
        w1, w2 = [], []
        for half in range(2):
            t0 = pl.multiple_of(tidx * tile + half * SUBLANES, SUBLANES)
            i1g = i1_ref[pl.ds(t0, SUBLANES), :]
            i2g = i2_ref[pl.ds(t0, SUBLANES), :]
            gtg = gate_ref[pl.ds(t0, SUBLANES), :]
            w1 += [jnp.where(key == i1g[s:s + 1, :], gtg[s:s + 1, :], 0.0).astype(BF16) for s in range(SUBLANES)]
            w2 += [jnp.where(key == i2g[s:s + 1, :], 1.0, 0.0).astype(BF16) for s in range(SUBLANES)]
        for s in range(0, tile, 2):
            lhs = jnp.concatenate([w1[s], w1[s + 1]], axis=1)
            rhs = jnp.concatenate([jnp.concatenate([w2[s], zero], axis=1),
                                   jnp.concatenate([zero, w2[s + 1]], axis=1)], axis=0)
            g = _dot_nt(lhs, rhs)
            word = pltpu.pack_elementwise([g[:, :N_KEYS], g[:, N_KEYS:]], packed_dtype=BF16)
            pair = tidx * SUBLANES + s // 2
            g_s[pl.ds(pl.multiple_of(pair * GATE_PITCH, SUBLANES), N_KEYS), :] = word

    def relayout(tidx):
        rows = pl.ds(pl.multiple_of(tidx * tile, tile), tile)
        for a in range(N_KEYS):
            words = g_s[pl.ds(tidx * SUBLANES * GATE_PITCH + a, SUBLANES, stride=GATE_PITCH), :]
            o_ref[rows, a * N_KEYS:(a + 1) * N_KEYS] = pltpu.bitcast(words, BF16)

    group = GATE_TILES_PER_STEP
    n_groups = tb // (tile * group)
    for u in range(group):
        build(u)

    def step(gidx, carry):
        for u in range(group):
            relayout((gidx - 1) * group + u)
        for u in range(group):
            build(gidx * group + u)
        return carry

    lax.fori_loop(1, n_groups, step, 0)
    for u in range(group):
        relayout((n_groups - 1) * group + u)


def _peer_gates(i1, i2, gate, *, rows):
    t, sel = i1.shape
    row = pl.BlockSpec((rows, sel), lambda i: (i, 0))
    return pl.pallas_call(
        _peer_gates_kernel,
        grid=(t // rows,),
        in_specs=[row, row, row],
        out_specs=pl.BlockSpec((rows, N_KEYS * N_KEYS), lambda i: (i, 0)),
        out_shape=jax.ShapeDtypeStruct((t, N_KEYS * N_KEYS), BF16),
        scratch_shapes=[pltpu.VMEM((rows // 2 * GATE_PITCH, N_KEYS), jnp.uint32)],
        compiler_params=_params("parallel"),
        name="peer_gates",
    )(i1, i2, gate)


def _peer_dense_kernel(xn_ref, gates_ref, u_ref, v_ref, x_ref, g_ref, o_ref, acc):
    k = pl.program_id(1)

    @pl.when(k == 0)
    def _():
        acc[...] = jnp.zeros_like(acc)

    h = _dot_nt(xn_ref[...], u_ref[...])
    a = (_gelu(h) * gates_ref[...].astype(F32)).astype(BF16)
    acc[...] += _dot(a, v_ref[...])

    @pl.when(k == pl.num_programs(1) - 1)
    def _():
        o_ref[...] = _rms(x_ref[...] + acc[...], g_ref[...])


def _peer_dense(xn, gates, u, v, x, g, *, rows, experts):
    t, d = x.shape
    n_exp = u.shape[0]
    tok = pl.BlockSpec((rows, d), lambda i, k: (i, 0))
    tab = pl.BlockSpec((experts, d), lambda i, k: (k, 0))
    return pl.pallas_call(
        _peer_dense_kernel,
        grid=(t // rows, n_exp // experts),
        in_specs=[tok, pl.BlockSpec((rows, experts), lambda i, k: (i, k)), tab, tab, tok, _full(g.shape)],
        out_specs=tok,
        out_shape=jax.ShapeDtypeStruct((t, d), F32),
        scratch_shapes=[pltpu.VMEM((rows, d), F32)],
        compiler_params=_params("parallel", "arbitrary"),
        name="peer_dense",
    )(xn, gates, u, v, x, g)


def _peer(x, g_ffn, wh, wl, kh, kl, u, v, g_final):
    t = x.shape[0]
    xn, i1, i2, gate = _peer_route(x, g_ffn, wh, wl, kh, kl)
    gates = _peer_gates(i1, i2, gate, rows=2 * LANES)
    return _peer_dense(xn, gates, u, v, x, g_final, rows=min(512, t), experts=2048)


def _head_perm():
    new = np.zeros(ATTN_WIDTH, np.int32)
    for h in range(N_Q_HEADS):
        dst = (h % 4) * LANES + (h // 4) * HEAD_DIM
        new[dst:dst + HEAD_DIM] = np.arange(h * HEAD_DIM, (h + 1) * HEAD_DIM)
    return new


def _rope_tables(pos):
    half = HEAD_DIM // 2
    inv = ROPE_THETA ** (-jnp.arange(half, dtype=F32) / half)
    ang = pos.astype(F32)[:, None] * inv[None, :]
    cos, sin = jnp.cos(ang), jnp.sin(ang)
    reps = LANES // HEAD_DIM
    return (jnp.tile(jnp.concatenate([cos, cos], axis=1), (1, reps)),
            jnp.tile(jnp.concatenate([-sin, sin], axis=1), (1, reps)))


def kernel(x_prompt, x_sample, mem_prompt, cache_swa_k, cache_swa_v, cache_mem_k, cache_mem_v, g_mix, w_in, attn_sinks, g_sgu, w_spatial, b_spatial, g_attn_out, g_gmlp_out, w_out, g_cross, g_mem, w_cq, w_mk, w_mv, w_co, g_ffn, w_peer_q, peer_sub_keys, peer_u, peer_v, g_final):
    batch, seq, d = x_prompt.shape
    dec_batch, dec_seq, _ = x_sample.shape
    depth = g_mix.shape[0]
    n_mem = mem_prompt.shape[1]
    assert seq % WINDOW == 0 and dec_seq == SUBLANES and cache_swa_k.shape[2] == WINDOW

    perm = _head_perm()
    cos_p, sin_p = _rope_tables(jnp.arange(seq, dtype=I32))
    cos_s, sin_s = _rope_tables(PAST_LEN +(jnp.arange(dec_batch * dec_seq, dtype=I32) % dec_seq))
    seg = jnp.asarray(np.kron(np.eye(N_GMLP_HEADS), np.full((HEAD_DIM, HEAD_DIM), 1.0 / HEAD_DIM)), BF16)
    row2 = lambda a: a.reshape(1, -1)

    xp = x_prompt.reshape(batch * seq, d)
    xs = x_sample.reshape(dec_batch * dec_seq, d)
    outs = {name: [] for name in ("kp", "vp", "mk", "mv", "ks", "vs", "gvs")}
    for l in range(depth):
        w_in_l = jnp.concatenate([w_in[l][:, :ATTN_WIDTH][:, perm], w_in[l][:, ATTN_WIDTH:]], axis=1).astype(BF16)
        w_out_l = jnp.concatenate([w_out[l][:ATTN_WIDTH][perm], w_out[l][ATTN_WIDTH:]], axis=0).astype(BF16)
        ga = row2(g_attn_out[l][perm])
        gg = row2(g_gmlp_out[l])
        gs = row2(g_sgu[l])
        bias = jnp.repeat(b_spatial[l].T, HEAD_DIM, axis=1)
        wt = jnp.tril(w_spatial[l][:, :dec_seq, :dec_seq])
        ctab = jnp.repeat(jnp.transpose(wt, (2, 1, 0)), HEAD_DIM, axis=2)
        sink_tab = jnp.broadcast_to(jnp.repeat(attn_sinks[l], dec_seq)[:, None], (N_Q_HEADS * dec_seq, LANES))

        mix = functools.partial(_mix_in, g=row2(g_mix[l]), w_in=w_in_l, gs=gs, seg=seg)
        rows_p = next(r for r in (1024, 512, WINDOW) if seq % r == 0)
        qp, kp, vp, gup, gvp = mix(xp, cosf=cos_p, sinf=sin_p, rows=rows_p, table_blocks=seq // rows_p)
        rows_s = min(512, dec_batch * dec_seq)
        qs, ks, vs, gus, gvs = mix(xs, cosf=cos_s, sinf=sin_s, rows=rows_s,
                                   table_blocks=dec_batch * dec_seq // rows_s)

        xp = _mix_prompt(attn_sinks[l], qp, kp, vp, gup, gvp, xp, w_spatial[l], bias, ga, gg, w_out_l,
                         batch=batch, seq=seq, rows=2 * WINDOW if seq % (2 * WINDOW) == 0 else WINDOW)
        ck = cache_swa_k[l].reshape(dec_batch, WINDOW, N_KV_HEADS * HEAD_DIM)
        cv = cache_swa_v[l].reshape(dec_batch, WINDOW, N_KV_HEADS * HEAD_DIM)
        xs, nks, nvs = _mix_sample(qs, ks, vs, gus, gvs, xs, ck, cv, sink_tab, ctab, bias[:dec_seq], ga, gg,
                                   w_out_l, group=min(16, dec_batch))

        w_kv = jnp.concatenate([w_mk[l], w_mv[l]], axis=1).astype(BF16)
        mk, mv = _mem_kv(mem_prompt.reshape(batch * n_mem, d), row2(g_mem[l]), w_kv, rows=n_mem)
        wq = w_cq[l].astype(BF16)
        wo = w_co[l].astype(BF16)
        gc = row2(g_cross[l])
        xp = _mem_attn_prompt(xp, gc, wq, mk, mv, wo, batch=batch, seq=seq, rows=512 if seq % 512 == 0 else 256)
        cmk = cache_mem_k[l].reshape(dec_batch, n_mem, MEM_HEADS * MEM_HEAD_DIM)
        cmv = cache_mem_v[l].reshape(dec_batch, n_mem, MEM_HEADS * MEM_HEAD_DIM)
        xs = _mem_attn_sample(xs, gc, wq, cmk, cmv, wo, group=min(8, dec_batch))

        wh, wl = _split(w_peer_q[l])
        kh, kl = _split(peer_sub_keys[l].reshape(2 * PEER_HEADS, N_KEYS, D_HALF))
        u = peer_u[l].astype(BF16)
        v = peer_v[l].astype(BF16)
        last = l == depth - 1
        assert last, "stacked layers need an un-normalised PEER output"
        peer = functools.partial(_peer, g_ffn=row2(g_ffn[l]), wh=wh, wl=wl, kh=kh, kl=kl, u=u, v=v,
                                 g_final=row2(g_final))
        xp = peer(xp)
        xs = peer(xs)

        outs["kp"].append(kp.reshape(batch, seq, N_KV_HEADS, HEAD_DIM)[:, seq - WINDOW:])
        outs["vp"].append(vp.reshape(batch, seq, N_KV_HEADS, HEAD_DIM)[:, seq - WINDOW:])
        outs["mk"].append(mk.reshape(batch, n_mem, MEM_HEADS, MEM_HEAD_DIM))
        outs["mv"].append(mv.reshape(batch, n_mem, MEM_HEADS, MEM_HEAD_DIM))
        outs["ks"].append(nks.reshape(dec_batch, WINDOW, N_KV_HEADS, HEAD_DIM))
        outs["vs"].append(nvs.reshape(dec_batch, WINDOW, N_KV_HEADS, HEAD_DIM))
        outs["gvs"].append(gvs.reshape(dec_batch, dec_seq, N_GMLP_HEADS, HEAD_DIM))

    stack = lambda name: jnp.stack(outs[name])
    return (xp.reshape(batch, seq, d), xs.reshape(dec_batch, dec_seq, d),
            stack("kp"), stack("vp"), stack("mk"), stack("mv"), stack("ks"), stack("vs"), stack("gvs"))
```

```python
import functools

import jax
import jax.numpy as jnp
import numpy as np
from jax import lax
from jax.experimental import pallas as pl
from jax.experimental.pallas import tpu as pltpu

F32 = jnp.float32
BF16 = jnp.bfloat16
I32 = jnp.int32

LANES = 128
SUBLANES = 8
VMEM_LIMIT_BYTES = 56 * 1024 * 1024

HEAD_DIM = 64
N_Q_HEADS = 8
N_KV_HEADS = 2
ATTN_WIDTH = N_Q_HEADS * HEAD_DIM
WINDOW = 128
PAST_LEN = 8192
ROPE_THETA = 10000.0
N_GMLP_HEADS = 8
GMLP_WIDTH = N_GMLP_HEADS * HEAD_DIM
CHUNK = 128
MEM_HEADS = 4
MEM_HEAD_DIM = 64
PEER_HEADS = 8
N_KEYS = 128
PEER_TOPK = 16
D_HALF = 128
EPS = 1e-6
NEG_INF = -1e30
SQRT_HALF = float(np.sqrt(0.5))

GATE_PITCH = N_KEYS + SUBLANES
ROUTE_HEADS_PER_BODY = 4
GATE_TILES_PER_STEP = 4

NT_DIMS = (((1,), (1,)), ((), ()))


def _params(*semantics):
    return pltpu.CompilerParams(dimension_semantics=semantics, vmem_limit_bytes=VMEM_LIMIT_BYTES)


def _full(shape):
    zeros = (0,) * len(shape)
    return pl.BlockSpec(shape, lambda *_: zeros)


def _rms(x, g):
    return x * lax.rsqrt(jnp.mean(x * x, axis=-1, keepdims=True) + EPS) * g


def _gelu(x):
    return 0.5 * x * (1.0 + lax.erf(x * SQRT_HALF))


def _dot(a, b):
    return jnp.dot(a, b, preferred_element_type=F32)


def _dot_nt(a, b):
    return lax.dot_general(a, b, NT_DIMS, preferred_element_type=F32)


def _split(x):
    hi = x.astype(BF16)
    return hi, (x - hi.astype(F32)).astype(BF16)


def _half_masks():
    lane = lax.broadcasted_iota(I32, (1, LANES), 1)
    low = lane < HEAD_DIM
    return low, low.astype(F32), 1.0 - low.astype(F32)


def _mix_in_kernel(x_ref, g_ref, w_ref, cos_ref, sin_ref, gs_ref, seg_ref,
                   q_ref, k_ref, v_ref, gu_ref, gv_ref):
    xn = _rms(x_ref[...], g_ref[...]).astype(BF16)
    proj = _dot(xn, w_ref[...])
    cosf = cos_ref[...]
    sinf = sin_ref[...]
    lane = lax.broadcasted_iota(I32, (1, LANES), 1)
    first_half = (lane & (HEAD_DIM - 1)) < HEAD_DIM // 2

    def rope(c):
        partner = jnp.where(first_half, pltpu.roll(c, LANES - HEAD_DIM // 2, 1),
                            pltpu.roll(c, HEAD_DIM // 2, 1))
        return c * cosf + partner * sinf

    for c in range(ATTN_WIDTH // LANES):
        sl = slice(c * LANES, (c + 1) * LANES)
        q_ref[:, sl] = rope(proj[:, sl]) * (HEAD_DIM ** -0.5)
    k_ref[...] = rope(proj[:, 512:640])
    v_ref[...] = proj[:, 640:768]
    gu_ref[...] = _gelu(proj[:, 768:1280])
    gv = _gelu(proj[:, 1280:1792])
    hi, lo = _split(gv * gv)
    seg = seg_ref[...]
    ms = _dot(hi, seg) + _dot(lo, seg)
    gv_ref[...] = gv * lax.rsqrt(ms + EPS) * gs_ref[...]


def _mix_in(x, g, w_in, cosf, sinf, gs, seg, *, rows, table_blocks):
    t = x.shape[0]
    d = x.shape[1]
    row = lambda width: pl.BlockSpec((rows, width), lambda i: (i, 0))
    tab = pl.BlockSpec((rows, LANES), lambda i: (i % table_blocks, 0))
    return pl.pallas_call(
        _mix_in_kernel,
        grid=(t // rows,),
        in_specs=[row(d), _full(g.shape), _full(w_in.shape), tab, tab, _full(gs.shape), _full(seg.shape)],
        out_specs=[row(512), row(128), row(128), row(512), row(512)],
        out_shape=[jax.ShapeDtypeStruct((t, 512), F32), jax.ShapeDtypeStruct((t, 128), F32),
                   jax.ShapeDtypeStruct((t, 128), F32), jax.ShapeDtypeStruct((t, 512), F32),
                   jax.ShapeDtypeStruct((t, 512), F32)],
        compiler_params=_params("parallel"),
        name="mix_in",
    )(x, g, w_in, cosf, sinf, gs, seg)


def _stack_heads(q, lowf, highf):
    parts = [(q[:, c * LANES:(c + 1) * LANES] * m).astype(BF16)
             for m in (lowf, highf) for c in range(4)]
    return jnp.concatenate(parts, axis=0)


def _sink_softmax(s, sink):
    m = jnp.maximum(jnp.max(s, axis=-1, keepdims=True), sink)
    p = jnp.exp(s - m)
    den = jnp.sum(p, axis=-1, keepdims=True) + jnp.exp(sink - m)
    return (p / den).astype(BF16)


def _merge_out(attn, sgu, ga, gg, wo, x):
    merged = jnp.concatenate([_rms(attn, ga), _rms(sgu, gg)], axis=1).astype(BF16)
    return x + _dot(merged, wo)


def _mix_prompt_kernel(sinks_ref, q_ref, kp_ref, kc_ref, vp_ref, vc_ref, gu_ref, gv_ref, x_ref,
                       ws_ref, bias_ref, ga_ref, gg_ref, wo_ref, o_ref):
    n = pl.program_id(1)
    low, lowf, highf = _half_masks()
    blk = WINDOW
    nsub = q_ref.shape[0] // blk
    sub = [slice(s * blk, (s + 1) * blk) for s in range(nsub)]
    band = [slice(s * blk, (s + 2) * blk) for s in range(nsub)]
    kall = jnp.concatenate([kp_ref[...], kc_ref[...]], axis=0).astype(BF16)
    vall = jnp.concatenate([vp_ref[...], vc_ref[...]], axis=0).astype(BF16)
    q = q_ref[...]
    i = lax.broadcasted_iota(I32, (blk, 2 * blk), 0)
    j = lax.broadcasted_iota(I32, (blk, 2 * blk), 1)
    in_window = (j > i) & (j <= i + blk)
    first_key = jnp.where(n > 0, 0, blk)
    valid = [in_window & (j >= first_key)] + [in_window] * (nsub - 1)
    scores = [_dot_nt(_stack_heads(q[sub[s]], lowf, highf), kall[band[s]]) for s in range(nsub)]
    probs = [jnp.concatenate(
        [_sink_softmax(jnp.where(valid[s], scores[s][h * blk:(h + 1) * blk], NEG_INF), sinks_ref[h])
         for h in range(N_Q_HEADS)], axis=0) for s in range(nsub)]
    outs = [_dot(probs[s], vall[band[s]]) for s in range(nsub)]
    attn = jnp.concatenate([jnp.concatenate(
        [jnp.where(low, o[c * blk:(c + 1) * blk], o[(4 + c) * blk:(5 + c) * blk]) for c in range(4)], axis=1)
        for o in outs], axis=0)

    gvb = gv_ref[...].astype(BF16)
    r = lax.broadcasted_iota(I32, (CHUNK, CHUNK), 0)
    c_ = lax.broadcasted_iota(I32, (CHUNK, CHUNK), 1)
    tril = r >= c_
    w = [jnp.where(tril, ws_ref[h], 0.0).astype(BF16) for h in range(N_GMLP_HEADS)]
    bias = bias_ref[...]
    mixed = jnp.concatenate([jnp.concatenate(
        [jnp.where(low, _dot(w[2 * c], gvb[sub[s], c * LANES:(c + 1) * LANES]),
                   _dot(w[2 * c + 1], gvb[sub[s], c * LANES:(c + 1) * LANES])) for c in range(4)], axis=1) + bias
        for s in range(nsub)], axis=0)
    sgu = gu_ref[...] * mixed
    o_ref[...] = _merge_out(attn, sgu, ga_ref[...], gg_ref[...], wo_ref[...], x_ref[...])


def _mix_prompt(sinks, q, k, v, gu, gv, x, ws, bias, ga, gg, wo, *, batch, seq, rows):
    nb = seq // rows
    per = rows // WINDOW
    cur = lambda width: pl.BlockSpec((rows, width), lambda b, n: (b * nb + n, 0))
    prev = lambda width: pl.BlockSpec(
        (WINDOW, width), lambda b, n: (b * nb * per + jnp.maximum(n * per - 1, 0), 0))
    return pl.pallas_call(
        _mix_prompt_kernel,
        grid=(batch, nb),
        in_specs=[pl.BlockSpec(memory_space=pltpu.SMEM),
                  cur(512), prev(128), cur(128), prev(128), cur(128), cur(512), cur(512), cur(x.shape[1]),
                  _full(ws.shape), _full(bias.shape), _full(ga.shape), _full(gg.shape), _full(wo.shape)],
        out_specs=cur(x.shape[1]),
        out_shape=jax.ShapeDtypeStruct(x.shape, F32),
        compiler_params=_params("parallel", "parallel"),
        name="mix_prompt",
    )(sinks, q, k, k, v, v, gu, gv, x, ws, bias, ga, gg, wo)


def _mix_sample_kernel(q_ref, kn_ref, vn_ref, gu_ref, gv_ref, x_ref, ck_ref, cv_ref,
                       sink_ref, ctab_ref, bias_ref, ga_ref, gg_ref, wo_ref,
                       o_ref, nk_ref, nv_ref, attn_s, sgu_s):
    nseq = ck_ref.shape[0]
    t = q_ref.shape[0] // nseq
    w = ck_ref.shape[1]
    low, lowf, highf = _half_masks()
    rows = N_Q_HEADS * t
    band = 2 * w
    i = lax.broadcasted_iota(I32, (rows, band), 0) & (t - 1)
    j = lax.broadcasted_iota(I32, (rows, band), 1)
    valid = (j > i) & (j <= i + w)
    sink = sink_ref[:, 0:1]
    pad = jnp.zeros((band - w - t, LANES), F32)

    seq_rows = [slice(b * t, (b + 1) * t) for b in range(nseq)]
    scores = []
    for b, rs in enumerate(seq_rows):
        qs = _stack_heads(q_ref[rs, :], lowf, highf)
        kb = jnp.concatenate([ck_ref[b], kn_ref[rs, :], pad], axis=0).astype(BF16)
        scores.append(jnp.where(valid, _dot_nt(qs, kb), NEG_INF))
    probs = [_sink_softmax(s, sink) for s in scores]
    for b, rs in enumerate(seq_rows):
        vc, vn = cv_ref[b], vn_ref[rs, :]
        o_all = _dot(probs[b], jnp.concatenate([vc, vn, pad], axis=0).astype(BF16))
        for c in range(4):
            attn_s[rs, c * LANES:(c + 1) * LANES] = jnp.where(
                low, o_all[c * t:(c + 1) * t], o_all[(4 + c) * t:(5 + c) * t])
        nk_ref[b] = jnp.concatenate([ck_ref[b, t:, :], kn_ref[rs, :]], axis=0)
        nv_ref[b] = jnp.concatenate([vc[t:], vn], axis=0)
    for b, rs in enumerate(seq_rows):
        gvb = gv_ref[rs, :]
        mixed = ctab_ref[0] * gvb[0:1, :]
        for r in range(1, t):
            mixed = mixed + ctab_ref[r] * gvb[r:r + 1, :]
        sgu_s[rs, :] = gu_ref[rs, :] * (mixed + bias_ref[...])
    o_ref[...] = _merge_out(attn_s[...], sgu_s[...], ga_ref[...], gg_ref[...], wo_ref[...], x_ref[...])


def _mix_sample(q, k, v, gu, gv, x, ck, cv, sink_tab, ctab, bias, ga, gg, wo, *, group):
    nseq, w, _ = ck.shape
    t = x.shape[0] // nseq
    assert t == SUBLANES and w == WINDOW and nseq % group == 0
    rows = group * t
    row = lambda width: pl.BlockSpec((rows, width), lambda i: (i, 0))
    cache = pl.BlockSpec((group, w, LANES), lambda i: (i, 0, 0))
    return pl.pallas_call(
        _mix_sample_kernel,
        grid=(nseq // group,),
        in_specs=[row(512), row(128), row(128), row(512), row(512), row(x.shape[1]), cache, cache,
                  _full(sink_tab.shape), _full(ctab.shape), _full(bias.shape),
                  _full(ga.shape), _full(gg.shape), _full(wo.shape)],
        out_specs=[row(x.shape[1]), cache, cache],
        out_shape=[jax.ShapeDtypeStruct(x.shape, F32), jax.ShapeDtypeStruct(ck.shape, F32),
                   jax.ShapeDtypeStruct(cv.shape, F32)],
        scratch_shapes=[pltpu.VMEM((rows, 512), F32), pltpu.VMEM((rows, 512), F32)],
        compiler_params=_params("parallel"),
        name="mix_sample",
    )(q, k, v, gu, gv, x, ck, cv, sink_tab, ctab, bias, ga, gg, wo)


def _mem_kv_kernel(m_ref, g_ref, w_ref, k_ref, v_ref):
    mn = _rms(m_ref[...], g_ref[...]).astype(BF16)
    kv = _dot(mn, w_ref[...])
    half = kv.shape[1] // 2
    k_ref[...] = kv[:, :half]
    v_ref[...] = kv[:, half:]


def _mem_kv(mem, g, w_kv, *, rows):
    t, d = mem.shape
    width = w_kv.shape[1] // 2
    row = lambda wd: pl.BlockSpec((rows, wd), lambda i: (i, 0))
    return pl.pallas_call(
        _mem_kv_kernel,
        grid=(t // rows,),
        in_specs=[row(d), _full(g.shape), _full(w_kv.shape)],
        out_specs=[row(width), row(width)],
        out_shape=[jax.ShapeDtypeStruct((t, width), F32)] * 2,
        compiler_params=_params("parallel"),
        name="mem_kv",
    )(mem, g, w_kv)


def _softmax(s):
    p = jnp.exp(s - jnp.max(s, axis=-1, keepdims=True))
    return (p / jnp.sum(p, axis=-1, keepdims=True)).astype(BF16)


def _mem_heads(q, mk, mv, low, lowf, highf):
    rows = q.shape[0]
    outs = []
    for c in range(MEM_HEADS // 2):
        sl = slice(c * LANES, (c + 1) * LANES)
        qc = q[:, sl]
        q2 = jnp.concatenate([(qc * lowf).astype(BF16), (qc * highf).astype(BF16)], axis=0)
        o = _dot(_softmax(_dot_nt(q2, mk[:, sl].astype(BF16))), mv[:, sl].astype(BF16))
        outs.append(jnp.where(low, o[:rows], o[rows:]))
    return jnp.concatenate(outs, axis=1)


def _mem_attn_prompt_kernel(x_ref, g_ref, wq_ref, mk_ref, mv_ref, wo_ref, o_ref):
    low, lowf, highf = _half_masks()
    x = x_ref[...]
    q = _dot(_rms(x, g_ref[...]).astype(BF16), wq_ref[...]) * (MEM_HEAD_DIM ** -0.5)
    o = _mem_heads(q, mk_ref[...], mv_ref[...], low, lowf, highf)
    o_ref[...] = x + _dot(o.astype(BF16), wo_ref[...])


def _mem_attn_prompt(x, g, wq, mk, mv, wo, *, batch, seq, rows):
    n_mem = mk.shape[0] // batch
    nb = seq // rows
    row = pl.BlockSpec((rows, x.shape[1]), lambda b, n: (b * nb + n, 0))
    mem = pl.BlockSpec((n_mem, mk.shape[1]), lambda b, n: (b, 0))
    return pl.pallas_call(
        _mem_attn_prompt_kernel,
        grid=(batch, nb),
        in_specs=[row, _full(g.shape), _full(wq.shape), mem, mem, _full(wo.shape)],
        out_specs=row,
        out_shape=jax.ShapeDtypeStruct(x.shape, F32),
        compiler_params=_params("parallel", "parallel"),
        name="mem_attn_prompt",
    )(x, g, wq, mk, mv, wo)


def _mem_attn_sample_kernel(x_ref, g_ref, wq_ref, mk_ref, mv_ref, wo_ref, o_ref, q_s, a_s):
    nseq = mk_ref.shape[0]
    t = x_ref.shape[0] // nseq
    low, lowf, highf = _half_masks()
    x = x_ref[...]
    q_s[...] = _dot(_rms(x, g_ref[...]).astype(BF16), wq_ref[...]) * (MEM_HEAD_DIM ** -0.5)
    for c in range(MEM_HEADS // 2):
        sl = slice(c * LANES, (c + 1) * LANES)
        scores = []
        for b in range(nseq):
            qc = q_s[b * t:(b + 1) * t, sl]
            q2 = jnp.concatenate([(qc * lowf).astype(BF16), (qc * highf).astype(BF16)], axis=0)
            scores.append(_dot_nt(q2, mk_ref[b, :, sl].astype(BF16)))
        p = _softmax(jnp.concatenate(scores, axis=0))
        for b in range(nseq):
            o = _dot(p[2 * t * b:2 * t * (b + 1)], mv_ref[b, :, sl].astype(BF16))
            a_s[b * t:(b + 1) * t, sl] = jnp.where(low, o[:t], o[t:])
    o_ref[...] = x + _dot(a_s[...].astype(BF16), wo_ref[...])


def _mem_attn_sample(x, g, wq, mk, mv, wo, *, group):
    nseq, n_mem, width = mk.shape
    t = x.shape[0] // nseq
    rows = group * t
    row = pl.BlockSpec((rows, x.shape[1]), lambda i: (i, 0))
    mem = pl.BlockSpec((group, n_mem, width), lambda i: (i, 0, 0))
    return pl.pallas_call(
        _mem_attn_sample_kernel,
        grid=(nseq // group,),
        in_specs=[row, _full(g.shape), _full(wq.shape), mem, mem, _full(wo.shape)],
        out_specs=row,
        out_shape=jax.ShapeDtypeStruct(x.shape, F32),
        scratch_shapes=[pltpu.VMEM((rows, width), F32), pltpu.VMEM((rows, width), F32)],
        compiler_params=_params("parallel"),
        name="mem_attn_sample",
    )(x, g, wq, mk, mv, wo)


def _top16(s):
    nrows, cols = s.shape
    r = lax.broadcasted_iota(I32, (nrows, cols), 0).astype(F32)
    slot = lax.broadcasted_iota(I32, (PEER_TOPK, cols), 0)
    vals = jnp.zeros((PEER_TOPK, cols), F32)
    idxs = jnp.zeros((PEER_TOPK, cols), F32)
    for p in range(PEER_TOPK):
        m = jnp.max(s, axis=0, keepdims=True)
        am = jnp.min(jnp.where(s == m, r, float(nrows)), axis=0, keepdims=True)
        vals = jnp.where(slot == p, m, vals)
        idxs = jnp.where(slot == p, am, idxs)
        s = jnp.where(r == am, -jnp.inf, s)
    return vals, idxs


def _sort16_pairs():
    def merge(lo, hi, r):
        step = r * 2
        if step < hi - lo:
            yield from merge(lo, hi, step)
            yield from merge(lo + r, hi, step)
            yield from [(i, i + r) for i in range(lo + r, hi - r, step)]
        else:
            yield (lo, lo + r)

    def sort(lo, hi):
        if hi > lo:
            mid = lo + (hi - lo) // 2
            yield from sort(lo, mid)
            yield from sort(mid + 1, hi)
            yield from merge(lo, hi, 1)

    return tuple(sort(0, PEER_TOPK - 1))


SORT16 = _sort16_pairs()


def _cmpx(v, pays, i, j):
    swap = v[j] > v[i]
    hi, lo = jnp.maximum(v[i], v[j]), jnp.minimum(v[i], v[j])
    for p in pays:
        p[i], p[j] = jnp.where(swap, p[j], p[i]), jnp.where(swap, p[i], p[j])
    v[i], v[j] = hi, lo


def _merge16(va, pa, vb, pb):
    n = PEER_TOPK
    v, pays = [], [[] for _ in pa]
    for i in range(n):
        other = vb[n - 1 - i]
        if other is None:
            v.append(va[i])
            for k in range(len(pa)):
                pays[k].append(pa[k][i])
            continue
        take = other > va[i]
        v.append(jnp.maximum(va[i], other))
        for k in range(len(pa)):
            pays[k].append(jnp.where(take, pb[k][n - 1 - i], pa[k][i]))
    d = n // 2
    while d:
        for i in range(n):
            if not i & d:
                _cmpx(v, pays, i, i + d)
        d //= 2
    return v, pays


def _xor_rows(x, d, sub):
    if d == SUBLANES // 2:
        return pltpu.roll(x, d, 0)
    return jnp.where((sub & d) == 0, pltpu.roll(x, SUBLANES - d, 0), pltpu.roll(x, d, 0))


def _row(x, k):
    return jnp.broadcast_to(x[k:k + 1, :], x.shape)


def _route_sorted(s_s, rows, i1_s, i2_s, gt_s):
    n = PEER_TOPK
    sub = lax.broadcasted_iota(I32, (SUBLANES, LANES), 0)
    subf = sub.astype(F32)
    bit2 = (sub & 4) == 0
    bit1 = (sub & 2) == 0
    halves = [slice(0, LANES), slice(LANES, 2 * LANES)]
    bad = jnp.zeros((SUBLANES, LANES), F32)

    def packed_merge(mask, d, xa, xb):
        va, pa = xa
        vb, pb = xb
        lv = [jnp.where(mask, a, b) for a, b in zip(va, vb)]
        rv = [_xor_rows(jnp.where(mask, b, a), d, sub) for a, b in zip(va, vb)]
        lp = [[jnp.where(mask, a, b) for a, b in zip(qa, qb)] for qa, qb in zip(pa, pb)]
        rp = [[_xor_rows(jnp.where(mask, b, a), d, sub) for a, b in zip(qa, qb)] for qa, qb in zip(pa, pb)]
        return _merge16(lv, lp, rv, rp)

    def self_merge(d, x):
        v, p = x
        return _merge16(v, p, [_xor_rows(a, d, sub) for a in v], [[_xor_rows(a, d, sub) for a in q] for q in p])

    def decreasing(v):
        ok = v[0] > v[1]
        flag = jnp.where(ok, 0.0, 1.0)
        for r in range(1, n - 1):
            flag = jnp.where(v[r] > v[r + 1], flag, 1.0)
        return flag

    sorted_lists = {}
    for c in range(2):
        for part in range(2):
            v = [s_s[c, SUBLANES * g:SUBLANES * (g + 1), halves[part]] for g in range(N_KEYS // SUBLANES)]
            ix = [subf + float(SUBLANES * g) for g in range(N_KEYS // SUBLANES)]
            pays = [ix]
            for i, j in SORT16:
                _cmpx(v, pays, i, j)
            sorted_lists[c, part] = (v, pays)
    x1 = [packed_merge(bit2, 4, sorted_lists[c, 0], sorted_lists[c, 1]) for c in range(2)]
    x2 = packed_merge(bit1, 2, x1[0], x1[1])
    tv, (ti,) = self_merge(1, x2)
    bad = jnp.maximum(bad, decreasing(tv))
    for c in range(2):
        for part in range(2):
            thr = _row(tv[n - 1], 4 * part + 2 * c)
            cnt = jnp.zeros((SUBLANES, LANES), F32)
            for g in range(N_KEYS // SUBLANES):
                cnt = cnt + jnp.where(s_s[c, SUBLANES * g:SUBLANES * (g + 1), halves[part]] >= thr, 1.0, 0.0)
            total = jnp.sum(cnt, axis=0, keepdims=True)
            bad = jnp.maximum(bad, jnp.broadcast_to(jnp.where(total == float(n), 0.0, 1.0), bad.shape))

    half = n // 2
    lists, tails = [], []
    for part in range(2):
        base = 4 * part
        s1 = [_row(tv[r], base) for r in range(n)]
        a1 = [_row(ti[r], base) for r in range(n)]
        s2 = [_row(tv[r], base + 2) for r in range(n)]
        a2 = [_row(ti[r], base + 2) for r in range(n)]
        p1, ip1 = s1[0], a1[0]
        for p in range(1, half):
            p1 = jnp.where(sub == p, s1[p], p1)
            ip1 = jnp.where(sub == p, a1[p], ip1)
        lists.append(([p1 + s2[q] for q in range(n)], [[ip1] * n, list(a2)]))
        tails.append(([s1[half + i] + s2[0] for i in range(half)],
                      [[a1[half + i] for i in range(half)], [a2[0]] * half]))
    y = packed_merge(bit2, 4, lists[0], lists[1])
    y = self_merge(2, y)
    y = self_merge(1, y)
    dv = [jnp.where(bit2, a, b) for a, b in zip(tails[0][0], tails[1][0])] + [None] * half
    dp = [[jnp.where(bit2, a, b) for a, b in zip(qa, qb)] + [None] * half
          for qa, qb in zip(tails[0][1], tails[1][1])]
    top, (sel1, sel2) = _merge16(y[0], y[1], dv, dp)
    bad = jnp.maximum(bad, decreasing(top))
    for part in range(2):
        thr = _row(top[n - 1], 4 * part)
        cnt = jnp.zeros((SUBLANES, LANES), F32)
        for q in range(n):
            cnt = cnt + jnp.where(lists[part][0][q] >= thr, 1.0, 0.0)
        total = jnp.sum(cnt, axis=0, keepdims=True)
        for i in range(half):
            total = total + jnp.where(tails[part][0][i][0:1, :] >= thr[0:1, :], 1.0, 0.0)
        bad = jnp.maximum(bad, jnp.broadcast_to(jnp.where(total == float(n), 0.0, 1.0), bad.shape))

    e = [jnp.exp(t - top[0]) for t in top]
    z = e[0]
    for k in range(1, n):
        z = z + e[k]
    gate = [ek / z for ek in e]
    for src, dst in ((sel1, i1_s), (sel2, i2_s), (gate, gt_s)):
        for grp in range(2):
            for part in range(2):
                tile = None
                for k in range(SUBLANES):
                    slab = src[grp * SUBLANES + k]
                    if (k < 4) != (part == 0):
                        slab = pltpu.roll(slab, 4, 0)
                    tile = slab if tile is None else jnp.where(sub == k, slab, tile)
                dst[rows[grp], halves[part]] = tile
    return jnp.max(bad)


def _peer_route_kernel(x_ref, g_ref, wh_ref, wl_ref, kh_ref, kl_ref,
                       xn_ref, i1_ref, i2_ref, gate_ref, q_s, s_s, i1_s, i2_s, gt_s):
    tb = x_ref.shape[0]
    xn = _rms(x_ref[...], g_ref[...])
    xh, xl = _split(xn)
    xn_ref[...] = xh
    wh = wh_ref[...]
    q = _dot(xh, wh) + _dot(xh, wl_ref[...]) + _dot(xl, wh)
    for hc in range(2 * PEER_HEADS):
        q_s[hc] = q[:, hc * D_HALF:(hc + 1) * D_HALF]

    group = s_s.shape[0]

    def head_pair(k, carry):
        tied = []
        for hh in range(group):
            h = group * k + hh
            for c in range(2):
                qh, ql = _split(q_s[2 * h + c])
                kh = kh_ref[2 * h + c]
                s_s[hh, c] = _dot_nt(kh, qh) + _dot_nt(kh, ql) + _dot_nt(kl_ref[2 * h + c], qh)
            rows = [pl.ds(pl.multiple_of(h * PEER_TOPK + r, SUBLANES), SUBLANES) for r in (0, SUBLANES)]
            tied.append(_route_sorted(s_s.at[hh], rows, i1_s, i2_s, gt_s))

        for hh in range(group):
            @pl.when(tied[hh] > 0.0)
            def _():
                for part in range(tb // LANES):
                    head_part(group * k + hh, hh, part)

        return carry

    def head_part(h, hh, part):
        cols = slice(part * LANES, (part + 1) * LANES)
        (s1, i1), (s2, i2) = [_top16(s_s[hh, c, :, cols]) for c in range(2)]
        tb = LANES
        half = PEER_TOPK // 2
        cand = jnp.concatenate(
            [jnp.broadcast_to(s1[0:1], (PEER_TOPK, tb)) + s2]
            + [jnp.broadcast_to(s1[p:p + 1], (half, tb)) + s2[:half] for p in range(1, half)]
            + [s1[half:] + jnp.broadcast_to(s2[0:1], (half, tb))], axis=0)
        top, cidx = _top16(cand)
        ci = cidx.astype(I32)
        mid = ci - PEER_TOPK
        tail = PEER_TOPK + half * (half - 1)
        pi = jnp.where(ci < PEER_TOPK, 0,
                       jnp.where(ci < tail, 1 + (mid >> (half.bit_length() - 1)), ci - tail + half))
        qi = jnp.where(ci < PEER_TOPK, ci, jnp.where(ci < tail, mid & (half - 1), 0))
        e1 = jnp.zeros((PEER_TOPK, tb), F32)
        e2 = jnp.zeros((PEER_TOPK, tb), F32)
        for p in range(PEER_TOPK):
            e1 = jnp.where(pi == p, jnp.broadcast_to(i1[p:p + 1], (PEER_TOPK, tb)), e1)
            e2 = jnp.where(qi == p, jnp.broadcast_to(i2[p:p + 1], (PEER_TOPK, tb)), e2)
        e = jnp.exp(top - jnp.max(top, axis=0, keepdims=True))
        rs = pl.ds(pl.multiple_of(h * PEER_TOPK, PEER_TOPK), PEER_TOPK)
        i1_s[rs, cols] = e1
        i2_s[rs, cols] = e2
        gt_s[rs, cols] = e / jnp.sum(e, axis=0, keepdims=True)

    lax.fori_loop(0, PEER_HEADS // group, head_pair, 0)
    i1_ref[...] = i1_s[...].T.astype(I32)
    i2_ref[...] = i2_s[...].T.astype(I32)
    gate_ref[...] = gt_s[...].T


def _peer_route(x, g, wh, wl, kh, kl):
    t, d = x.shape
    tb = 2 * LANES
    assert t % tb == 0
    sel = PEER_HEADS * PEER_TOPK
    row = lambda width: pl.BlockSpec((tb, width), lambda i: (i, 0))
    return pl.pallas_call(
        _peer_route_kernel,
        grid=(t // tb,),
        in_specs=[row(d), _full(g.shape), _full(wh.shape), _full(wl.shape), _full(kh.shape), _full(kl.shape)],
        out_specs=[row(d), row(sel), row(sel), row(sel)],
        out_shape=[jax.ShapeDtypeStruct((t, d), BF16), jax.ShapeDtypeStruct((t, sel), I32),
                   jax.ShapeDtypeStruct((t, sel), I32), jax.ShapeDtypeStruct((t, sel), F32)],
        scratch_shapes=[pltpu.VMEM((2 * PEER_HEADS, tb, D_HALF), F32), pltpu.VMEM((ROUTE_HEADS_PER_BODY, 2, N_KEYS, tb), F32),
                        pltpu.VMEM((sel, tb), F32), pltpu.VMEM((sel, tb), F32), pltpu.VMEM((sel, tb), F32)],
        compiler_params=_params("parallel"),
        name="peer_route",
    )(x, g, wh, wl, kh, kl)


def _peer_gates_kernel(i1_ref, i2_ref, gate_ref, o_ref, g_s):
    tb = i1_ref.shape[0]
    sel = i1_ref.shape[1]
    key = lax.broadcasted_iota(I32, (N_KEYS, sel), 0)
    zero = jnp.zeros((N_KEYS, sel), BF16)

    tile = 2 * SUBLANES

    def build(tidx):
        w1, w2 = [], []
        for half in range(2):
            t0 = pl.multiple_of(tidx * tile + half * SUBLANES, SUBLANES)
            i1g = i1_ref[pl.ds(t0, SUBLANES), :]
            i2g = i2_ref[pl.ds(t0, SUBLANES), :]
            gtg = gate_ref[pl.ds(t0, SUBLANES), :]
            w1 += [jnp.where(key == i1g[s:s + 1, :], gtg[s:s + 1, :], 0.0).astype(BF16) for s in range(SUBLANES)]
            w2 += [jnp.where(key == i2g[s:s + 1, :], 1.0, 0.0).astype(BF16) for s in range(SUBLANES)]
        for s in range(0, tile, 2):
            lhs = jnp.concatenate([w1[s], w1[s + 1]], axis=1)
            rhs = jnp.concatenate([jnp.concatenate([w2[s], zero], axis=1),
                                   jnp.concatenate([zero, w2[s + 1]], axis=1)], axis=0)
            g = _dot_nt(lhs, rhs)
            word = pltpu.pack_elementwise([g[:, :N_KEYS], g[:, N_KEYS:]], packed_dtype=BF16)
            pair = tidx * SUBLANES + s // 2
            g_s[pl.ds(pl.multiple_of(pair * GATE_PITCH, SUBLANES), N_KEYS), :] = word

    def relayout(tidx):
        rows = pl.ds(pl.multiple_of(tidx * tile, tile), tile)
        for a in range(N_KEYS):
            words = g_s[pl.ds(tidx * SUBLANES * GATE_PITCH + a, SUBLANES, stride=GATE_PITCH), :]
            o_ref[rows, a * N_KEYS:(a + 1) * N_KEYS] = pltpu.bitcast(words, BF16)

    group = GATE_TILES_PER_STEP
    n_groups = tb // (tile * group)
    for u in range(group):
        build(u)

    def step(gidx, carry):
        for u in range(group):
            relayout((gidx - 1) * group + u)
        for u in range(group):
            build(gidx * group + u)
        return carry

    lax.fori_loop(1, n_groups, step, 0)
    for u in range(group):
        relayout((n_groups - 1) * group + u)


def _peer_gates(i1, i2, gate, *, rows):
    t, sel = i1.shape
    row = pl.BlockSpec((rows, sel), lambda i: (i, 0))
    return pl.pallas_call(
        _peer_gates_kernel,
        grid=(t // rows,),
        in_specs=[row, row, row],
        out_specs=pl.BlockSpec((rows, N_KEYS * N_KEYS), lambda i: (i, 0)),
        out_shape=jax.ShapeDtypeStruct((t, N_KEYS * N_KEYS), BF16),
        scratch_shapes=[pltpu.VMEM((rows // 2 * GATE_PITCH, N_KEYS), jnp.uint32)],
        compiler_params=_params("parallel"),
        name="peer_gates",
    )(i1, i2, gate)


def _peer_dense_kernel(xn_ref, gates_ref, u_ref, v_ref, x_ref, g_ref, o_ref, acc):
    k = pl.program_id(1)

    @pl.when(k == 0)
    def _():
        acc[...] = jnp.zeros_like(acc)

    h = _dot_nt(xn_ref[...], u_ref[...])
    a = (_gelu(h) * gates_ref[...].astype(F32)).astype(BF16)
    acc[...] += _dot(a, v_ref[...])

    @pl.when(k == pl.num_programs(1) - 1)
    def _():
        o_ref[...] = _rms(x_ref[...] + acc[...], g_ref[...])


def _peer_dense(xn, gates, u, v, x, g, *, rows, experts):
    t, d = x.shape
    n_exp = u.shape[0]
    tok = pl.BlockSpec((rows, d), lambda i, k: (i, 0))
    tab = pl.BlockSpec((experts, d), lambda i, k: (k, 0))
    return pl.pallas_call(
        _peer_dense_kernel,
        grid=(t // rows, n_exp // experts),
        in_specs=[tok, pl.BlockSpec((rows, experts), lambda i, k: (i, k)), tab, tab, tok, _full(g.shape)],
        out_specs=tok,
        out_shape=jax.ShapeDtypeStruct((t, d), F32),
        scratch_shapes=[pltpu.VMEM((rows, d), F32)],
        compiler_params=_params("parallel", "arbitrary"),
        name="peer_dense",
    )(xn, gates, u, v, x, g)


def _peer(x, g_ffn, wh, wl, kh, kl, u, v, g_final):
    t = x.shape[0]
    xn, i1, i2, gate = _peer_route(x, g_ffn, wh, wl, kh, kl)
    gates = _peer_gates(i1, i2, gate, rows=2 * LANES)
    return _peer_dense(xn, gates, u, v, x, g_final, rows=min(512, t), experts=2048)


def _head_perm():
    new = np.zeros(ATTN_WIDTH, np.int32)
    for h in range(N_Q_HEADS):
        dst = (h % 4) * LANES + (h // 4) * HEAD_DIM
        new[dst:dst + HEAD_DIM] = np.arange(h * HEAD_DIM, (h + 1) * HEAD_DIM)
    return new


def _rope_tables(pos):
    half = HEAD_DIM // 2
    inv = ROPE_THETA ** (-jnp.arange(half, dtype=F32) / half)
    ang = pos.astype(F32)[:, None] * inv[None, :]
    cos, sin = jnp.cos(ang), jnp.sin(ang)
    reps = LANES // HEAD_DIM
    return (jnp.tile(jnp.concatenate([cos, cos], axis=1), (1, reps)),
            jnp.tile(jnp.concatenate([-sin, sin], axis=1), (1, reps)))


def kernel(x_prompt, x_sample, mem_prompt, cache_swa_k, cache_swa_v, cache_mem_k, cache_mem_v, g_mix, w_in, attn_sinks, g_sgu, w_spatial, b_spatial, g_attn_out, g_gmlp_out, w_out, g_cross, g_mem, w_cq, w_mk, w_mv, w_co, g_ffn, w_peer_q, peer_sub_keys, peer_u, peer_v, g_final):
    batch, seq, d = x_prompt.shape
    dec_batch, dec_seq, _ = x_sample.shape
    depth = g_mix.shape[0]
    n_mem = mem_prompt.shape[1]
    assert seq % WINDOW == 0 and dec_seq == SUBLANES and cache_swa_k.shape[2] == WINDOW

    perm = _head_perm()
    cos_p, sin_p = _rope_tables(jnp.arange(seq, dtype=I32))
    cos_s, sin_s = _rope_tables(PAST_LEN +(jnp.arange(dec_batch * dec_seq, dtype=I32) % dec_seq))
    seg = jnp.asarray(np.kron(np.eye(N_GMLP_HEADS), np.full((HEAD_DIM, HEAD_DIM), 1.0 / HEAD_DIM)), BF16)
    row2 = lambda a: a.reshape(1, -1)

    xp = x_prompt.reshape(batch * seq, d)
    xs = x_sample.reshape(dec_batch * dec_seq, d)
    outs = {name: [] for name in ("kp", "vp", "mk", "mv", "ks", "vs", "gvs")}
    for l in range(depth):
        w_in_l = jnp.concatenate([w_in[l][:, :ATTN_WIDTH][:, perm], w_in[l][:, ATTN_WIDTH:]], axis=1).astype(BF16)
        w_out_l = jnp.concatenate([w_out[l][:ATTN_WIDTH][perm], w_out[l][ATTN_WIDTH:]], axis=0).astype(BF16)
        ga = row2(g_attn_out[l][perm])
        gg = row2(g_gmlp_out[l])
        gs = row2(g_sgu[l])
        bias = jnp.repeat(b_spatial[l].T, HEAD_DIM, axis=1)
        wt = jnp.tril(w_spatial[l][:, :dec_seq, :dec_seq])
        ctab = jnp.repeat(jnp.transpose(wt, (2, 1, 0)), HEAD_DIM, axis=2)
        sink_tab = jnp.broadcast_to(jnp.repeat(attn_sinks[l], dec_seq)[:, None], (N_Q_HEADS * dec_seq, LANES))

        mix = functools.partial(_mix_in, g=row2(g_mix[l]), w_in=w_in_l, gs=gs, seg=seg)
        rows_p = next(r for r in (1024, 512, WINDOW) if seq % r == 0)
        qp, kp, vp, gup, gvp = mix(xp, cosf=cos_p, sinf=sin_p, rows=rows_p, table_blocks=seq // rows_p)
        rows_s = min(512, dec_batch * dec_seq)
        qs, ks, vs, gus, gvs = mix(xs, cosf=cos_s, sinf=sin_s, rows=rows_s,
                                   table_blocks=dec_batch * dec_seq // rows_s)

        xp = _mix_prompt(attn_sinks[l], qp, kp, vp, gup, gvp, xp, w_spatial[l], bias, ga, gg, w_out_l,
                         batch=batch, seq=seq, rows=next(r for r in (512, 256, WINDOW) if seq % r == 0))
        ck = cache_swa_k[l].reshape(dec_batch, WINDOW, N_KV_HEADS * HEAD_DIM)
        cv = cache_swa_v[l].reshape(dec_batch, WINDOW, N_KV_HEADS * HEAD_DIM)
        xs, nks, nvs = _mix_sample(qs, ks, vs, gus, gvs, xs, ck, cv, sink_tab, ctab, bias[:dec_seq], ga, gg,
                                   w_out_l, group=min(16, dec_batch))

        w_kv = jnp.concatenate([w_mk[l], w_mv[l]], axis=1).astype(BF16)
        mk, mv = _mem_kv(mem_prompt.reshape(batch * n_mem, d), row2(g_mem[l]), w_kv, rows=n_mem)
        wq = w_cq[l].astype(BF16)
        wo = w_co[l].astype(BF16)
        gc = row2(g_cross[l])
        xp = _mem_attn_prompt(xp, gc, wq, mk, mv, wo, batch=batch, seq=seq, rows=512 if seq % 512 == 0 else 256)
        cmk = cache_mem_k[l].reshape(dec_batch, n_mem, MEM_HEADS * MEM_HEAD_DIM)
        cmv = cache_mem_v[l].reshape(dec_batch, n_mem, MEM_HEADS * MEM_HEAD_DIM)
        xs = _mem_attn_sample(xs, gc, wq, cmk, cmv, wo, group=min(8, dec_batch))

        wh, wl = _split(w_peer_q[l])
        kh, kl = _split(peer_sub_keys[l].reshape(2 * PEER_HEADS, N_KEYS, D_HALF))
        u = peer_u[l].astype(BF16)
        v = peer_v[l].astype(BF16)
        last = l == depth - 1
        assert last, "stacked layers need an un-normalised PEER output"
        peer = functools.partial(_peer, g_ffn=row2(g_ffn[l]), wh=wh, wl=wl, kh=kh, kl=kl, u=u, v=v,
                                 g_final=row2(g_final))
        xp = peer(xp)
        xs = peer(xs)

        outs["kp"].append(kp.reshape(batch, seq, N_KV_HEADS, HEAD_DIM)[:, seq - WINDOW:])
        outs["vp"].append(vp.reshape(batch, seq, N_KV_HEADS, HEAD_DIM)[:, seq - WINDOW:])
        outs["mk"].append(mk.reshape(batch, n_mem, MEM_HEADS, MEM_HEAD_DIM))
        outs["mv"].append(mv.reshape(batch, n_mem, MEM_HEADS, MEM_HEAD_DIM))
        outs["ks"].append(nks.reshape(dec_batch, WINDOW, N_KV_HEADS, HEAD_DIM))
        outs["vs"].append(nvs.reshape(dec_batch, WINDOW, N_KV_HEADS, HEAD_DIM))
        outs["gvs"].append(gvs.reshape(dec_batch, dec_seq, N_GMLP_HEADS, HEAD_DIM))

    stack = lambda name: jnp.stack(outs[name])
    return (xp.reshape(batch, seq, d), xs.reshape(dec_batch, dec_seq, d),
            stack("kp"), stack("vp"), stack("mk"), stack("mv"), stack("ks"), stack("vs"), stack("gvs"))
```

```python
import functools

import jax
import jax.numpy as jnp
import numpy as np
from jax import lax
from jax.experimental import pallas as pl
from jax.experimental.pallas import tpu as pltpu

F32 = jnp.float32
BF16 = jnp.bfloat16
I32 = jnp.int32

LANES = 128
SUBLANES = 8
VMEM_LIMIT_BYTES = 56 * 1024 * 1024

HEAD_DIM = 64
N_Q_HEADS = 8
N_KV_HEADS = 2
ATTN_WIDTH = N_Q_HEADS * HEAD_DIM
WINDOW = 128
PAST_LEN = 8192
ROPE_THETA = 10000.0
N_GMLP_HEADS = 8
GMLP_WIDTH = N_GMLP_HEADS * HEAD_DIM
CHUNK = 128
MEM_HEADS = 4
MEM_HEAD_DIM = 64
PEER_HEADS = 8
N_KEYS = 128
PEER_TOPK = 16
D_HALF = 128
EPS = 1e-6
NEG_INF = -1e30
SQRT_HALF = float(np.sqrt(0.5))

GATE_PITCH = N_KEYS + SUBLANES
ROUTE_HEADS_PER_BODY = 4
GATE_TILES_PER_STEP = 4

NT_DIMS = (((1,), (1,)), ((), ()))


def _params(*semantics):
    return pltpu.CompilerParams(dimension_semantics=semantics, vmem_limit_bytes=VMEM_LIMIT_BYTES)


def _full(shape):
    zeros = (0,) * len(shape)
    return pl.BlockSpec(shape, lambda *_: zeros)


def _rms(x, g):
    return x * lax.rsqrt(jnp.mean(x * x, axis=-1, keepdims=True) + EPS) * g


def _gelu(x):
    return 0.5 * x * (1.0 + lax.erf(x * SQRT_HALF))


def _dot(a, b):
    return jnp.dot(a, b, preferred_element_type=F32)


def _dot_nt(a, b):
    return lax.dot_general(a, b, NT_DIMS, preferred_element_type=F32)


def _split(x):
    hi = x.astype(BF16)
    return hi, (x - hi.astype(F32)).astype(BF16)


def _half_masks():
    lane = lax.broadcasted_iota(I32, (1, LANES), 1)
    low = lane < HEAD_DIM
    return low, low.astype(F32), 1.0 - low.astype(F32)


def _mix_in_kernel(x_ref, g_ref, w_ref, cos_ref, sin_ref, gs_ref, seg_ref,
                   q_ref, k_ref, v_ref, gu_ref, gv_ref):
    xn = _rms(x_ref[...], g_ref[...]).astype(BF16)
    proj = _dot(xn, w_ref[...])
    cosf = cos_ref[...]
    sinf = sin_ref[...]
    lane = lax.broadcasted_iota(I32, (1, LANES), 1)
    first_half = (lane & (HEAD_DIM - 1)) < HEAD_DIM // 2

    def rope(c):
        partner = jnp.where(first_half, pltpu.roll(c, LANES - HEAD_DIM // 2, 1),
                            pltpu.roll(c, HEAD_DIM // 2, 1))
        return c * cosf + partner * sinf

    for c in range(ATTN_WIDTH // LANES):
        sl = slice(c * LANES, (c + 1) * LANES)
        q_ref[:, sl] = rope(proj[:, sl]) * (HEAD_DIM ** -0.5)
    k_ref[...] = rope(proj[:, 512:640])
    v_ref[...] = proj[:, 640:768]
    gu_ref[...] = _gelu(proj[:, 768:1280])
    gv = _gelu(proj[:, 1280:1792])
    hi, lo = _split(gv * gv)
    seg = seg_ref[...]
    ms = _dot(hi, seg) + _dot(lo, seg)
    gv_ref[...] = gv * lax.rsqrt(ms + EPS) * gs_ref[...]


def _mix_in(x, g, w_in, cosf, sinf, gs, seg, *, rows, table_blocks):
    t = x.shape[0]
    d = x.shape[1]
    row = lambda width: pl.BlockSpec((rows, width), lambda i: (i, 0))
    tab = pl.BlockSpec((rows, LANES), lambda i: (i % table_blocks, 0))
    return pl.pallas_call(
        _mix_in_kernel,
        grid=(t // rows,),
        in_specs=[row(d), _full(g.shape), _full(w_in.shape), tab, tab, _full(gs.shape), _full(seg.shape)],
        out_specs=[row(512), row(128), row(128), row(512), row(512)],
        out_shape=[jax.ShapeDtypeStruct((t, 512), F32), jax.ShapeDtypeStruct((t, 128), F32),
                   jax.ShapeDtypeStruct((t, 128), F32), jax.ShapeDtypeStruct((t, 512), F32),
                   jax.ShapeDtypeStruct((t, 512), F32)],
        compiler_params=_params("parallel"),
        name="mix_in",
    )(x, g, w_in, cosf, sinf, gs, seg)


def _stack_heads(q, lowf, highf):
    parts = [(q[:, c * LANES:(c + 1) * LANES] * m).astype(BF16)
             for m in (lowf, highf) for c in range(4)]
    return jnp.concatenate(parts, axis=0)


def _sink_softmax(s, sink):
    m = jnp.maximum(jnp.max(s, axis=-1, keepdims=True), sink)
    p = jnp.exp(s - m)
    den = jnp.sum(p, axis=-1, keepdims=True) + jnp.exp(sink - m)
    return (p / den).astype(BF16)


def _merge_out(attn, sgu, ga, gg, wo, x):
    merged = jnp.concatenate([_rms(attn, ga), _rms(sgu, gg)], axis=1).astype(BF16)
    return x + _dot(merged, wo)


def _mix_prompt_kernel(sinks_ref, q_ref, kp_ref, kc_ref, vp_ref, vc_ref, gu_ref, gv_ref, x_ref,
                       ws_ref, bias_ref, ga_ref, gg_ref, wo_ref, o_ref):
    n = pl.program_id(1)
    low, lowf, highf = _half_masks()
    blk = WINDOW
    nsub = q_ref.shape[0] // blk
    sub = [slice(s * blk, (s + 1) * blk) for s in range(nsub)]
    band = [slice(s * blk, (s + 2) * blk) for s in range(nsub)]
    kall = jnp.concatenate([kp_ref[...], kc_ref[...]], axis=0).astype(BF16)
    vall = jnp.concatenate([vp_ref[...], vc_ref[...]], axis=0).astype(BF16)
    q = q_ref[...]
    i = lax.broadcasted_iota(I32, (blk, 2 * blk), 0)
    j = lax.broadcasted_iota(I32, (blk, 2 * blk), 1)
    in_window = (j > i) & (j <= i + blk)
    first_key = jnp.where(n > 0, 0, blk)
    valid = [in_window & (j >= first_key)] + [in_window] * (nsub - 1)
    scores = [_dot_nt(_stack_heads(q[sub[s]], lowf, highf), kall[band[s]]) for s in range(nsub)]
    probs = [jnp.concatenate(
        [_sink_softmax(jnp.where(valid[s], scores[s][h * blk:(h + 1) * blk], NEG_INF), sinks_ref[h])
         for h in range(N_Q_HEADS)], axis=0) for s in range(nsub)]
    outs = [_dot(probs[s], vall[band[s]]) for s in range(nsub)]
    attn = jnp.concatenate([jnp.concatenate(
        [jnp.where(low, o[c * blk:(c + 1) * blk], o[(4 + c) * blk:(5 + c) * blk]) for c in range(4)], axis=1)
        for o in outs], axis=0)

    gvb = gv_ref[...].astype(BF16)
    r = lax.broadcasted_iota(I32, (CHUNK, CHUNK), 0)
    c_ = lax.broadcasted_iota(I32, (CHUNK, CHUNK), 1)
    tril = r >= c_
    w = [jnp.where(tril, ws_ref[h], 0.0).astype(BF16) for h in range(N_GMLP_HEADS)]
    bias = bias_ref[...]
    mixed = jnp.concatenate([jnp.concatenate(
        [jnp.where(low, _dot(w[2 * c], gvb[sub[s], c * LANES:(c + 1) * LANES]),
                   _dot(w[2 * c + 1], gvb[sub[s], c * LANES:(c + 1) * LANES])) for c in range(4)], axis=1) + bias
        for s in range(nsub)], axis=0)
    sgu = gu_ref[...] * mixed
    o_ref[...] = _merge_out(attn, sgu, ga_ref[...], gg_ref[...], wo_ref[...], x_ref[...])


def _mix_prompt(sinks, q, k, v, gu, gv, x, ws, bias, ga, gg, wo, *, batch, seq, rows):
    nb = seq // rows
    per = rows // WINDOW
    cur = lambda width: pl.BlockSpec((rows, width), lambda b, n: (b * nb + n, 0))
    prev = lambda width: pl.BlockSpec(
        (WINDOW, width), lambda b, n: (b * nb * per + jnp.maximum(n * per - 1, 0), 0))
    return pl.pallas_call(
        _mix_prompt_kernel,
        grid=(batch, nb),
        in_specs=[pl.BlockSpec(memory_space=pltpu.SMEM),
                  cur(512), prev(128), cur(128), prev(128), cur(128), cur(512), cur(512), cur(x.shape[1]),
                  _full(ws.shape), _full(bias.shape), _full(ga.shape), _full(gg.shape), _full(wo.shape)],
        out_specs=cur(x.shape[1]),
        out_shape=jax.ShapeDtypeStruct(x.shape, F32),
        compiler_params=_params("parallel", "parallel"),
        name="mix_prompt",
    )(sinks, q, k, k, v, v, gu, gv, x, ws, bias, ga, gg, wo)


def _mix_sample_kernel(q_ref, kn_ref, vn_ref, gu_ref, gv_ref, x_ref, ck_ref, cv_ref,
                       sink_ref, ctab_ref, bias_ref, ga_ref, gg_ref, wo_ref,
                       o_ref, nk_ref, nv_ref, attn_s, sgu_s):
    nseq = ck_ref.shape[0]
    t = q_ref.shape[0] // nseq
    w = ck_ref.shape[1]
    low, lowf, highf = _half_masks()
    rows = N_Q_HEADS * t
    band = 2 * w
    i = lax.broadcasted_iota(I32, (rows, band), 0) & (t - 1)
    j = lax.broadcasted_iota(I32, (rows, band), 1)
    valid = (j > i) & (j <= i + w)
    sink = sink_ref[:, 0:1]
    pad = jnp.zeros((band - w - t, LANES), F32)

    seq_rows = [slice(b * t, (b + 1) * t) for b in range(nseq)]
    scores = []
    for b, rs in enumerate(seq_rows):
        qs = _stack_heads(q_ref[rs, :], lowf, highf)
        kb = jnp.concatenate([ck_ref[b], kn_ref[rs, :], pad], axis=0).astype(BF16)
        scores.append(jnp.where(valid, _dot_nt(qs, kb), NEG_INF))
    probs = [_sink_softmax(s, sink) for s in scores]
    for b, rs in enumerate(seq_rows):
        vc, vn = cv_ref[b], vn_ref[rs, :]
        o_all = _dot(probs[b], jnp.concatenate([vc, vn, pad], axis=0).astype(BF16))
        for c in range(4):
            attn_s[rs, c * LANES:(c + 1) * LANES] = jnp.where(
                low, o_all[c * t:(c + 1) * t], o_all[(4 + c) * t:(5 + c) * t])
        nk_ref[b] = jnp.concatenate([ck_ref[b, t:, :], kn_ref[rs, :]], axis=0)
        nv_ref[b] = jnp.concatenate([vc[t:], vn], axis=0)
    for b, rs in enumerate(seq_rows):
        gvb = gv_ref[rs, :]
        mixed = ctab_ref[0] * gvb[0:1, :]
        for r in range(1, t):
            mixed = mixed + ctab_ref[r] * gvb[r:r + 1, :]
        sgu_s[rs, :] = gu_ref[rs, :] * (mixed + bias_ref[...])
    o_ref[...] = _merge_out(attn_s[...], sgu_s[...], ga_ref[...], gg_ref[...], wo_ref[...], x_ref[...])


def _mix_sample(q, k, v, gu, gv, x, ck, cv, sink_tab, ctab, bias, ga, gg, wo, *, group):
    nseq, w, _ = ck.shape
    t = x.shape[0] // nseq
    assert t == SUBLANES and w == WINDOW and nseq % group == 0
    rows = group * t
    row = lambda width: pl.BlockSpec((rows, width), lambda i: (i, 0))
    cache = pl.BlockSpec((group, w, LANES), lambda i: (i, 0, 0))
    return pl.pallas_call(
        _mix_sample_kernel,
        grid=(nseq // group,),
        in_specs=[row(512), row(128), row(128), row(512), row(512), row(x.shape[1]), cache, cache,
                  _full(sink_tab.shape), _full(ctab.shape), _full(bias.shape),
                  _full(ga.shape), _full(gg.shape), _full(wo.shape)],
        out_specs=[row(x.shape[1]), cache, cache],
        out_shape=[jax.ShapeDtypeStruct(x.shape, F32), jax.ShapeDtypeStruct(ck.shape, F32),
                   jax.ShapeDtypeStruct(cv.shape, F32)],
        scratch_shapes=[pltpu.VMEM((rows, 512), F32), pltpu.VMEM((rows, 512), F32)],
        compiler_params=_params("parallel"),
        name="mix_sample",
    )(q, k, v, gu, gv, x, ck, cv, sink_tab, ctab, bias, ga, gg, wo)


def _mem_kv_kernel(m_ref, g_ref, w_ref, k_ref, v_ref):
    mn = _rms(m_ref[...], g_ref[...]).astype(BF16)
    kv = _dot(mn, w_ref[...])
    half = kv.shape[1] // 2
    k_ref[...] = kv[:, :half]
    v_ref[...] = kv[:, half:]


def _mem_kv(mem, g, w_kv, *, rows):
    t, d = mem.shape
    width = w_kv.shape[1] // 2
    row = lambda wd: pl.BlockSpec((rows, wd), lambda i: (i, 0))
    return pl.pallas_call(
        _mem_kv_kernel,
        grid=(t // rows,),
        in_specs=[row(d), _full(g.shape), _full(w_kv.shape)],
        out_specs=[row(width), row(width)],
        out_shape=[jax.ShapeDtypeStruct((t, width), F32)] * 2,
        compiler_params=_params("parallel"),
        name="mem_kv",
    )(mem, g, w_kv)


def _softmax(s):
    p = jnp.exp(s - jnp.max(s, axis=-1, keepdims=True))
    return (p / jnp.sum(p, axis=-1, keepdims=True)).astype(BF16)


def _mem_heads(q, mk, mv, low, lowf, highf):
    rows = q.shape[0]
    outs = []
    for c in range(MEM_HEADS // 2):
        sl = slice(c * LANES, (c + 1) * LANES)
        qc = q[:, sl]
        q2 = jnp.concatenate([(qc * lowf).astype(BF16), (qc * highf).astype(BF16)], axis=0)
        o = _dot(_softmax(_dot_nt(q2, mk[:, sl].astype(BF16))), mv[:, sl].astype(BF16))
        outs.append(jnp.where(low, o[:rows], o[rows:]))
    return jnp.concatenate(outs, axis=1)


def _mem_attn_prompt_kernel(x_ref, g_ref, wq_ref, mk_ref, mv_ref, wo_ref, o_ref):
    low, lowf, highf = _half_masks()
    x = x_ref[...]
    q = _dot(_rms(x, g_ref[...]).astype(BF16), wq_ref[...]) * (MEM_HEAD_DIM ** -0.5)
    o = _mem_heads(q, mk_ref[...], mv_ref[...], low, lowf, highf)
    o_ref[...] = x + _dot(o.astype(BF16), wo_ref[...])


def _mem_attn_prompt(x, g, wq, mk, mv, wo, *, batch, seq, rows):
    n_mem = mk.shape[0] // batch
    nb = seq // rows
    row = pl.BlockSpec((rows, x.shape[1]), lambda b, n: (b * nb + n, 0))
    mem = pl.BlockSpec((n_mem, mk.shape[1]), lambda b, n: (b, 0))
    return pl.pallas_call(
        _mem_attn_prompt_kernel,
        grid=(batch, nb),
        in_specs=[row, _full(g.shape), _full(wq.shape), mem, mem, _full(wo.shape)],
        out_specs=row,
        out_shape=jax.ShapeDtypeStruct(x.shape, F32),
        compiler_params=_params("parallel", "parallel"),
        name="mem_attn_prompt",
    )(x, g, wq, mk, mv, wo)


def _mem_attn_sample_kernel(x_ref, g_ref, wq_ref, mk_ref, mv_ref, wo_ref, o_ref, q_s, a_s):
    nseq = mk_ref.shape[0]
    t = x_ref.shape[0] // nseq
    low, lowf, highf = _half_masks()
    x = x_ref[...]
    q_s[...] = _dot(_rms(x, g_ref[...]).astype(BF16), wq_ref[...]) * (MEM_HEAD_DIM ** -0.5)
    for c in range(MEM_HEADS // 2):
        sl = slice(c * LANES, (c + 1) * LANES)
        scores = []
        for b in range(nseq):
            qc = q_s[b * t:(b + 1) * t, sl]
            q2 = jnp.concatenate([(qc * lowf).astype(BF16), (qc * highf).astype(BF16)], axis=0)
            scores.append(_dot_nt(q2, mk_ref[b, :, sl].astype(BF16)))
        p = _softmax(jnp.concatenate(scores, axis=0))
        for b in range(nseq):
            o = _dot(p[2 * t * b:2 * t * (b + 1)], mv_ref[b, :, sl].astype(BF16))
            a_s[b * t:(b + 1) * t, sl] = jnp.where(low, o[:t], o[t:])
    o_ref[...] = x + _dot(a_s[...].astype(BF16), wo_ref[...])


def _mem_attn_sample(x, g, wq, mk, mv, wo, *, group):
    nseq, n_mem, width = mk.shape
    t = x.shape[0] // nseq
    rows = group * t
    row = pl.BlockSpec((rows, x.shape[1]), lambda i: (i, 0))
    mem = pl.BlockSpec((group, n_mem, width), lambda i: (i, 0, 0))
    return pl.pallas_call(
        _mem_attn_sample_kernel,
        grid=(nseq // group,),
        in_specs=[row, _full(g.shape), _full(wq.shape), mem, mem, _full(wo.shape)],
        out_specs=row,
        out_shape=jax.ShapeDtypeStruct(x.shape, F32),
        scratch_shapes=[pltpu.VMEM((rows, width), F32), pltpu.VMEM((rows, width), F32)],
        compiler_params=_params("parallel"),
        name="mem_attn_sample",
    )(x, g, wq, mk, mv, wo)


def _top16(s):
    nrows, cols = s.shape
    r = lax.broadcasted_iota(I32, (nrows, cols), 0).astype(F32)
    slot = lax.broadcasted_iota(I32, (PEER_TOPK, cols), 0)
    vals = jnp.zeros((PEER_TOPK, cols), F32)
    idxs = jnp.zeros((PEER_TOPK, cols), F32)
    for p in range(PEER_TOPK):
        m = jnp.max(s, axis=0, keepdims=True)
        am = jnp.min(jnp.where(s == m, r, float(nrows)), axis=0, keepdims=True)
        vals = jnp.where(slot == p, m, vals)
        idxs = jnp.where(slot == p, am, idxs)
        s = jnp.where(r == am, -jnp.inf, s)
    return vals, idxs


def _sort16_pairs():
    def merge(lo, hi, r):
        step = r * 2
        if step < hi - lo:
            yield from merge(lo, hi, step)
            yield from merge(lo + r, hi, step)
            yield from [(i, i + r) for i in range(lo + r, hi - r, step)]
        else:
            yield (lo, lo + r)

    def sort(lo, hi):
        if hi > lo:
            mid = lo + (hi - lo) // 2
            yield from sort(lo, mid)
            yield from sort(mid + 1, hi)
            yield from merge(lo, hi, 1)

    return tuple(sort(0, PEER_TOPK - 1))


SORT16 = _sort16_pairs()


def _cmpx(v, pays, i, j):
    swap = v[j] > v[i]
    hi, lo = jnp.maximum(v[i], v[j]), jnp.minimum(v[i], v[j])
    for p in pays:
        p[i], p[j] = jnp.where(swap, p[j], p[i]), jnp.where(swap, p[i], p[j])
    v[i], v[j] = hi, lo


def _merge16(va, pa, vb, pb):
    n = PEER_TOPK
    v, pays = [], [[] for _ in pa]
    for i in range(n):
        other = vb[n - 1 - i]
        if other is None:
            v.append(va[i])
            for k in range(len(pa)):
                pays[k].append(pa[k][i])
            continue
        take = other > va[i]
        v.append(jnp.maximum(va[i], other))
        for k in range(len(pa)):
            pays[k].append(jnp.where(take, pb[k][n - 1 - i], pa[k][i]))
    d = n // 2
    while d:
        for i in range(n):
            if not i & d:
                _cmpx(v, pays, i, i + d)
        d //= 2
    return v, pays


def _xor_rows(x, d, sub):
    if d == SUBLANES // 2:
        return pltpu.roll(x, d, 0)
    return jnp.where((sub & d) == 0, pltpu.roll(x, SUBLANES - d, 0), pltpu.roll(x, d, 0))


def _row(x, k):
    return jnp.broadcast_to(x[k:k + 1, :], x.shape)


def _route_sorted(s_s, rows, i1_s, i2_s, gt_s):
    n = PEER_TOPK
    sub = lax.broadcasted_iota(I32, (SUBLANES, LANES), 0)
    subf = sub.astype(F32)
    bit2 = (sub & 4) == 0
    bit1 = (sub & 2) == 0
    halves = [slice(0, LANES), slice(LANES, 2 * LANES)]
    bad = jnp.zeros((SUBLANES, LANES), F32)

    def packed_merge(mask, d, xa, xb):
        va, pa = xa
        vb, pb = xb
        lv = [jnp.where(mask, a, b) for a, b in zip(va, vb)]
        rv = [_xor_rows(jnp.where(mask, b, a), d, sub) for a, b in zip(va, vb)]
        lp = [[jnp.where(mask, a, b) for a, b in zip(qa, qb)] for qa, qb in zip(pa, pb)]
        rp = [[_xor_rows(jnp.where(mask, b, a), d, sub) for a, b in zip(qa, qb)] for qa, qb in zip(pa, pb)]
        return _merge16(lv, lp, rv, rp)

    def self_merge(d, x):
        v, p = x
        return _merge16(v, p, [_xor_rows(a, d, sub) for a in v], [[_xor_rows(a, d, sub) for a in q] for q in p])

    def decreasing(v):
        ok = v[0] > v[1]
        flag = jnp.where(ok, 0.0, 1.0)
        for r in range(1, n - 1):
            flag = jnp.where(v[r] > v[r + 1], flag, 1.0)
        return flag

    sorted_lists = {}
    for c in range(2):
        for part in range(2):
            v = [s_s[c, SUBLANES * g:SUBLANES * (g + 1), halves[part]] for g in range(N_KEYS // SUBLANES)]
            ix = [subf + float(SUBLANES * g) for g in range(N_KEYS // SUBLANES)]
            pays = [ix]
            for i, j in SORT16:
                _cmpx(v, pays, i, j)
            sorted_lists[c, part] = (v, pays)
    x1 = [packed_merge(bit2, 4, sorted_lists[c, 0], sorted_lists[c, 1]) for c in range(2)]
    x2 = packed_merge(bit1, 2, x1[0], x1[1])
    tv, (ti,) = self_merge(1, x2)
    bad = jnp.maximum(bad, decreasing(tv))
    for c in range(2):
        for part in range(2):
            thr = _row(tv[n - 1], 4 * part + 2 * c)
            cnt = jnp.zeros((SUBLANES, LANES), F32)
            for g in range(N_KEYS // SUBLANES):
                cnt = cnt + jnp.where(s_s[c, SUBLANES * g:SUBLANES * (g + 1), halves[part]] >= thr, 1.0, 0.0)
            total = jnp.sum(cnt, axis=0, keepdims=True)
            bad = jnp.maximum(bad, jnp.broadcast_to(jnp.where(total == float(n), 0.0, 1.0), bad.shape))

    half = n // 2
    lists, tails = [], []
    for part in range(2):
        base = 4 * part
        s1 = [_row(tv[r], base) for r in range(n)]
        a1 = [_row(ti[r], base) for r in range(n)]
        s2 = [_row(tv[r], base + 2) for r in range(n)]
        a2 = [_row(ti[r], base + 2) for r in range(n)]
        p1, ip1 = s1[0], a1[0]
        for p in range(1, half):
            p1 = jnp.where(sub == p, s1[p], p1)
            ip1 = jnp.where(sub == p, a1[p], ip1)
        lists.append(([p1 + s2[q] for q in range(n)], [[ip1] * n, list(a2)]))
        tails.append(([s1[half + i] + s2[0] for i in range(half)],
                      [[a1[half + i] for i in range(half)], [a2[0]] * half]))
    y = packed_merge(bit2, 4, lists[0], lists[1])
    y = self_merge(2, y)
    y = self_merge(1, y)
    dv = [jnp.where(bit2, a, b) for a, b in zip(tails[0][0], tails[1][0])] + [None] * half
    dp = [[jnp.where(bit2, a, b) for a, b in zip(qa, qb)] + [None] * half
          for qa, qb in zip(tails[0][1], tails[1][1])]
    top, (sel1, sel2) = _merge16(y[0], y[1], dv, dp)
    bad = jnp.maximum(bad, decreasing(top))
    for part in range(2):
        thr = _row(top[n - 1], 4 * part)
        cnt = jnp.zeros((SUBLANES, LANES), F32)
        for q in range(n):
            cnt = cnt + jnp.where(lists[part][0][q] >= thr, 1.0, 0.0)
        total = jnp.sum(cnt, axis=0, keepdims=True)
        for i in range(half):
            total = total + jnp.where(tails[part][0][i][0:1, :] >= thr[0:1, :], 1.0, 0.0)
        bad = jnp.maximum(bad, jnp.broadcast_to(jnp.where(total == float(n), 0.0, 1.0), bad.shape))

    e = [jnp.exp(t - top[0]) for t in top]
    z = e[0]
    for k in range(1, n):
        z = z + e[k]
    gate = [ek / z for ek in e]
    for src, dst in ((sel1, i1_s), (sel2, i2_s), (gate, gt_s)):
        for grp in range(2):
            for part in range(2):
                tile = None
                for k in range(SUBLANES):
                    slab = src[grp * SUBLANES + k]
                    if (k < 4) != (part == 0):
                        slab = pltpu.roll(slab, 4, 0)
                    tile = slab if tile is None else jnp.where(sub == k, slab, tile)
                dst[rows[grp], halves[part]] = tile
    return jnp.max(bad)


def _peer_route_kernel(x_ref, g_ref, w_ref, k_ref,
                       xn_ref, i1_ref, i2_ref, gate_ref, q_s, s_s, i1_s, i2_s, gt_s):
    tb = x_ref.shape[0]
    xn = _rms(x_ref[...], g_ref[...]).astype(BF16)
    xn_ref[...] = xn
    q = _dot(xn, w_ref[...]).astype(BF16)
    for hc in range(2 * PEER_HEADS):
        q_s[hc] = q[:, hc * D_HALF:(hc + 1) * D_HALF]

    group = s_s.shape[0]

    def head_pair(k, carry):
        tied = []
        for hh in range(group):
            h = group * k + hh
            for c in range(2):
                s_s[hh, c] = _dot_nt(k_ref[2 * h + c], q_s[2 * h + c])
            rows = [pl.ds(pl.multiple_of(h * PEER_TOPK + r, SUBLANES), SUBLANES) for r in (0, SUBLANES)]
            tied.append(_route_sorted(s_s.at[hh], rows, i1_s, i2_s, gt_s))

        for hh in range(group):
            @pl.when(tied[hh] > 0.0)
            def _():
                for part in range(tb // LANES):
                    head_part(group * k + hh, hh, part)

        return carry

    def head_part(h, hh, part):
        cols = slice(part * LANES, (part + 1) * LANES)
        (s1, i1), (s2, i2) = [_top16(s_s[hh, c, :, cols]) for c in range(2)]
        tb = LANES
        half = PEER_TOPK // 2
        cand = jnp.concatenate(
            [jnp.broadcast_to(s1[0:1], (PEER_TOPK, tb)) + s2]
            + [jnp.broadcast_to(s1[p:p + 1], (half, tb)) + s2[:half] for p in range(1, half)]
            + [s1[half:] + jnp.broadcast_to(s2[0:1], (half, tb))], axis=0)
        top, cidx = _top16(cand)
        ci = cidx.astype(I32)
        mid = ci - PEER_TOPK
        tail = PEER_TOPK + half * (half - 1)
        pi = jnp.where(ci < PEER_TOPK, 0,
                       jnp.where(ci < tail, 1 + (mid >> (half.bit_length() - 1)), ci - tail + half))
        qi = jnp.where(ci < PEER_TOPK, ci, jnp.where(ci < tail, mid & (half - 1), 0))
        e1 = jnp.zeros((PEER_TOPK, tb), F32)
        e2 = jnp.zeros((PEER_TOPK, tb), F32)
        for p in range(PEER_TOPK):
            e1 = jnp.where(pi == p, jnp.broadcast_to(i1[p:p + 1], (PEER_TOPK, tb)), e1)
            e2 = jnp.where(qi == p, jnp.broadcast_to(i2[p:p + 1], (PEER_TOPK, tb)), e2)
        e = jnp.exp(top - jnp.max(top, axis=0, keepdims=True))
        rs = pl.ds(pl.multiple_of(h * PEER_TOPK, PEER_TOPK), PEER_TOPK)
        i1_s[rs, cols] = e1
        i2_s[rs, cols] = e2
        gt_s[rs, cols] = e / jnp.sum(e, axis=0, keepdims=True)

    lax.fori_loop(0, PEER_HEADS // group, head_pair, 0)
    i1_ref[...] = i1_s[...].T.astype(I32)
    i2_ref[...] = i2_s[...].T.astype(I32)
    gate_ref[...] = gt_s[...].T


def _peer_route(x, g, w, k):
    t, d = x.shape
    tb = 2 * LANES
    assert t % tb == 0
    sel = PEER_HEADS * PEER_TOPK
    row = lambda width: pl.BlockSpec((tb, width), lambda i: (i, 0))
    return pl.pallas_call(
        _peer_route_kernel,
        grid=(t // tb,),
        in_specs=[row(d), _full(g.shape), _full(w.shape), _full(k.shape)],
        out_specs=[row(d), row(sel), row(sel), row(sel)],
        out_shape=[jax.ShapeDtypeStruct((t, d), BF16), jax.ShapeDtypeStruct((t, sel), I32),
                   jax.ShapeDtypeStruct((t, sel), I32), jax.ShapeDtypeStruct((t, sel), F32)],
        scratch_shapes=[pltpu.VMEM((2 * PEER_HEADS, tb, D_HALF), BF16),
                        pltpu.VMEM((ROUTE_HEADS_PER_BODY, 2, N_KEYS, tb), F32),
                        pltpu.VMEM((sel, tb), F32), pltpu.VMEM((sel, tb), F32), pltpu.VMEM((sel, tb), F32)],
        compiler_params=_params("parallel"),
        name="peer_route",
    )(x, g, w, k)


def _peer_gates_kernel(i1_ref, i2_ref, gate_ref, o_ref, g_s):
    tb = i1_ref.shape[0]
    sel = i1_ref.shape[1]
    key = lax.broadcasted_iota(I32, (N_KEYS, sel), 0)
    zero = jnp.zeros((N_KEYS, sel), BF16)

    tile = 2 * SUBLANES

    def build(tidx):
        w1, w2 = [], []
        for half in range(2):
            t0 = pl.multiple_of(tidx * tile + half * SUBLANES, SUBLANES)
            i1g = i1_ref[pl.ds(t0, SUBLANES), :]
            i2g = i2_ref[pl.ds(t0, SUBLANES), :]
            gtg = gate_ref[pl.ds(t0, SUBLANES), :]
            w1 += [jnp.where(key == i1g[s:s + 1, :], gtg[s:s + 1, :], 0.0).astype(BF16) for s in range(SUBLANES)]
            w2 += [jnp.where(key == i2g[s:s + 1, :], 1.0, 0.0).astype(BF16) for s in range(SUBLANES)]
        for s in range(0, tile, 2):
            lhs = jnp.concatenate([w1[s], w1[s + 1]], axis=1)
            rhs = jnp.concatenate([jnp.concatenate([w2[s], zero], axis=1),
                                   jnp.concatenate([zero, w2[s + 1]], axis=1)], axis=0)
            g = _dot_nt(lhs, rhs)
            word = pltpu.pack_elementwise([g[:, :N_KEYS], g[:, N_KEYS:]], packed_dtype=BF16)
            pair = tidx * SUBLANES + s // 2
            g_s[pl.ds(pl.multiple_of(pair * GATE_PITCH, SUBLANES), N_KEYS), :] = word

    def relayout(tidx):
        rows = pl.ds(pl.multiple_of(tidx * tile, tile), tile)
        for a in range(N_KEYS):
            words = g_s[pl.ds(tidx * SUBLANES * GATE_PITCH + a, SUBLANES, stride=GATE_PITCH), :]
            o_ref[rows, a * N_KEYS:(a + 1) * N_KEYS] = pltpu.bitcast(words, BF16)

    group = GATE_TILES_PER_STEP
    n_groups = tb // (tile * group)
    for u in range(group):
        build(u)

    def step(gidx, carry):
        for u in range(group):
            relayout((gidx - 1) * group + u)
        for u in range(group):
            build(gidx * group + u)
        return carry

    lax.fori_loop(1, n_groups, step, 0)
    for u in range(group):
        relayout((n_groups - 1) * group + u)


def _peer_gates(i1, i2, gate, *, rows):
    t, sel = i1.shape
    row = pl.BlockSpec((rows, sel), lambda i: (i, 0))
    return pl.pallas_call(
        _peer_gates_kernel,
        grid=(t // rows,),
        in_specs=[row, row, row],
        out_specs=pl.BlockSpec((rows, N_KEYS * N_KEYS), lambda i: (i, 0)),
        out_shape=jax.ShapeDtypeStruct((t, N_KEYS * N_KEYS), BF16),
        scratch_shapes=[pltpu.VMEM((rows // 2 * GATE_PITCH, N_KEYS), jnp.uint32)],
        compiler_params=_params("parallel"),
        name="peer_gates",
    )(i1, i2, gate)


def _peer_dense_kernel(xn_ref, gates_ref, u_ref, v_ref, x_ref, g_ref, o_ref, acc):
    k = pl.program_id(1)

    @pl.when(k == 0)
    def _():
        acc[...] = jnp.zeros_like(acc)

    h = _dot_nt(xn_ref[...], u_ref[...])
    a = (_gelu(h) * gates_ref[...].astype(F32)).astype(BF16)
    acc[...] += _dot(a, v_ref[...])

    @pl.when(k == pl.num_programs(1) - 1)
    def _():
        o_ref[...] = _rms(x_ref[...] + acc[...], g_ref[...])


def _peer_dense(xn, gates, u, v, x, g, *, rows, experts):
    t, d = x.shape
    n_exp = u.shape[0]
    tok = pl.BlockSpec((rows, d), lambda i, k: (i, 0))
    tab = pl.BlockSpec((experts, d), lambda i, k: (k, 0))
    return pl.pallas_call(
        _peer_dense_kernel,
        grid=(t // rows, n_exp // experts),
        in_specs=[tok, pl.BlockSpec((rows, experts), lambda i, k: (i, k)), tab, tab, tok, _full(g.shape)],
        out_specs=tok,
        out_shape=jax.ShapeDtypeStruct((t, d), F32),
        scratch_shapes=[pltpu.VMEM((rows, d), F32)],
        compiler_params=_params("parallel", "arbitrary"),
        name="peer_dense",
    )(xn, gates, u, v, x, g)


def _peer(x, g_ffn, wq, keys, u, v, g_final):
    t = x.shape[0]
    xn, i1, i2, gate = _peer_route(x, g_ffn, wq, keys)
    gates = _peer_gates(i1, i2, gate, rows=2 * LANES)
    return _peer_dense(xn, gates, u, v, x, g_final, rows=min(512, t), experts=2048)


def _head_perm():
    new = np.zeros(ATTN_WIDTH, np.int32)
    for h in range(N_Q_HEADS):
        dst = (h % 4) * LANES + (h // 4) * HEAD_DIM
        new[dst:dst + HEAD_DIM] = np.arange(h * HEAD_DIM, (h + 1) * HEAD_DIM)
    return new


def _rope_tables(pos):
    half = HEAD_DIM // 2
    inv = ROPE_THETA ** (-jnp.arange(half, dtype=F32) / half)
    ang = pos.astype(F32)[:, None] * inv[None, :]
    cos, sin = jnp.cos(ang), jnp.sin(ang)
    reps = LANES // HEAD_DIM
    return (jnp.tile(jnp.concatenate([cos, cos], axis=1), (1, reps)),
            jnp.tile(jnp.concatenate([-sin, sin], axis=1), (1, reps)))


def kernel(x_prompt, x_sample, mem_prompt, cache_swa_k, cache_swa_v, cache_mem_k, cache_mem_v, g_mix, w_in, attn_sinks, g_sgu, w_spatial, b_spatial, g_attn_out, g_gmlp_out, w_out, g_cross, g_mem, w_cq, w_mk, w_mv, w_co, g_ffn, w_peer_q, peer_sub_keys, peer_u, peer_v, g_final):
    batch, seq, d = x_prompt.shape
    dec_batch, dec_seq, _ = x_sample.shape
    depth = g_mix.shape[0]
    n_mem = mem_prompt.shape[1]
    assert seq % WINDOW == 0 and dec_seq == SUBLANES and cache_swa_k.shape[2] == WINDOW

    perm = _head_perm()
    cos_p, sin_p = _rope_tables(jnp.arange(seq, dtype=I32))
    cos_s, sin_s = _rope_tables(PAST_LEN +(jnp.arange(dec_batch * dec_seq, dtype=I32) % dec_seq))
    seg = jnp.asarray(np.kron(np.eye(N_GMLP_HEADS), np.full((HEAD_DIM, HEAD_DIM), 1.0 / HEAD_DIM)), BF16)
    row2 = lambda a: a.reshape(1, -1)

    xp = x_prompt.reshape(batch * seq, d)
    xs = x_sample.reshape(dec_batch * dec_seq, d)
    outs = {name: [] for name in ("kp", "vp", "mk", "mv", "ks", "vs", "gvs")}
    for l in range(depth):
        w_in_l = jnp.concatenate([w_in[l][:, :ATTN_WIDTH][:, perm], w_in[l][:, ATTN_WIDTH:]], axis=1).astype(BF16)
        w_out_l = jnp.concatenate([w_out[l][:ATTN_WIDTH][perm], w_out[l][ATTN_WIDTH:]], axis=0).astype(BF16)
        ga = row2(g_attn_out[l][perm])
        gg = row2(g_gmlp_out[l])
        gs = row2(g_sgu[l])
        bias = jnp.repeat(b_spatial[l].T, HEAD_DIM, axis=1)
        wt = jnp.tril(w_spatial[l][:, :dec_seq, :dec_seq])
        ctab = jnp.repeat(jnp.transpose(wt, (2, 1, 0)), HEAD_DIM, axis=2)
        sink_tab = jnp.broadcast_to(jnp.repeat(attn_sinks[l], dec_seq)[:, None], (N_Q_HEADS * dec_seq, LANES))

        mix = functools.partial(_mix_in, g=row2(g_mix[l]), w_in=w_in_l, gs=gs, seg=seg)
        rows_p = next(r for r in (1024, 512, WINDOW) if seq % r == 0)
        qp, kp, vp, gup, gvp = mix(xp, cosf=cos_p, sinf=sin_p, rows=rows_p, table_blocks=seq // rows_p)
        rows_s = min(512, dec_batch * dec_seq)
        qs, ks, vs, gus, gvs = mix(xs, cosf=cos_s, sinf=sin_s, rows=rows_s,
                                   table_blocks=dec_batch * dec_seq // rows_s)

        xp = _mix_prompt(attn_sinks[l], qp, kp, vp, gup, gvp, xp, w_spatial[l], bias, ga, gg, w_out_l,
                         batch=batch, seq=seq, rows=next(r for r in (512, 256, WINDOW) if seq % r == 0))
        ck = cache_swa_k[l].reshape(dec_batch, WINDOW, N_KV_HEADS * HEAD_DIM)
        cv = cache_swa_v[l].reshape(dec_batch, WINDOW, N_KV_HEADS * HEAD_DIM)
        xs, nks, nvs = _mix_sample(qs, ks, vs, gus, gvs, xs, ck, cv, sink_tab, ctab, bias[:dec_seq], ga, gg,
                                   w_out_l, group=min(16, dec_batch))

        w_kv = jnp.concatenate([w_mk[l], w_mv[l]], axis=1).astype(BF16)
        mk, mv = _mem_kv(mem_prompt.reshape(batch * n_mem, d), row2(g_mem[l]), w_kv, rows=n_mem)
        wq = w_cq[l].astype(BF16)
        wo = w_co[l].astype(BF16)
        gc = row2(g_cross[l])
        xp = _mem_attn_prompt(xp, gc, wq, mk, mv, wo, batch=batch, seq=seq, rows=512 if seq % 512 == 0 else 256)
        cmk = cache_mem_k[l].reshape(dec_batch, n_mem, MEM_HEADS * MEM_HEAD_DIM)
        cmv = cache_mem_v[l].reshape(dec_batch, n_mem, MEM_HEADS * MEM_HEAD_DIM)
        xs = _mem_attn_sample(xs, gc, wq, cmk, cmv, wo, group=min(8, dec_batch))

        wpq = w_peer_q[l].astype(BF16)
        keys = peer_sub_keys[l].reshape(2 * PEER_HEADS, N_KEYS, D_HALF).astype(BF16)
        u = peer_u[l].astype(BF16)
        v = peer_v[l].astype(BF16)
        last = l == depth - 1
        assert last, "stacked layers need an un-normalised PEER output"
        peer = functools.partial(_peer, g_ffn=row2(g_ffn[l]), wq=wpq, keys=keys, u=u, v=v,
                                 g_final=row2(g_final))
        xp = peer(xp)
        xs = peer(xs)

        outs["kp"].append(kp.reshape(batch, seq, N_KV_HEADS, HEAD_DIM)[:, seq - WINDOW:])
        outs["vp"].append(vp.reshape(batch, seq, N_KV_HEADS, HEAD_DIM)[:, seq - WINDOW:])
        outs["mk"].append(mk.reshape(batch, n_mem, MEM_HEADS, MEM_HEAD_DIM))
        outs["mv"].append(mv.reshape(batch, n_mem, MEM_HEADS, MEM_HEAD_DIM))
        outs["ks"].append(nks.reshape(dec_batch, WINDOW, N_KV_HEADS, HEAD_DIM))
        outs["vs"].append(nvs.reshape(dec_batch, WINDOW, N_KV_HEADS, HEAD_DIM))
        outs["gvs"].append(gvs.reshape(dec_batch, dec_seq, N_GMLP_HEADS, HEAD_DIM))

    stack = lambda name: jnp.stack(outs[name])
    return (xp.reshape(batch, seq, d), xs.reshape(dec_batch, dec_seq, d),
            stack("kp"), stack("vp"), stack("mk"), stack("mv"), stack("ks"), stack("vs"), stack("gvs"))
```

```python
import functools

import jax
import jax.numpy as jnp
import numpy as np
from jax import lax
from jax.experimental import pallas as pl
from jax.experimental.pallas import tpu as pltpu

F32 = jnp.float32
BF16 = jnp.bfloat16
I32 = jnp.int32

LANES = 128
SUBLANES = 8
VMEM_LIMIT_BYTES = 56 * 1024 * 1024

HEAD_DIM = 64
N_Q_HEADS = 8
N_KV_HEADS = 2
ATTN_WIDTH = N_Q_HEADS * HEAD_DIM
WINDOW = 128
PAST_LEN = 8192
ROPE_THETA = 10000.0
N_GMLP_HEADS = 8
GMLP_WIDTH = N_GMLP_HEADS * HEAD_DIM
CHUNK = 128
MEM_HEADS = 4
MEM_HEAD_DIM = 64
PEER_HEADS = 8
N_KEYS = 128
PEER_TOPK = 16
D_HALF = 128
EPS = 1e-6
NEG_INF = -1e30
SQRT_HALF = float(np.sqrt(0.5))

GATE_PITCH = N_KEYS + SUBLANES
ROUTE_HEADS_PER_BODY = 8
GATE_TILES_PER_STEP = 8

NT_DIMS = (((1,), (1,)), ((), ()))


def _params(*semantics):
    return pltpu.CompilerParams(dimension_semantics=semantics, vmem_limit_bytes=VMEM_LIMIT_BYTES)


def _full(shape):
    zeros = (0,) * len(shape)
    return pl.BlockSpec(shape, lambda *_: zeros)


def _rms(x, g):
    return x * lax.rsqrt(jnp.mean(x * x, axis=-1, keepdims=True) + EPS) * g


def _gelu(x):
    return 0.5 * x * (1.0 + lax.erf(x * SQRT_HALF))


def _dot(a, b):
    return jnp.dot(a, b, preferred_element_type=F32)


def _dot_nt(a, b):
    return lax.dot_general(a, b, NT_DIMS, preferred_element_type=F32)


def _split(x):
    hi = x.astype(BF16)
    return hi, (x - hi.astype(F32)).astype(BF16)


def _half_masks():
    lane = lax.broadcasted_iota(I32, (1, LANES), 1)
    low = lane < HEAD_DIM
    return low, low.astype(F32), 1.0 - low.astype(F32)


def _mix_in_kernel(x_ref, g_ref, w_ref, cos_ref, sin_ref, gs_ref, seg_ref,
                   q_ref, k_ref, v_ref, gu_ref, gv_ref):
    xn = _rms(x_ref[...], g_ref[...]).astype(BF16)
    proj = _dot(xn, w_ref[...])
    cosf = cos_ref[...]
    sinf = sin_ref[...]
    lane = lax.broadcasted_iota(I32, (1, LANES), 1)
    first_half = (lane & (HEAD_DIM - 1)) < HEAD_DIM // 2

    def rope(c):
        partner = jnp.where(first_half, pltpu.roll(c, LANES - HEAD_DIM // 2, 1),
                            pltpu.roll(c, HEAD_DIM // 2, 1))
        return c * cosf + partner * sinf

    for c in range(ATTN_WIDTH // LANES):
        sl = slice(c * LANES, (c + 1) * LANES)
        q_ref[:, sl] = rope(proj[:, sl]) * (HEAD_DIM ** -0.5)
    k_ref[...] = rope(proj[:, 512:640])
    v_ref[...] = proj[:, 640:768]
    gu_ref[...] = _gelu(proj[:, 768:1280])
    gv = _gelu(proj[:, 1280:1792])
    hi, lo = _split(gv * gv)
    seg = seg_ref[...]
    ms = _dot(hi, seg) + _dot(lo, seg)
    gv_ref[...] = gv * lax.rsqrt(ms + EPS) * gs_ref[...]


def _mix_in(x, g, w_in, cosf, sinf, gs, seg, *, rows, table_blocks):
    t = x.shape[0]
    d = x.shape[1]
    row = lambda width: pl.BlockSpec((rows, width), lambda i: (i, 0))
    tab = pl.BlockSpec((rows, LANES), lambda i: (i % table_blocks, 0))
    return pl.pallas_call(
        _mix_in_kernel,
        grid=(t // rows,),
        in_specs=[row(d), _full(g.shape), _full(w_in.shape), tab, tab, _full(gs.shape), _full(seg.shape)],
        out_specs=[row(512), row(128), row(128), row(512), row(512)],
        out_shape=[jax.ShapeDtypeStruct((t, 512), F32), jax.ShapeDtypeStruct((t, 128), F32),
                   jax.ShapeDtypeStruct((t, 128), F32), jax.ShapeDtypeStruct((t, 512), F32),
                   jax.ShapeDtypeStruct((t, 512), F32)],
        compiler_params=_params("parallel"),
        name="mix_in",
    )(x, g, w_in, cosf, sinf, gs, seg)


def _stack_heads(q, lowf, highf):
    parts = [(q[:, c * LANES:(c + 1) * LANES] * m).astype(BF16)
             for m in (lowf, highf) for c in range(4)]
    return jnp.concatenate(parts, axis=0)


def _sink_softmax(s, sink):
    m = jnp.maximum(jnp.max(s, axis=-1, keepdims=True), sink)
    p = jnp.exp(s - m)
    den = jnp.sum(p, axis=-1, keepdims=True) + jnp.exp(sink - m)
    return (p / den).astype(BF16)


def _merge_out(attn, sgu, ga, gg, wo, x):
    merged = jnp.concatenate([_rms(attn, ga), _rms(sgu, gg)], axis=1).astype(BF16)
    return x + _dot(merged, wo)


def _mix_prompt_kernel(sinks_ref, q_ref, kp_ref, kc_ref, vp_ref, vc_ref, gu_ref, gv_ref, x_ref,
                       ws_ref, bias_ref, ga_ref, gg_ref, wo_ref, o_ref):
    n = pl.program_id(1)
    low, lowf, highf = _half_masks()
    blk = WINDOW
    nsub = q_ref.shape[0] // blk
    sub = [slice(s * blk, (s + 1) * blk) for s in range(nsub)]
    band = [slice(s * blk, (s + 2) * blk) for s in range(nsub)]
    kall = jnp.concatenate([kp_ref[...], kc_ref[...]], axis=0).astype(BF16)
    vall = jnp.concatenate([vp_ref[...], vc_ref[...]], axis=0).astype(BF16)
    q = q_ref[...]
    i = lax.broadcasted_iota(I32, (blk, 2 * blk), 0)
    j = lax.broadcasted_iota(I32, (blk, 2 * blk), 1)
    in_window = (j > i) & (j <= i + blk)
    first_key = jnp.where(n > 0, 0, blk)
    valid = [in_window & (j >= first_key)] + [in_window] * (nsub - 1)
    scores = [_dot_nt(_stack_heads(q[sub[s]], lowf, highf), kall[band[s]]) for s in range(nsub)]
    probs = [jnp.concatenate(
        [_sink_softmax(jnp.where(valid[s], scores[s][h * blk:(h + 1) * blk], NEG_INF), sinks_ref[h])
         for h in range(N_Q_HEADS)], axis=0) for s in range(nsub)]
    outs = [_dot(probs[s], vall[band[s]]) for s in range(nsub)]
    attn = jnp.concatenate([jnp.concatenate(
        [jnp.where(low, o[c * blk:(c + 1) * blk], o[(4 + c) * blk:(5 + c) * blk]) for c in range(4)], axis=1)
        for o in outs], axis=0)

    gvb = gv_ref[...].astype(BF16)
    r = lax.broadcasted_iota(I32, (CHUNK, CHUNK), 0)
    c_ = lax.broadcasted_iota(I32, (CHUNK, CHUNK), 1)
    tril = r >= c_
    w = [jnp.where(tril, ws_ref[h], 0.0).astype(BF16) for h in range(N_GMLP_HEADS)]
    bias = bias_ref[...]
    mixed = jnp.concatenate([jnp.concatenate(
        [jnp.where(low, _dot(w[2 * c], gvb[sub[s], c * LANES:(c + 1) * LANES]),
                   _dot(w[2 * c + 1], gvb[sub[s], c * LANES:(c + 1) * LANES])) for c in range(4)], axis=1) + bias
        for s in range(nsub)], axis=0)
    sgu = gu_ref[...] * mixed
    o_ref[...] = _merge_out(attn, sgu, ga_ref[...], gg_ref[...], wo_ref[...], x_ref[...])


def _mix_prompt(sinks, q, k, v, gu, gv, x, ws, bias, ga, gg, wo, *, batch, seq, rows):
    nb = seq // rows
    per = rows // WINDOW
    cur = lambda width: pl.BlockSpec((rows, width), lambda b, n: (b * nb + n, 0))
    prev = lambda width: pl.BlockSpec(
        (WINDOW, width), lambda b, n: (b * nb * per + jnp.maximum(n * per - 1, 0), 0))
    return pl.pallas_call(
        _mix_prompt_kernel,
        grid=(batch, nb),
        in_specs=[pl.BlockSpec(memory_space=pltpu.SMEM),
                  cur(512), prev(128), cur(128), prev(128), cur(128), cur(512), cur(512), cur(x.shape[1]),
                  _full(ws.shape), _full(bias.shape), _full(ga.shape), _full(gg.shape), _full(wo.shape)],
        out_specs=cur(x.shape[1]),
        out_shape=jax.ShapeDtypeStruct(x.shape, F32),
        compiler_params=_params("parallel", "parallel"),
        name="mix_prompt",
    )(sinks, q, k, k, v, v, gu, gv, x, ws, bias, ga, gg, wo)


def _mix_sample_kernel(q_ref, kn_ref, vn_ref, gu_ref, gv_ref, x_ref, ck_ref, cv_ref,
                       sink_ref, ctab_ref, bias_ref, ga_ref, gg_ref, wo_ref,
                       o_ref, nk_ref, nv_ref, attn_s, sgu_s):
    nseq = ck_ref.shape[0]
    t = q_ref.shape[0] // nseq
    w = ck_ref.shape[1]
    low, lowf, highf = _half_masks()
    rows = N_Q_HEADS * t
    band = 2 * w
    i = lax.broadcasted_iota(I32, (rows, band), 0) & (t - 1)
    j = lax.broadcasted_iota(I32, (rows, band), 1)
    valid = (j > i) & (j <= i + w)
    sink = sink_ref[:, 0:1]
    pad = jnp.zeros((band - w - t, LANES), F32)

    seq_rows = [slice(b * t, (b + 1) * t) for b in range(nseq)]
    scores = []
    for b, rs in enumerate(seq_rows):
        qs = _stack_heads(q_ref[rs, :], lowf, highf)
        kb = jnp.concatenate([ck_ref[b], kn_ref[rs, :], pad], axis=0).astype(BF16)
        scores.append(jnp.where(valid, _dot_nt(qs, kb), NEG_INF))
    probs = [_sink_softmax(s, sink) for s in scores]
    for b, rs in enumerate(seq_rows):
        vc, vn = cv_ref[b], vn_ref[rs, :]
        o_all = _dot(probs[b], jnp.concatenate([vc, vn, pad], axis=0).astype(BF16))
        for c in range(4):
            attn_s[rs, c * LANES:(c + 1) * LANES] = jnp.where(
                low, o_all[c * t:(c + 1) * t], o_all[(4 + c) * t:(5 + c) * t])
        nk_ref[b] = jnp.concatenate([ck_ref[b, t:, :], kn_ref[rs, :]], axis=0)
        nv_ref[b] = jnp.concatenate([vc[t:], vn], axis=0)
    for b, rs in enumerate(seq_rows):
        gvb = gv_ref[rs, :]
        mixed = ctab_ref[0] * gvb[0:1, :]
        for r in range(1, t):
            mixed = mixed + ctab_ref[r] * gvb[r:r + 1, :]
        sgu_s[rs, :] = gu_ref[rs, :] * (mixed + bias_ref[...])
    o_ref[...] = _merge_out(attn_s[...], sgu_s[...], ga_ref[...], gg_ref[...], wo_ref[...], x_ref[...])


def _mix_sample(q, k, v, gu, gv, x, ck, cv, sink_tab, ctab, bias, ga, gg, wo, *, group):
    nseq, w, _ = ck.shape
    t = x.shape[0] // nseq
    assert t == SUBLANES and w == WINDOW and nseq % group == 0
    rows = group * t
    row = lambda width: pl.BlockSpec((rows, width), lambda i: (i, 0))
    cache = pl.BlockSpec((group, w, LANES), lambda i: (i, 0, 0))
    return pl.pallas_call(
        _mix_sample_kernel,
        grid=(nseq // group,),
        in_specs=[row(512), row(128), row(128), row(512), row(512), row(x.shape[1]), cache, cache,
                  _full(sink_tab.shape), _full(ctab.shape), _full(bias.shape),
                  _full(ga.shape), _full(gg.shape), _full(wo.shape)],
        out_specs=[row(x.shape[1]), cache, cache],
        out_shape=[jax.ShapeDtypeStruct(x.shape, F32), jax.ShapeDtypeStruct(ck.shape, F32),
                   jax.ShapeDtypeStruct(cv.shape, F32)],
        scratch_shapes=[pltpu.VMEM((rows, 512), F32), pltpu.VMEM((rows, 512), F32)],
        compiler_params=_params("parallel"),
        name="mix_sample",
    )(q, k, v, gu, gv, x, ck, cv, sink_tab, ctab, bias, ga, gg, wo)


def _mem_kv_kernel(m_ref, g_ref, w_ref, k_ref, v_ref):
    mn = _rms(m_ref[...], g_ref[...]).astype(BF16)
    kv = _dot(mn, w_ref[...])
    half = kv.shape[1] // 2
    k_ref[...] = kv[:, :half]
    v_ref[...] = kv[:, half:]


def _mem_kv(mem, g, w_kv, *, rows):
    t, d = mem.shape
    width = w_kv.shape[1] // 2
    row = lambda wd: pl.BlockSpec((rows, wd), lambda i: (i, 0))
    return pl.pallas_call(
        _mem_kv_kernel,
        grid=(t // rows,),
        in_specs=[row(d), _full(g.shape), _full(w_kv.shape)],
        out_specs=[row(width), row(width)],
        out_shape=[jax.ShapeDtypeStruct((t, width), F32)] * 2,
        compiler_params=_params("parallel"),
        name="mem_kv",
    )(mem, g, w_kv)


def _softmax(s):
    p = jnp.exp(s - jnp.max(s, axis=-1, keepdims=True))
    return (p / jnp.sum(p, axis=-1, keepdims=True)).astype(BF16)


def _mem_heads(q, mk, mv, low, lowf, highf):
    rows = q.shape[0]
    outs = []
    for c in range(MEM_HEADS // 2):
        sl = slice(c * LANES, (c + 1) * LANES)
        qc = q[:, sl]
        q2 = jnp.concatenate([(qc * lowf).astype(BF16), (qc * highf).astype(BF16)], axis=0)
        o = _dot(_softmax(_dot_nt(q2, mk[:, sl].astype(BF16))), mv[:, sl].astype(BF16))
        outs.append(jnp.where(low, o[:rows], o[rows:]))
    return jnp.concatenate(outs, axis=1)


def _mem_attn_prompt_kernel(x_ref, g_ref, wq_ref, mk_ref, mv_ref, wo_ref, o_ref):
    low, lowf, highf = _half_masks()
    x = x_ref[...]
    q = _dot(_rms(x, g_ref[...]).astype(BF16), wq_ref[...]) * (MEM_HEAD_DIM ** -0.5)
    o = _mem_heads(q, mk_ref[...], mv_ref[...], low, lowf, highf)
    o_ref[...] = x + _dot(o.astype(BF16), wo_ref[...])


def _mem_attn_prompt(x, g, wq, mk, mv, wo, *, batch, seq, rows):
    n_mem = mk.shape[0] // batch
    nb = seq // rows
    row = pl.BlockSpec((rows, x.shape[1]), lambda b, n: (b * nb + n, 0))
    mem = pl.BlockSpec((n_mem, mk.shape[1]), lambda b, n: (b, 0))
    return pl.pallas_call(
        _mem_attn_prompt_kernel,
        grid=(batch, nb),
        in_specs=[row, _full(g.shape), _full(wq.shape), mem, mem, _full(wo.shape)],
        out_specs=row,
        out_shape=jax.ShapeDtypeStruct(x.shape, F32),
        compiler_params=_params("parallel", "parallel"),
        name="mem_attn_prompt",
    )(x, g, wq, mk, mv, wo)


def _mem_attn_sample_kernel(x_ref, g_ref, wq_ref, mk_ref, mv_ref, wo_ref, o_ref, q_s, a_s):
    nseq = mk_ref.shape[0]
    t = x_ref.shape[0] // nseq
    low, lowf, highf = _half_masks()
    x = x_ref[...]
    q_s[...] = _dot(_rms(x, g_ref[...]).astype(BF16), wq_ref[...]) * (MEM_HEAD_DIM ** -0.5)
    for c in range(MEM_HEADS // 2):
        sl = slice(c * LANES, (c + 1) * LANES)
        scores = []
        for b in range(nseq):
            qc = q_s[b * t:(b + 1) * t, sl]
            q2 = jnp.concatenate([(qc * lowf).astype(BF16), (qc * highf).astype(BF16)], axis=0)
            scores.append(_dot_nt(q2, mk_ref[b, :, sl].astype(BF16)))
        p = _softmax(jnp.concatenate(scores, axis=0))
        for b in range(nseq):
            o = _dot(p[2 * t * b:2 * t * (b + 1)], mv_ref[b, :, sl].astype(BF16))
            a_s[b * t:(b + 1) * t, sl] = jnp.where(low, o[:t], o[t:])
    o_ref[...] = x + _dot(a_s[...].astype(BF16), wo_ref[...])


def _mem_attn_sample(x, g, wq, mk, mv, wo, *, group):
    nseq, n_mem, width = mk.shape
    t = x.shape[0] // nseq
    rows = group * t
    row = pl.BlockSpec((rows, x.shape[1]), lambda i: (i, 0))
    mem = pl.BlockSpec((group, n_mem, width), lambda i: (i, 0, 0))
    return pl.pallas_call(
        _mem_attn_sample_kernel,
        grid=(nseq // group,),
        in_specs=[row, _full(g.shape), _full(wq.shape), mem, mem, _full(wo.shape)],
        out_specs=row,
        out_shape=jax.ShapeDtypeStruct(x.shape, F32),
        scratch_shapes=[pltpu.VMEM((rows, width), F32), pltpu.VMEM((rows, width), F32)],
        compiler_params=_params("parallel"),
        name="mem_attn_sample",
    )(x, g, wq, mk, mv, wo)


def _top16(s):
    nrows, cols = s.shape
    r = lax.broadcasted_iota(I32, (nrows, cols), 0).astype(F32)
    slot = lax.broadcasted_iota(I32, (PEER_TOPK, cols), 0)
    vals = jnp.zeros((PEER_TOPK, cols), F32)
    idxs = jnp.zeros((PEER_TOPK, cols), F32)
    for p in range(PEER_TOPK):
        m = jnp.max(s, axis=0, keepdims=True)
        am = jnp.min(jnp.where(s == m, r, float(nrows)), axis=0, keepdims=True)
        vals = jnp.where(slot == p, m, vals)
        idxs = jnp.where(slot == p, am, idxs)
        s = jnp.where(r == am, -jnp.inf, s)
    return vals, idxs


def _sort16_pairs():
    def merge(lo, hi, r):
        step = r * 2
        if step < hi - lo:
            yield from merge(lo, hi, step)
            yield from merge(lo + r, hi, step)
            yield from [(i, i + r) for i in range(lo + r, hi - r, step)]
        else:
            yield (lo, lo + r)

    def sort(lo, hi):
        if hi > lo:
            mid = lo + (hi - lo) // 2
            yield from sort(lo, mid)
            yield from sort(mid + 1, hi)
            yield from merge(lo, hi, 1)

    return tuple(sort(0, PEER_TOPK - 1))


SORT16 = _sort16_pairs()


def _cmpx(v, pays, i, j):
    swap = v[j] > v[i]
    hi, lo = jnp.maximum(v[i], v[j]), jnp.minimum(v[i], v[j])
    for p in pays:
        p[i], p[j] = jnp.where(swap, p[j], p[i]), jnp.where(swap, p[i], p[j])
    v[i], v[j] = hi, lo


def _merge16(va, pa, vb, pb):
    n = PEER_TOPK
    v, pays = [], [[] for _ in pa]
    for i in range(n):
        other = vb[n - 1 - i]
        if other is None:
            v.append(va[i])
            for k in range(len(pa)):
                pays[k].append(pa[k][i])
            continue
        take = other > va[i]
        v.append(jnp.maximum(va[i], other))
        for k in range(len(pa)):
            pays[k].append(jnp.where(take, pb[k][n - 1 - i], pa[k][i]))
    d = n // 2
    while d:
        for i in range(n):
            if not i & d:
                _cmpx(v, pays, i, i + d)
        d //= 2
    return v, pays


def _xor_rows(x, d, sub):
    if d == SUBLANES // 2:
        return pltpu.roll(x, d, 0)
    return jnp.where((sub & d) == 0, pltpu.roll(x, SUBLANES - d, 0), pltpu.roll(x, d, 0))


def _row(x, k):
    return jnp.broadcast_to(x[k:k + 1, :], x.shape)


def _route_sorted(s_s, rows, i1_s, i2_s, gt_s):
    n = PEER_TOPK
    sub = lax.broadcasted_iota(I32, (SUBLANES, LANES), 0)
    subf = sub.astype(F32)
    bit2 = (sub & 4) == 0
    bit1 = (sub & 2) == 0
    halves = [slice(0, LANES), slice(LANES, 2 * LANES)]
    bad = jnp.zeros((SUBLANES, LANES), F32)

    def packed_merge(mask, d, xa, xb):
        va, pa = xa
        vb, pb = xb
        lv = [jnp.where(mask, a, b) for a, b in zip(va, vb)]
        rv = [_xor_rows(jnp.where(mask, b, a), d, sub) for a, b in zip(va, vb)]
        lp = [[jnp.where(mask, a, b) for a, b in zip(qa, qb)] for qa, qb in zip(pa, pb)]
        rp = [[_xor_rows(jnp.where(mask, b, a), d, sub) for a, b in zip(qa, qb)] for qa, qb in zip(pa, pb)]
        return _merge16(lv, lp, rv, rp)

    def self_merge(d, x):
        v, p = x
        return _merge16(v, p, [_xor_rows(a, d, sub) for a in v], [[_xor_rows(a, d, sub) for a in q] for q in p])

    def decreasing(v):
        ok = v[0] > v[1]
        flag = jnp.where(ok, 0.0, 1.0)
        for r in range(1, n - 1):
            flag = jnp.where(v[r] > v[r + 1], flag, 1.0)
        return flag

    sorted_lists = {}
    for c in range(2):
        for part in range(2):
            v = [s_s[c, SUBLANES * g:SUBLANES * (g + 1), halves[part]] for g in range(N_KEYS // SUBLANES)]
            ix = [subf + float(SUBLANES * g) for g in range(N_KEYS // SUBLANES)]
            pays = [ix]
            for i, j in SORT16:
                _cmpx(v, pays, i, j)
            sorted_lists[c, part] = (v, pays)
    x1 = [packed_merge(bit2, 4, sorted_lists[c, 0], sorted_lists[c, 1]) for c in range(2)]
    x2 = packed_merge(bit1, 2, x1[0], x1[1])
    tv, (ti,) = self_merge(1, x2)
    bad = jnp.maximum(bad, decreasing(tv))
    for c in range(2):
        for part in range(2):
            thr = _row(tv[n - 1], 4 * part + 2 * c)
            cnt = jnp.zeros((SUBLANES, LANES), F32)
            for g in range(N_KEYS // SUBLANES):
                cnt = cnt + jnp.where(s_s[c, SUBLANES * g:SUBLANES * (g + 1), halves[part]] >= thr, 1.0, 0.0)
            total = jnp.sum(cnt, axis=0, keepdims=True)
            bad = jnp.maximum(bad, jnp.broadcast_to(jnp.where(total == float(n), 0.0, 1.0), bad.shape))

    half = n // 2
    lists, tails = [], []
    for part in range(2):
        base = 4 * part
        s1 = [_row(tv[r], base) for r in range(n)]
        a1 = [_row(ti[r], base) for r in range(n)]
        s2 = [_row(tv[r], base + 2) for r in range(n)]
        a2 = [_row(ti[r], base + 2) for r in range(n)]
        p1, ip1 = s1[0], a1[0]
        for p in range(1, half):
            p1 = jnp.where(sub == p, s1[p], p1)
            ip1 = jnp.where(sub == p, a1[p], ip1)
        lists.append(([p1 + s2[q] for q in range(n)], [[ip1] * n, list(a2)]))
        tails.append(([s1[half + i] + s2[0] for i in range(half)],
                      [[a1[half + i] for i in range(half)], [a2[0]] * half]))
    y = packed_merge(bit2, 4, lists[0], lists[1])
    y = self_merge(2, y)
    y = self_merge(1, y)
    dv = [jnp.where(bit2, a, b) for a, b in zip(tails[0][0], tails[1][0])] + [None] * half
    dp = [[jnp.where(bit2, a, b) for a, b in zip(qa, qb)] + [None] * half
          for qa, qb in zip(tails[0][1], tails[1][1])]
    top, (sel1, sel2) = _merge16(y[0], y[1], dv, dp)
    bad = jnp.maximum(bad, decreasing(top))
    for part in range(2):
        thr = _row(top[n - 1], 4 * part)
        cnt = jnp.zeros((SUBLANES, LANES), F32)
        for q in range(n):
            cnt = cnt + jnp.where(lists[part][0][q] >= thr, 1.0, 0.0)
        total = jnp.sum(cnt, axis=0, keepdims=True)
        for i in range(half):
            total = total + jnp.where(tails[part][0][i][0:1, :] >= thr[0:1, :], 1.0, 0.0)
        bad = jnp.maximum(bad, jnp.broadcast_to(jnp.where(total == float(n), 0.0, 1.0), bad.shape))

    e = [jnp.exp(t - top[0]) for t in top]
    z = e[0]
    for k in range(1, n):
        z = z + e[k]
    gate = [ek / z for ek in e]
    for src, dst in ((sel1, i1_s), (sel2, i2_s), (gate, gt_s)):
        for grp in range(2):
            for part in range(2):
                tile = None
                for k in range(SUBLANES):
                    slab = src[grp * SUBLANES + k]
                    if (k < 4) != (part == 0):
                        slab = pltpu.roll(slab, 4, 0)
                    tile = slab if tile is None else jnp.where(sub == k, slab, tile)
                dst[rows[grp], halves[part]] = tile
    return jnp.max(bad)


def _peer_route_kernel(x_ref, g_ref, w_ref, k_ref,
                       xn_ref, i1_ref, i2_ref, gate_ref, q_s, s_s, i1_s, i2_s, gt_s):
    tb = x_ref.shape[0]
    xn = _rms(x_ref[...], g_ref[...]).astype(BF16)
    xn_ref[...] = xn
    q = _dot(xn, w_ref[...]).astype(BF16)
    for hc in range(2 * PEER_HEADS):
        q_s[hc] = q[:, hc * D_HALF:(hc + 1) * D_HALF]

    group = s_s.shape[0]

    def head_pair(k, carry):
        tied = []
        for hh in range(group):
            h = group * k + hh
            for c in range(2):
                s_s[hh, c] = _dot_nt(k_ref[2 * h + c], q_s[2 * h + c])
            rows = [pl.ds(pl.multiple_of(h * PEER_TOPK + r, SUBLANES), SUBLANES) for r in (0, SUBLANES)]
            tied.append(_route_sorted(s_s.at[hh], rows, i1_s, i2_s, gt_s))

        for hh in range(group):
            @pl.when(tied[hh] > 0.0)
            def _():
                for part in range(tb // LANES):
                    head_part(group * k + hh, hh, part)

        return carry

    def head_part(h, hh, part):
        cols = slice(part * LANES, (part + 1) * LANES)
        (s1, i1), (s2, i2) = [_top16(s_s[hh, c, :, cols]) for c in range(2)]
        tb = LANES
        half = PEER_TOPK // 2
        cand = jnp.concatenate(
            [jnp.broadcast_to(s1[0:1], (PEER_TOPK, tb)) + s2]
            + [jnp.broadcast_to(s1[p:p + 1], (half, tb)) + s2[:half] for p in range(1, half)]
            + [s1[half:] + jnp.broadcast_to(s2[0:1], (half, tb))], axis=0)
        top, cidx = _top16(cand)
        ci = cidx.astype(I32)
        mid = ci - PEER_TOPK
        tail = PEER_TOPK + half * (half - 1)
        pi = jnp.where(ci < PEER_TOPK, 0,
                       jnp.where(ci < tail, 1 + (mid >> (half.bit_length() - 1)), ci - tail + half))
        qi = jnp.where(ci < PEER_TOPK, ci, jnp.where(ci < tail, mid & (half - 1), 0))
        e1 = jnp.zeros((PEER_TOPK, tb), F32)
        e2 = jnp.zeros((PEER_TOPK, tb), F32)
        for p in range(PEER_TOPK):
            e1 = jnp.where(pi == p, jnp.broadcast_to(i1[p:p + 1], (PEER_TOPK, tb)), e1)
            e2 = jnp.where(qi == p, jnp.broadcast_to(i2[p:p + 1], (PEER_TOPK, tb)), e2)
        e = jnp.exp(top - jnp.max(top, axis=0, keepdims=True))
        rs = pl.ds(pl.multiple_of(h * PEER_TOPK, PEER_TOPK), PEER_TOPK)
        i1_s[rs, cols] = e1
        i2_s[rs, cols] = e2
        gt_s[rs, cols] = e / jnp.sum(e, axis=0, keepdims=True)

    lax.fori_loop(0, PEER_HEADS // group, head_pair, 0)
    i1_ref[...] = i1_s[...].T.astype(I32)
    i2_ref[...] = i2_s[...].T.astype(I32)
    gate_ref[...] = gt_s[...].T


def _peer_route(x, g, w, k):
    t, d = x.shape
    tb = 2 * LANES
    assert t % tb == 0
    sel = PEER_HEADS * PEER_TOPK
    row = lambda width: pl.BlockSpec((tb, width), lambda i: (i, 0))
    return pl.pallas_call(
        _peer_route_kernel,
        grid=(t // tb,),
        in_specs=[row(d), _full(g.shape), _full(w.shape), _full(k.shape)],
        out_specs=[row(d), row(sel), row(sel), row(sel)],
        out_shape=[jax.ShapeDtypeStruct((t, d), BF16), jax.ShapeDtypeStruct((t, sel), I32),
                   jax.ShapeDtypeStruct((t, sel), I32), jax.ShapeDtypeStruct((t, sel), F32)],
        scratch_shapes=[pltpu.VMEM((2 * PEER_HEADS, tb, D_HALF), BF16),
                        pltpu.VMEM((ROUTE_HEADS_PER_BODY, 2, N_KEYS, tb), F32),
                        pltpu.VMEM((sel, tb), F32), pltpu.VMEM((sel, tb), F32), pltpu.VMEM((sel, tb), F32)],
        compiler_params=_params("parallel"),
        name="peer_route",
    )(x, g, w, k)


def _peer_gates_kernel(i1_ref, i2_ref, gate_ref, o_ref, g_s):
    tb = i1_ref.shape[0]
    sel = i1_ref.shape[1]
    key = lax.broadcasted_iota(I32, (N_KEYS, sel), 0)
    zero = jnp.zeros((N_KEYS, sel), BF16)

    tile = 2 * SUBLANES

    def build(tidx):
        w1, w2 = [], []
        for half in range(2):
            t0 = pl.multiple_of(tidx * tile + half * SUBLANES, SUBLANES)
            i1g = i1_ref[pl.ds(t0, SUBLANES), :]
            i2g = i2_ref[pl.ds(t0, SUBLANES), :]
            gtg = gate_ref[pl.ds(t0, SUBLANES), :]
            w1 += [jnp.where(key == i1g[s:s + 1, :], gtg[s:s + 1, :], 0.0).astype(BF16) for s in range(SUBLANES)]
            w2 += [jnp.where(key == i2g[s:s + 1, :], 1.0, 0.0).astype(BF16) for s in range(SUBLANES)]
        for s in range(0, tile, 2):
            lhs = jnp.concatenate([w1[s], w1[s + 1]], axis=1)
            rhs = jnp.concatenate([jnp.concatenate([w2[s], zero], axis=1),
                                   jnp.concatenate([zero, w2[s + 1]], axis=1)], axis=0)
            g = _dot_nt(lhs, rhs)
            word = pltpu.pack_elementwise([g[:, :N_KEYS], g[:, N_KEYS:]], packed_dtype=BF16)
            pair = tidx * SUBLANES + s // 2
            g_s[pl.ds(pl.multiple_of(pair * GATE_PITCH, SUBLANES), N_KEYS), :] = word

    def relayout(tidx):
        rows = pl.ds(pl.multiple_of(tidx * tile, tile), tile)
        for a in range(N_KEYS):
            words = g_s[pl.ds(tidx * SUBLANES * GATE_PITCH + a, SUBLANES, stride=GATE_PITCH), :]
            o_ref[rows, a * N_KEYS:(a + 1) * N_KEYS] = pltpu.bitcast(words, BF16)

    group = GATE_TILES_PER_STEP
    n_groups = tb // (tile * group)
    for u in range(group):
        build(u)

    def step(gidx, carry):
        for u in range(group):
            relayout((gidx - 1) * group + u)
        for u in range(group):
            build(gidx * group + u)
        return carry

    lax.fori_loop(1, n_groups, step, 0)
    for u in range(group):
        relayout((n_groups - 1) * group + u)


def _peer_gates(i1, i2, gate, *, rows):
    t, sel = i1.shape
    row = pl.BlockSpec((rows, sel), lambda i: (i, 0))
    return pl.pallas_call(
        _peer_gates_kernel,
        grid=(t // rows,),
        in_specs=[row, row, row],
        out_specs=pl.BlockSpec((rows, N_KEYS * N_KEYS), lambda i: (i, 0)),
        out_shape=jax.ShapeDtypeStruct((t, N_KEYS * N_KEYS), BF16),
        scratch_shapes=[pltpu.VMEM((rows // 2 * GATE_PITCH, N_KEYS), jnp.uint32)],
        compiler_params=_params("parallel"),
        name="peer_gates",
    )(i1, i2, gate)


def _peer_dense_kernel(xn_ref, gates_ref, u_ref, v_ref, x_ref, g_ref, o_ref, acc):
    k = pl.program_id(1)

    @pl.when(k == 0)
    def _():
        acc[...] = jnp.zeros_like(acc)

    h = _dot_nt(xn_ref[...], u_ref[...])
    a = (_gelu(h) * gates_ref[...].astype(F32)).astype(BF16)
    acc[...] += _dot(a, v_ref[...])

    @pl.when(k == pl.num_programs(1) - 1)
    def _():
        o_ref[...] = _rms(x_ref[...] + acc[...], g_ref[...])


def _peer_dense(xn, gates, u, v, x, g, *, rows, experts):
    t, d = x.shape
    n_exp = u.shape[0]
    tok = pl.BlockSpec((rows, d), lambda i, k: (i, 0))
    tab = pl.BlockSpec((experts, d), lambda i, k: (k, 0))
    return pl.pallas_call(
        _peer_dense_kernel,
        grid=(t // rows, n_exp // experts),
        in_specs=[tok, pl.BlockSpec((rows, experts), lambda i, k: (i, k)), tab, tab, tok, _full(g.shape)],
        out_specs=tok,
        out_shape=jax.ShapeDtypeStruct((t, d), F32),
        scratch_shapes=[pltpu.VMEM((rows, d), F32)],
        compiler_params=_params("parallel", "arbitrary"),
        name="peer_dense",
    )(xn, gates, u, v, x, g)


def _peer(x, g_ffn, wq, keys, u, v, g_final):
    t = x.shape[0]
    xn, i1, i2, gate = _peer_route(x, g_ffn, wq, keys)
    gates = _peer_gates(i1, i2, gate, rows=2 * LANES)
    return _peer_dense(xn, gates, u, v, x, g_final, rows=min(512, t), experts=2048)


def _head_perm():
    new = np.zeros(ATTN_WIDTH, np.int32)
    for h in range(N_Q_HEADS):
        dst = (h % 4) * LANES + (h // 4) * HEAD_DIM
        new[dst:dst + HEAD_DIM] = np.arange(h * HEAD_DIM, (h + 1) * HEAD_DIM)
    return new


def _rope_tables(pos):
    half = HEAD_DIM // 2
    inv = ROPE_THETA ** (-jnp.arange(half, dtype=F32) / half)
    ang = pos.astype(F32)[:, None] * inv[None, :]
    cos, sin = jnp.cos(ang), jnp.sin(ang)
    reps = LANES // HEAD_DIM
    return (jnp.tile(jnp.concatenate([cos, cos], axis=1), (1, reps)),
            jnp.tile(jnp.concatenate([-sin, sin], axis=1), (1, reps)))


def kernel(x_prompt, x_sample, mem_prompt, cache_swa_k, cache_swa_v, cache_mem_k, cache_mem_v, g_mix, w_in, attn_sinks, g_sgu, w_spatial, b_spatial, g_attn_out, g_gmlp_out, w_out, g_cross, g_mem, w_cq, w_mk, w_mv, w_co, g_ffn, w_peer_q, peer_sub_keys, peer_u, peer_v, g_final):
    batch, seq, d = x_prompt.shape
    dec_batch, dec_seq, _ = x_sample.shape
    depth = g_mix.shape[0]
    n_mem = mem_prompt.shape[1]
    assert seq % WINDOW == 0 and dec_seq == SUBLANES and cache_swa_k.shape[2] == WINDOW

    perm = _head_perm()
    cos_p, sin_p = _rope_tables(jnp.arange(seq, dtype=I32))
    cos_s, sin_s = _rope_tables(PAST_LEN +(jnp.arange(dec_batch * dec_seq, dtype=I32) % dec_seq))
    seg = jnp.asarray(np.kron(np.eye(N_GMLP_HEADS), np.full((HEAD_DIM, HEAD_DIM), 1.0 / HEAD_DIM)), BF16)
    row2 = lambda a: a.reshape(1, -1)

    xp = x_prompt.reshape(batch * seq, d)
    xs = x_sample.reshape(dec_batch * dec_seq, d)
    outs = {name: [] for name in ("kp", "vp", "mk", "mv", "ks", "vs", "gvs")}
    for l in range(depth):
        w_in_l = jnp.concatenate([w_in[l][:, :ATTN_WIDTH][:, perm], w_in[l][:, ATTN_WIDTH:]], axis=1).astype(BF16)
        w_out_l = jnp.concatenate([w_out[l][:ATTN_WIDTH][perm], w_out[l][ATTN_WIDTH:]], axis=0).astype(BF16)
        ga = row2(g_attn_out[l][perm])
        gg = row2(g_gmlp_out[l])
        gs = row2(g_sgu[l])
        bias = jnp.repeat(b_spatial[l].T, HEAD_DIM, axis=1)
        wt = jnp.tril(w_spatial[l][:, :dec_seq, :dec_seq])
        ctab = jnp.repeat(jnp.transpose(wt, (2, 1, 0)), HEAD_DIM, axis=2)
        sink_tab = jnp.broadcast_to(jnp.repeat(attn_sinks[l], dec_seq)[:, None], (N_Q_HEADS * dec_seq, LANES))

        mix = functools.partial(_mix_in, g=row2(g_mix[l]), w_in=w_in_l, gs=gs, seg=seg)
        rows_p = next(r for r in (1024, 512, WINDOW) if seq % r == 0)
        qp, kp, vp, gup, gvp = mix(xp, cosf=cos_p, sinf=sin_p, rows=rows_p, table_blocks=seq // rows_p)
        rows_s = min(512, dec_batch * dec_seq)
        qs, ks, vs, gus, gvs = mix(xs, cosf=cos_s, sinf=sin_s, rows=rows_s,
                                   table_blocks=dec_batch * dec_seq // rows_s)

        xp = _mix_prompt(attn_sinks[l], qp, kp, vp, gup, gvp, xp, w_spatial[l], bias, ga, gg, w_out_l,
                         batch=batch, seq=seq, rows=next(r for r in (512, 256, WINDOW) if seq % r == 0))
        ck = cache_swa_k[l].reshape(dec_batch, WINDOW, N_KV_HEADS * HEAD_DIM)
        cv = cache_swa_v[l].reshape(dec_batch, WINDOW, N_KV_HEADS * HEAD_DIM)
        xs, nks, nvs = _mix_sample(qs, ks, vs, gus, gvs, xs, ck, cv, sink_tab, ctab, bias[:dec_seq], ga, gg,
                                   w_out_l, group=min(16, dec_batch))

        w_kv = jnp.concatenate([w_mk[l], w_mv[l]], axis=1).astype(BF16)
        mk, mv = _mem_kv(mem_prompt.reshape(batch * n_mem, d), row2(g_mem[l]), w_kv, rows=n_mem)
        wq = w_cq[l].astype(BF16)
        wo = w_co[l].astype(BF16)
        gc = row2(g_cross[l])
        xp = _mem_attn_prompt(xp, gc, wq, mk, mv, wo, batch=batch, seq=seq, rows=512 if seq % 512 == 0 else 256)
        cmk = cache_mem_k[l].reshape(dec_batch, n_mem, MEM_HEADS * MEM_HEAD_DIM)
        cmv = cache_mem_v[l].reshape(dec_batch, n_mem, MEM_HEADS * MEM_HEAD_DIM)
        xs = _mem_attn_sample(xs, gc, wq, cmk, cmv, wo, group=min(8, dec_batch))

        wpq = w_peer_q[l].astype(BF16)
        keys = peer_sub_keys[l].reshape(2 * PEER_HEADS, N_KEYS, D_HALF).astype(BF16)
        u = peer_u[l].astype(BF16)
        v = peer_v[l].astype(BF16)
        last = l == depth - 1
        assert last, "stacked layers need an un-normalised PEER output"
        peer = functools.partial(_peer, g_ffn=row2(g_ffn[l]), wq=wpq, keys=keys, u=u, v=v,
                                 g_final=row2(g_final))
        xp = peer(xp)
        xs = peer(xs)

        outs["kp"].append(kp.reshape(batch, seq, N_KV_HEADS, HEAD_DIM)[:, seq - WINDOW:])
        outs["vp"].append(vp.reshape(batch, seq, N_KV_HEADS, HEAD_DIM)[:, seq - WINDOW:])
        outs["mk"].append(mk.reshape(batch, n_mem, MEM_HEADS, MEM_HEAD_DIM))
        outs["mv"].append(mv.reshape(batch, n_mem, MEM_HEADS, MEM_HEAD_DIM))
        outs["ks"].append(nks.reshape(dec_batch, WINDOW, N_KV_HEADS, HEAD_DIM))
        outs["vs"].append(nvs.reshape(dec_batch, WINDOW, N_KV_HEADS, HEAD_DIM))
        outs["gvs"].append(gvs.reshape(dec_batch, dec_seq, N_GMLP_HEADS, HEAD_DIM))

    stack = lambda name: jnp.stack(outs[name])
    return (xp.reshape(batch, seq, d), xs.reshape(dec_batch, dec_seq, d),
            stack("kp"), stack("vp"), stack("mk"), stack("mv"), stack("ks"), stack("vs"), stack("gvs"))
```

```python
import functools

import jax
import jax.numpy as jnp
import numpy as np
from jax import lax
from jax.experimental import pallas as pl
from jax.experimental.pallas import tpu as pltpu

F32 = jnp.float32
BF16 = jnp.bfloat16
I32 = jnp.int32

LANES = 128
SUBLANES = 8
VMEM_LIMIT_BYTES = 56 * 1024 * 1024

HEAD_DIM = 64
N_Q_HEADS = 8
N_KV_HEADS = 2
ATTN_WIDTH = N_Q_HEADS * HEAD_DIM
WINDOW = 128
PAST_LEN = 8192
ROPE_THETA = 10000.0
N_GMLP_HEADS = 8
GMLP_WIDTH = N_GMLP_HEADS * HEAD_DIM
CHUNK = 128
MEM_HEADS = 4
MEM_HEAD_DIM = 64
PEER_HEADS = 8
N_KEYS = 128
PEER_TOPK = 16
D_HALF = 128
EPS = 1e-6
NEG_INF = -1e30
SQRT_HALF = float(np.sqrt(0.5))

GATE_PITCH = N_KEYS + SUBLANES
ROUTE_HEADS_PER_BODY = 8
GATE_TILES_PER_STEP = 8

NT_DIMS = (((1,), (1,)), ((), ()))


def _params(*semantics):
    return pltpu.CompilerParams(dimension_semantics=semantics, vmem_limit_bytes=VMEM_LIMIT_BYTES)


def _full(shape):
    zeros = (0,) * len(shape)
    return pl.BlockSpec(shape, lambda *_: zeros)


def _rms(x, g):
    return x * lax.rsqrt(jnp.mean(x * x, axis=-1, keepdims=True) + EPS) * g


def _gelu(x):
    return 0.5 * x * (1.0 + lax.erf(x * SQRT_HALF))


def _dot(a, b):
    return jnp.dot(a, b, preferred_element_type=F32)


def _dot_nt(a, b):
    return lax.dot_general(a, b, NT_DIMS, preferred_element_type=F32)


def _split(x):
    hi = x.astype(BF16)
    return hi, (x - hi.astype(F32)).astype(BF16)


def _half_masks():
    lane = lax.broadcasted_iota(I32, (1, LANES), 1)
    low = lane < HEAD_DIM
    return low, low.astype(F32), 1.0 - low.astype(F32)


def _mix_in_kernel(x_ref, g_ref, w_ref, cos_ref, sin_ref, gs_ref, seg_ref,
                   q_ref, k_ref, v_ref, gu_ref, gv_ref):
    xn = _rms(x_ref[...], g_ref[...]).astype(BF16)
    proj = _dot(xn, w_ref[...])
    cosf = cos_ref[...]
    sinf = sin_ref[...]
    lane = lax.broadcasted_iota(I32, (1, LANES), 1)
    first_half = (lane & (HEAD_DIM - 1)) < HEAD_DIM // 2

    def rope(c):
        partner = jnp.where(first_half, pltpu.roll(c, LANES - HEAD_DIM // 2, 1),
                            pltpu.roll(c, HEAD_DIM // 2, 1))
        return c * cosf + partner * sinf

    for c in range(ATTN_WIDTH // LANES):
        sl = slice(c * LANES, (c + 1) * LANES)
        q_ref[:, sl] = rope(proj[:, sl]) * (HEAD_DIM ** -0.5)
    k_ref[...] = rope(proj[:, 512:640])
    v_ref[...] = proj[:, 640:768]
    gu_ref[...] = _gelu(proj[:, 768:1280])
    gv = _gelu(proj[:, 1280:1792])
    hi, lo = _split(gv * gv)
    seg = seg_ref[...]
    ms = _dot(hi, seg) + _dot(lo, seg)
    gv_ref[...] = gv * lax.rsqrt(ms + EPS) * gs_ref[...]


def _mix_in(x, g, w_in, cosf, sinf, gs, seg, *, rows, table_blocks):
    t = x.shape[0]
    d = x.shape[1]
    row = lambda width: pl.BlockSpec((rows, width), lambda i: (i, 0))
    tab = pl.BlockSpec((rows, LANES), lambda i: (i % table_blocks, 0))
    return pl.pallas_call(
        _mix_in_kernel,
        grid=(t // rows,),
        in_specs=[row(d), _full(g.shape), _full(w_in.shape), tab, tab, _full(gs.shape), _full(seg.shape)],
        out_specs=[row(512), row(128), row(128), row(512), row(512)],
        out_shape=[jax.ShapeDtypeStruct((t, 512), F32), jax.ShapeDtypeStruct((t, 128), F32),
                   jax.ShapeDtypeStruct((t, 128), F32), jax.ShapeDtypeStruct((t, 512), F32),
                   jax.ShapeDtypeStruct((t, 512), F32)],
        compiler_params=_params("parallel"),
        name="mix_in",
    )(x, g, w_in, cosf, sinf, gs, seg)


def _stack_heads(q, lowf, highf):
    parts = [(q[:, c * LANES:(c + 1) * LANES] * m).astype(BF16)
             for m in (lowf, highf) for c in range(4)]
    return jnp.concatenate(parts, axis=0)


def _sink_softmax(s, sink):
    m = jnp.maximum(jnp.max(s, axis=-1, keepdims=True), sink)
    p = jnp.exp(s - m)
    den = jnp.sum(p, axis=-1, keepdims=True) + jnp.exp(sink - m)
    return (p / den).astype(BF16)


def _merge_out(attn, sgu, ga, gg, wo, x):
    merged = jnp.concatenate([_rms(attn, ga), _rms(sgu, gg)], axis=1).astype(BF16)
    return x + _dot(merged, wo)


def _mix_prompt_kernel(sinks_ref, q_ref, kp_ref, kc_ref, vp_ref, vc_ref, gu_ref, gv_ref, x_ref,
                       ws_ref, bias_ref, ga_ref, gg_ref, wo_ref, o_ref):
    n = pl.program_id(1)
    low, lowf, highf = _half_masks()
    blk = WINDOW
    nsub = q_ref.shape[0] // blk
    sub = [slice(s * blk, (s + 1) * blk) for s in range(nsub)]
    band = [slice(s * blk, (s + 2) * blk) for s in range(nsub)]
    kall = jnp.concatenate([kp_ref[...], kc_ref[...]], axis=0).astype(BF16)
    vall = jnp.concatenate([vp_ref[...], vc_ref[...]], axis=0).astype(BF16)
    q = q_ref[...]
    i = lax.broadcasted_iota(I32, (blk, 2 * blk), 0)
    j = lax.broadcasted_iota(I32, (blk, 2 * blk), 1)
    in_window = (j > i) & (j <= i + blk)
    first_key = jnp.where(n > 0, 0, blk)
    valid = [in_window & (j >= first_key)] + [in_window] * (nsub - 1)
    scores = [_dot_nt(_stack_heads(q[sub[s]], lowf, highf), kall[band[s]]) for s in range(nsub)]
    probs = [jnp.concatenate(
        [_sink_softmax(jnp.where(valid[s], scores[s][h * blk:(h + 1) * blk], NEG_INF), sinks_ref[h])
         for h in range(N_Q_HEADS)], axis=0) for s in range(nsub)]
    outs = [_dot(probs[s], vall[band[s]]) for s in range(nsub)]
    attn = jnp.concatenate([jnp.concatenate(
        [jnp.where(low, o[c * blk:(c + 1) * blk], o[(4 + c) * blk:(5 + c) * blk]) for c in range(4)], axis=1)
        for o in outs], axis=0)

    gvb = gv_ref[...].astype(BF16)
    r = lax.broadcasted_iota(I32, (CHUNK, CHUNK), 0)
    c_ = lax.broadcasted_iota(I32, (CHUNK, CHUNK), 1)
    tril = r >= c_
    w = [jnp.where(tril, ws_ref[h], 0.0).astype(BF16) for h in range(N_GMLP_HEADS)]
    bias = bias_ref[...]
    mixed = jnp.concatenate([jnp.concatenate(
        [jnp.where(low, _dot(w[2 * c], gvb[sub[s], c * LANES:(c + 1) * LANES]),
                   _dot(w[2 * c + 1], gvb[sub[s], c * LANES:(c + 1) * LANES])) for c in range(4)], axis=1) + bias
        for s in range(nsub)], axis=0)
    sgu = gu_ref[...] * mixed
    o_ref[...] = _merge_out(attn, sgu, ga_ref[...], gg_ref[...], wo_ref[...], x_ref[...])


def _mix_prompt(sinks, q, k, v, gu, gv, x, ws, bias, ga, gg, wo, *, batch, seq, rows):
    nb = seq // rows
    per = rows // WINDOW
    cur = lambda width: pl.BlockSpec((rows, width), lambda b, n: (b * nb + n, 0))
    prev = lambda width: pl.BlockSpec(
        (WINDOW, width), lambda b, n: (b * nb * per + jnp.maximum(n * per - 1, 0), 0))
    return pl.pallas_call(
        _mix_prompt_kernel,
        grid=(batch, nb),
        in_specs=[pl.BlockSpec(memory_space=pltpu.SMEM),
                  cur(512), prev(128), cur(128), prev(128), cur(128), cur(512), cur(512), cur(x.shape[1]),
                  _full(ws.shape), _full(bias.shape), _full(ga.shape), _full(gg.shape), _full(wo.shape)],
        out_specs=cur(x.shape[1]),
        out_shape=jax.ShapeDtypeStruct(x.shape, F32),
        compiler_params=_params("parallel", "parallel"),
        name="mix_prompt",
    )(sinks, q, k, k, v, v, gu, gv, x, ws, bias, ga, gg, wo)


def _mix_sample_kernel(q_ref, kn_ref, vn_ref, gu_ref, gv_ref, x_ref, ck_ref, cv_ref,
                       sink_ref, ctab_ref, bias_ref, ga_ref, gg_ref, wo_ref,
                       o_ref, nk_ref, nv_ref, attn_s, sgu_s):
    nseq = ck_ref.shape[0]
    t = q_ref.shape[0] // nseq
    w = ck_ref.shape[1]
    low, lowf, highf = _half_masks()
    rows = N_Q_HEADS * t
    band = 2 * w
    i = lax.broadcasted_iota(I32, (rows, band), 0) & (t - 1)
    j = lax.broadcasted_iota(I32, (rows, band), 1)
    valid = (j > i) & (j <= i + w)
    sink = sink_ref[:, 0:1]
    pad = jnp.zeros((band - w - t, LANES), F32)

    seq_rows = [slice(b * t, (b + 1) * t) for b in range(nseq)]
    scores = []
    for b, rs in enumerate(seq_rows):
        qs = _stack_heads(q_ref[rs, :], lowf, highf)
        kb = jnp.concatenate([ck_ref[b], kn_ref[rs, :], pad], axis=0).astype(BF16)
        scores.append(jnp.where(valid, _dot_nt(qs, kb), NEG_INF))
    probs = [_sink_softmax(s, sink) for s in scores]
    for b, rs in enumerate(seq_rows):
        vc, vn = cv_ref[b], vn_ref[rs, :]
        o_all = _dot(probs[b], jnp.concatenate([vc, vn, pad], axis=0).astype(BF16))
        for c in range(4):
            attn_s[rs, c * LANES:(c + 1) * LANES] = jnp.where(
                low, o_all[c * t:(c + 1) * t], o_all[(4 + c) * t:(5 + c) * t])
        nk_ref[b] = jnp.concatenate([ck_ref[b, t:, :], kn_ref[rs, :]], axis=0)
        nv_ref[b] = jnp.concatenate([vc[t:], vn], axis=0)
    for b, rs in enumerate(seq_rows):
        gvb = gv_ref[rs, :]
        mixed = ctab_ref[0] * gvb[0:1, :]
        for r in range(1, t):
            mixed = mixed + ctab_ref[r] * gvb[r:r + 1, :]
        sgu_s[rs, :] = gu_ref[rs, :] * (mixed + bias_ref[...])
    o_ref[...] = _merge_out(attn_s[...], sgu_s[...], ga_ref[...], gg_ref[...], wo_ref[...], x_ref[...])


def _mix_sample(q, k, v, gu, gv, x, ck, cv, sink_tab, ctab, bias, ga, gg, wo, *, group):
    nseq, w, _ = ck.shape
    t = x.shape[0] // nseq
    assert t == SUBLANES and w == WINDOW and nseq % group == 0
    rows = group * t
    row = lambda width: pl.BlockSpec((rows, width), lambda i: (i, 0))
    cache = pl.BlockSpec((group, w, LANES), lambda i: (i, 0, 0))
    return pl.pallas_call(
        _mix_sample_kernel,
        grid=(nseq // group,),
        in_specs=[row(512), row(128), row(128), row(512), row(512), row(x.shape[1]), cache, cache,
                  _full(sink_tab.shape), _full(ctab.shape), _full(bias.shape),
                  _full(ga.shape), _full(gg.shape), _full(wo.shape)],
        out_specs=[row(x.shape[1]), cache, cache],
        out_shape=[jax.ShapeDtypeStruct(x.shape, F32), jax.ShapeDtypeStruct(ck.shape, F32),
                   jax.ShapeDtypeStruct(cv.shape, F32)],
        scratch_shapes=[pltpu.VMEM((rows, 512), F32), pltpu.VMEM((rows, 512), F32)],
        compiler_params=_params("parallel"),
        name="mix_sample",
    )(q, k, v, gu, gv, x, ck, cv, sink_tab, ctab, bias, ga, gg, wo)


def _mem_kv_kernel(m_ref, g_ref, w_ref, k_ref, v_ref):
    mn = _rms(m_ref[...], g_ref[...]).astype(BF16)
    kv = _dot(mn, w_ref[...])
    half = kv.shape[1] // 2
    k_ref[...] = kv[:, :half]
    v_ref[...] = kv[:, half:]


def _mem_kv(mem, g, w_kv, *, rows):
    t, d = mem.shape
    width = w_kv.shape[1] // 2
    row = lambda wd: pl.BlockSpec((rows, wd), lambda i: (i, 0))
    return pl.pallas_call(
        _mem_kv_kernel,
        grid=(t // rows,),
        in_specs=[row(d), _full(g.shape), _full(w_kv.shape)],
        out_specs=[row(width), row(width)],
        out_shape=[jax.ShapeDtypeStruct((t, width), F32)] * 2,
        compiler_params=_params("parallel"),
        name="mem_kv",
    )(mem, g, w_kv)


def _softmax(s):
    p = jnp.exp(s - jnp.max(s, axis=-1, keepdims=True))
    return (p / jnp.sum(p, axis=-1, keepdims=True)).astype(BF16)


def _mem_heads(q, mk, mv, low, lowf, highf):
    rows = q.shape[0]
    outs = []
    for c in range(MEM_HEADS // 2):
        sl = slice(c * LANES, (c + 1) * LANES)
        qc = q[:, sl]
        q2 = jnp.concatenate([(qc * lowf).astype(BF16), (qc * highf).astype(BF16)], axis=0)
        o = _dot(_softmax(_dot_nt(q2, mk[:, sl].astype(BF16))), mv[:, sl].astype(BF16))
        outs.append(jnp.where(low, o[:rows], o[rows:]))
    return jnp.concatenate(outs, axis=1)


def _mem_attn_prompt_kernel(x_ref, g_ref, wq_ref, mk_ref, mv_ref, wo_ref, o_ref):
    low, lowf, highf = _half_masks()
    x = x_ref[...]
    q = _dot(_rms(x, g_ref[...]).astype(BF16), wq_ref[...]) * (MEM_HEAD_DIM ** -0.5)
    o = _mem_heads(q, mk_ref[...], mv_ref[...], low, lowf, highf)
    o_ref[...] = x + _dot(o.astype(BF16), wo_ref[...])


def _mem_attn_prompt(x, g, wq, mk, mv, wo, *, batch, seq, rows):
    n_mem = mk.shape[0] // batch
    nb = seq // rows
    row = pl.BlockSpec((rows, x.shape[1]), lambda b, n: (b * nb + n, 0))
    mem = pl.BlockSpec((n_mem, mk.shape[1]), lambda b, n: (b, 0))
    return pl.pallas_call(
        _mem_attn_prompt_kernel,
        grid=(batch, nb),
        in_specs=[row, _full(g.shape), _full(wq.shape), mem, mem, _full(wo.shape)],
        out_specs=row,
        out_shape=jax.ShapeDtypeStruct(x.shape, F32),
        compiler_params=_params("parallel", "parallel"),
        name="mem_attn_prompt",
    )(x, g, wq, mk, mv, wo)


def _mem_attn_sample_kernel(x_ref, g_ref, wq_ref, mk_ref, mv_ref, wo_ref, o_ref, q_s, a_s):
    nseq = mk_ref.shape[0]
    t = x_ref.shape[0] // nseq
    low, lowf, highf = _half_masks()
    x = x_ref[...]
    q_s[...] = _dot(_rms(x, g_ref[...]).astype(BF16), wq_ref[...]) * (MEM_HEAD_DIM ** -0.5)
    for c in range(MEM_HEADS // 2):
        sl = slice(c * LANES, (c + 1) * LANES)
        scores = []
        for b in range(nseq):
            qc = q_s[b * t:(b + 1) * t, sl]
            q2 = jnp.concatenate([(qc * lowf).astype(BF16), (qc * highf).astype(BF16)], axis=0)
            scores.append(_dot_nt(q2, mk_ref[b, :, sl].astype(BF16)))
        p = _softmax(jnp.concatenate(scores, axis=0))
        for b in range(nseq):
            o = _dot(p[2 * t * b:2 * t * (b + 1)], mv_ref[b, :, sl].astype(BF16))
            a_s[b * t:(b + 1) * t, sl] = jnp.where(low, o[:t], o[t:])
    o_ref[...] = x + _dot(a_s[...].astype(BF16), wo_ref[...])


def _mem_attn_sample(x, g, wq, mk, mv, wo, *, group):
    nseq, n_mem, width = mk.shape
    t = x.shape[0] // nseq
    rows = group * t
    row = pl.BlockSpec((rows, x.shape[1]), lambda i: (i, 0))
    mem = pl.BlockSpec((group, n_mem, width), lambda i: (i, 0, 0))
    return pl.pallas_call(
        _mem_attn_sample_kernel,
        grid=(nseq // group,),
        in_specs=[row, _full(g.shape), _full(wq.shape), mem, mem, _full(wo.shape)],
        out_specs=row,
        out_shape=jax.ShapeDtypeStruct(x.shape, F32),
        scratch_shapes=[pltpu.VMEM((rows, width), F32), pltpu.VMEM((rows, width), F32)],
        compiler_params=_params("parallel"),
        name="mem_attn_sample",
    )(x, g, wq, mk, mv, wo)


def _top16(s):
    nrows, cols = s.shape
    r = lax.broadcasted_iota(I32, (nrows, cols), 0).astype(F32)
    slot = lax.broadcasted_iota(I32, (PEER_TOPK, cols), 0)
    vals = jnp.zeros((PEER_TOPK, cols), F32)
    idxs = jnp.zeros((PEER_TOPK, cols), F32)
    for p in range(PEER_TOPK):
        m = jnp.max(s, axis=0, keepdims=True)
        am = jnp.min(jnp.where(s == m, r, float(nrows)), axis=0, keepdims=True)
        vals = jnp.where(slot == p, m, vals)
        idxs = jnp.where(slot == p, am, idxs)
        s = jnp.where(r == am, -jnp.inf, s)
    return vals, idxs


def _sort16_pairs():
    def merge(lo, hi, r):
        step = r * 2
        if step < hi - lo:
            yield from merge(lo, hi, step)
            yield from merge(lo + r, hi, step)
            yield from [(i, i + r) for i in range(lo + r, hi - r, step)]
        else:
            yield (lo, lo + r)

    def sort(lo, hi):
        if hi > lo:
            mid = lo + (hi - lo) // 2
            yield from sort(lo, mid)
            yield from sort(mid + 1, hi)
            yield from merge(lo, hi, 1)

    return tuple(sort(0, PEER_TOPK - 1))


SORT16 = _sort16_pairs()


def _cmpx(v, pays, i, j):
    swap = v[j] > v[i]
    hi, lo = jnp.maximum(v[i], v[j]), jnp.minimum(v[i], v[j])
    for p in pays:
        p[i], p[j] = jnp.where(swap, p[j], p[i]), jnp.where(swap, p[i], p[j])
    v[i], v[j] = hi, lo


def _merge16(va, pa, vb, pb):
    n = PEER_TOPK
    v, pays = [], [[] for _ in pa]
    for i in range(n):
        other = vb[n - 1 - i]
        if other is None:
            v.append(va[i])
            for k in range(len(pa)):
                pays[k].append(pa[k][i])
            continue
        take = other > va[i]
        v.append(jnp.maximum(va[i], other))
        for k in range(len(pa)):
            pays[k].append(jnp.where(take, pb[k][n - 1 - i], pa[k][i]))
    d = n // 2
    while d:
        for i in range(n):
            if not i & d:
                _cmpx(v, pays, i, i + d)
        d //= 2
    return v, pays


def _xor_rows(x, d, sub):
    if d == SUBLANES // 2:
        return pltpu.roll(x, d, 0)
    return jnp.where((sub & d) == 0, pltpu.roll(x, SUBLANES - d, 0), pltpu.roll(x, d, 0))


def _row(x, k):
    return jnp.broadcast_to(x[k:k + 1, :], x.shape)


def _route_sorted(s_s, rows, i1_s, i2_s, gt_s):
    n = PEER_TOPK
    sub = lax.broadcasted_iota(I32, (SUBLANES, LANES), 0)
    subf = sub.astype(F32)
    bit2 = (sub & 4) == 0
    bit1 = (sub & 2) == 0
    halves = [slice(0, LANES), slice(LANES, 2 * LANES)]
    bad = jnp.zeros((SUBLANES, LANES), F32)

    def packed_merge(mask, d, xa, xb):
        va, pa = xa
        vb, pb = xb
        lv = [jnp.where(mask, a, b) for a, b in zip(va, vb)]
        rv = [_xor_rows(jnp.where(mask, b, a), d, sub) for a, b in zip(va, vb)]
        lp = [[jnp.where(mask, a, b) for a, b in zip(qa, qb)] for qa, qb in zip(pa, pb)]
        rp = [[_xor_rows(jnp.where(mask, b, a), d, sub) for a, b in zip(qa, qb)] for qa, qb in zip(pa, pb)]
        return _merge16(lv, lp, rv, rp)

    def self_merge(d, x):
        v, p = x
        return _merge16(v, p, [_xor_rows(a, d, sub) for a in v], [[_xor_rows(a, d, sub) for a in q] for q in p])

    def decreasing(v):
        ok = v[0] > v[1]
        flag = jnp.where(ok, 0.0, 1.0)
        for r in range(1, n - 1):
            flag = jnp.where(v[r] > v[r + 1], flag, 1.0)
        return flag

    sorted_lists = {}
    for c in range(2):
        for part in range(2):
            v = [s_s[c, SUBLANES * g:SUBLANES * (g + 1), halves[part]] for g in range(N_KEYS // SUBLANES)]
            ix = [subf + float(SUBLANES * g) for g in range(N_KEYS // SUBLANES)]
            pays = [ix]
            for i, j in SORT16:
                _cmpx(v, pays, i, j)
            sorted_lists[c, part] = (v, pays)
    x1 = [packed_merge(bit2, 4, sorted_lists[c, 0], sorted_lists[c, 1]) for c in range(2)]
    x2 = packed_merge(bit1, 2, x1[0], x1[1])
    tv, (ti,) = self_merge(1, x2)
    bad = jnp.maximum(bad, decreasing(tv))
    for c in range(2):
        for part in range(2):
            thr = _row(tv[n - 1], 4 * part + 2 * c)
            cnt = jnp.zeros((SUBLANES, LANES), F32)
            for g in range(N_KEYS // SUBLANES):
                cnt = cnt + jnp.where(s_s[c, SUBLANES * g:SUBLANES * (g + 1), halves[part]] >= thr, 1.0, 0.0)
            total = jnp.sum(cnt, axis=0, keepdims=True)
            bad = jnp.maximum(bad, jnp.broadcast_to(jnp.where(total == float(n), 0.0, 1.0), bad.shape))

    half = n // 2
    lists, tails = [], []
    for part in range(2):
        base = 4 * part
        s1 = [_row(tv[r], base) for r in range(n)]
        a1 = [_row(ti[r], base) for r in range(n)]
        s2 = [_row(tv[r], base + 2) for r in range(n)]
        a2 = [_row(ti[r], base + 2) for r in range(n)]
        p1, ip1 = s1[0], a1[0]
        for p in range(1, half):
            p1 = jnp.where(sub == p, s1[p], p1)
            ip1 = jnp.where(sub == p, a1[p], ip1)
        lists.append(([p1 + s2[q] for q in range(n)], [[ip1] * n, list(a2)]))
        tails.append(([s1[half + i] + s2[0] for i in range(half)],
                      [[a1[half + i] for i in range(half)], [a2[0]] * half]))
    y = packed_merge(bit2, 4, lists[0], lists[1])
    y = self_merge(2, y)
    y = self_merge(1, y)
    dv = [jnp.where(bit2, a, b) for a, b in zip(tails[0][0], tails[1][0])] + [None] * half
    dp = [[jnp.where(bit2, a, b) for a, b in zip(qa, qb)] + [None] * half
          for qa, qb in zip(tails[0][1], tails[1][1])]
    top, (sel1, sel2) = _merge16(y[0], y[1], dv, dp)
    bad = jnp.maximum(bad, decreasing(top))
    for part in range(2):
        thr = _row(top[n - 1], 4 * part)
        cnt = jnp.zeros((SUBLANES, LANES), F32)
        for q in range(n):
            cnt = cnt + jnp.where(lists[part][0][q] >= thr, 1.0, 0.0)
        total = jnp.sum(cnt, axis=0, keepdims=True)
        for i in range(half):
            total = total + jnp.where(tails[part][0][i][0:1, :] >= thr[0:1, :], 1.0, 0.0)
        bad = jnp.maximum(bad, jnp.broadcast_to(jnp.where(total == float(n), 0.0, 1.0), bad.shape))

    e = [jnp.exp(t - top[0]) for t in top]
    z = e[0]
    for k in range(1, n):
        z = z + e[k]
    gate = [ek / z for ek in e]
    for src, dst in ((sel1, i1_s), (sel2, i2_s), (gate, gt_s)):
        for grp in range(2):
            for part in range(2):
                tile = None
                for k in range(SUBLANES):
                    slab = src[grp * SUBLANES + k]
                    if (k < 4) != (part == 0):
                        slab = pltpu.roll(slab, 4, 0)
                    tile = slab if tile is None else jnp.where(sub == k, slab, tile)
                dst[rows[grp], halves[part]] = tile
    return jnp.max(bad)


def _peer_route_kernel(x_ref, g_ref, w_ref, k_ref,
                       xn_ref, i1_ref, i2_ref, gate_ref, q_s, s_s, i1_s, i2_s, gt_s):
    tb = x_ref.shape[0]
    xn = _rms(x_ref[...], g_ref[...]).astype(BF16)
    xn_ref[...] = xn
    q = _dot(xn, w_ref[...]).astype(BF16)
    for hc in range(2 * PEER_HEADS):
        q_s[hc] = q[:, hc * D_HALF:(hc + 1) * D_HALF]

    group = s_s.shape[0]

    def head_pair(k, carry):
        tied = []
        for hh in range(group):
            h = group * k + hh
            for c in range(2):
                s_s[hh, c] = _dot_nt(k_ref[2 * h + c], q_s[2 * h + c])
            rows = [pl.ds(pl.multiple_of(h * PEER_TOPK + r, SUBLANES), SUBLANES) for r in (0, SUBLANES)]
            tied.append(_route_sorted(s_s.at[hh], rows, i1_s, i2_s, gt_s))

        for hh in range(group):
            @pl.when(tied[hh] > 0.0)
            def _():
                for part in range(tb // LANES):
                    head_part(group * k + hh, hh, part)

        return carry

    def head_part(h, hh, part):
        cols = slice(part * LANES, (part + 1) * LANES)
        (s1, i1), (s2, i2) = [_top16(s_s[hh, c, :, cols]) for c in range(2)]
        tb = LANES
        half = PEER_TOPK // 2
        cand = jnp.concatenate(
            [jnp.broadcast_to(s1[0:1], (PEER_TOPK, tb)) + s2]
            + [jnp.broadcast_to(s1[p:p + 1], (half, tb)) + s2[:half] for p in range(1, half)]
            + [s1[half:] + jnp.broadcast_to(s2[0:1], (half, tb))], axis=0)
        top, cidx = _top16(cand)
        ci = cidx.astype(I32)
        mid = ci - PEER_TOPK
        tail = PEER_TOPK + half * (half - 1)
        pi = jnp.where(ci < PEER_TOPK, 0,
                       jnp.where(ci < tail, 1 + (mid >> (half.bit_length() - 1)), ci - tail + half))
        qi = jnp.where(ci < PEER_TOPK, ci, jnp.where(ci < tail, mid & (half - 1), 0))
        e1 = jnp.zeros((PEER_TOPK, tb), F32)
        e2 = jnp.zeros((PEER_TOPK, tb), F32)
        for p in range(PEER_TOPK):
            e1 = jnp.where(pi == p, jnp.broadcast_to(i1[p:p + 1], (PEER_TOPK, tb)), e1)
            e2 = jnp.where(qi == p, jnp.broadcast_to(i2[p:p + 1], (PEER_TOPK, tb)), e2)
        e = jnp.exp(top - jnp.max(top, axis=0, keepdims=True))
        rs = pl.ds(pl.multiple_of(h * PEER_TOPK, PEER_TOPK), PEER_TOPK)
        i1_s[rs, cols] = e1
        i2_s[rs, cols] = e2
        gt_s[rs, cols] = e / jnp.sum(e, axis=0, keepdims=True)

    lax.fori_loop(0, PEER_HEADS // group, head_pair, 0)
    i1_ref[...] = i1_s[...].T.astype(I32)
    i2_ref[...] = i2_s[...].T.astype(I32)
    gate_ref[...] = gt_s[...].T


def _peer_route(x, g, w, k):
    t, d = x.shape
    tb = 2 * LANES
    assert t % tb == 0
    sel = PEER_HEADS * PEER_TOPK
    row = lambda width: pl.BlockSpec((tb, width), lambda i: (i, 0))
    return pl.pallas_call(
        _peer_route_kernel,
        grid=(t // tb,),
        in_specs=[row(d), _full(g.shape), _full(w.shape), _full(k.shape)],
        out_specs=[row(d), row(sel), row(sel), row(sel)],
        out_shape=[jax.ShapeDtypeStruct((t, d), BF16), jax.ShapeDtypeStruct((t, sel), I32),
                   jax.ShapeDtypeStruct((t, sel), I32), jax.ShapeDtypeStruct((t, sel), F32)],
        scratch_shapes=[pltpu.VMEM((2 * PEER_HEADS, tb, D_HALF), BF16),
                        pltpu.VMEM((ROUTE_HEADS_PER_BODY, 2, N_KEYS, tb), F32),
                        pltpu.VMEM((sel, tb), F32), pltpu.VMEM((sel, tb), F32), pltpu.VMEM((sel, tb), F32)],
        compiler_params=_params("parallel"),
        name="peer_route",
    )(x, g, w, k)


def _peer_gates_kernel(i1_ref, i2_ref, gate_ref, *refs):
    if len(refs) > 2:
        u_ref, v_ref, o_ref, ub_ref, vb_ref, g_s = refs
        ub_ref[...] = u_ref[...].astype(BF16)
        vb_ref[...] = v_ref[...].astype(BF16)
    else:
        o_ref, g_s = refs
    tb = i1_ref.shape[0]
    sel = i1_ref.shape[1]
    key = lax.broadcasted_iota(I32, (N_KEYS, sel), 0)
    tile = 2 * SUBLANES

    def build(tidx):
        w1, w2 = [], []
        for half in range(2):
            t0 = pl.multiple_of(tidx * tile + half * SUBLANES, SUBLANES)
            i1g = i1_ref[pl.ds(t0, SUBLANES), :]
            i2g = i2_ref[pl.ds(t0, SUBLANES), :]
            gtg = gate_ref[pl.ds(t0, SUBLANES), :]
            w1 += [jnp.where(key == i1g[s:s + 1, :], gtg[s:s + 1, :], 0.0).astype(BF16) for s in range(SUBLANES)]
            w2 += [jnp.where(key == i2g[s:s + 1, :], 1.0, 0.0).astype(BF16) for s in range(SUBLANES)]
        for s in range(0, tile, 2):
            word = pltpu.pack_elementwise([_dot_nt(w1[s], w2[s]), _dot_nt(w1[s + 1], w2[s + 1])],
                                          packed_dtype=BF16)
            pair = tidx * SUBLANES + s // 2
            g_s[pl.ds(pl.multiple_of(pair * GATE_PITCH, SUBLANES), N_KEYS), :] = word

    def relayout(tidx):
        rows = pl.ds(pl.multiple_of(tidx * tile, tile), tile)
        for a in range(N_KEYS):
            words = g_s[pl.ds(tidx * SUBLANES * GATE_PITCH + a, SUBLANES, stride=GATE_PITCH), :]
            o_ref[rows, a * N_KEYS:(a + 1) * N_KEYS] = pltpu.bitcast(words, BF16)

    group = GATE_TILES_PER_STEP
    n_groups = tb // (tile * group)
    for u in range(group):
        build(u)

    def step(gidx, carry):
        for u in range(group):
            relayout((gidx - 1) * group + u)
        for u in range(group):
            build(gidx * group + u)
        return carry

    lax.fori_loop(1, n_groups, step, 0)
    for u in range(group):
        relayout((n_groups - 1) * group + u)


def _peer_gates(i1, i2, gate, *, rows, tables=()):
    t, sel = i1.shape
    steps = t // rows
    row = pl.BlockSpec((rows, sel), lambda i: (i, 0))
    in_specs = [row, row, row]
    out_specs = [pl.BlockSpec((rows, N_KEYS * N_KEYS), lambda i: (i, 0))]
    out_shape = [jax.ShapeDtypeStruct((t, N_KEYS * N_KEYS), BF16)]
    for tab in tables:
        n_exp, d = tab.shape
        assert n_exp % steps == 0 and (n_exp // steps) % (2 * SUBLANES) == 0
        spec = pl.BlockSpec((n_exp // steps, d), lambda i: (i, 0))
        in_specs.append(spec)
        out_specs.append(spec)
        out_shape.append(jax.ShapeDtypeStruct(tab.shape, BF16))
    return pl.pallas_call(
        _peer_gates_kernel,
        grid=(steps,),
        in_specs=in_specs,
        out_specs=out_specs,
        out_shape=out_shape,
        scratch_shapes=[pltpu.VMEM((rows // 2 * GATE_PITCH, N_KEYS), jnp.uint32)],
        compiler_params=_params("parallel"),
        name="peer_gates",
    )(i1, i2, gate, *tables)


def _peer_dense_kernel(xn_ref, gates_ref, u_ref, v_ref, x_ref, g_ref, o_ref, acc):
    k = pl.program_id(1)

    @pl.when(k == 0)
    def _():
        acc[...] = jnp.zeros_like(acc)

    h = _dot_nt(xn_ref[...], u_ref[...])
    a = (_gelu(h) * gates_ref[...].astype(F32)).astype(BF16)
    acc[...] += _dot(a, v_ref[...])

    @pl.when(k == pl.num_programs(1) - 1)
    def _():
        o_ref[...] = _rms(x_ref[...] + acc[...], g_ref[...])


def _peer_dense(xn, gates, u, v, x, g, *, rows, experts):
    t, d = x.shape
    n_exp = u.shape[0]
    tok = pl.BlockSpec((rows, d), lambda i, k: (i, 0))
    tab = pl.BlockSpec((experts, d), lambda i, k: (k, 0))
    return pl.pallas_call(
        _peer_dense_kernel,
        grid=(t // rows, n_exp // experts),
        in_specs=[tok, pl.BlockSpec((rows, experts), lambda i, k: (i, k)), tab, tab, tok, _full(g.shape)],
        out_specs=tok,
        out_shape=jax.ShapeDtypeStruct((t, d), F32),
        scratch_shapes=[pltpu.VMEM((rows, d), F32)],
        compiler_params=_params("parallel", "arbitrary"),
        name="peer_dense",
    )(xn, gates, u, v, x, g)


def _peer(x, g_ffn, wq, keys, u, v, g_final):
    t = x.shape[0]
    xn, i1, i2, gate = _peer_route(x, g_ffn, wq, keys)
    tables = (u, v) if u.dtype == F32 else ()
    gates, *rounded = _peer_gates(i1, i2, gate, rows=2 * LANES, tables=tables)
    if rounded:
        u, v = rounded
    return _peer_dense(xn, gates, u, v, x, g_final, rows=min(512, t), experts=2048), u, v


def _head_perm():
    new = np.zeros(ATTN_WIDTH, np.int32)
    for h in range(N_Q_HEADS):
        dst = (h % 4) * LANES + (h // 4) * HEAD_DIM
        new[dst:dst + HEAD_DIM] = np.arange(h * HEAD_DIM, (h + 1) * HEAD_DIM)
    return new


def _rope_tables(pos):
    half = HEAD_DIM // 2
    inv = ROPE_THETA ** (-jnp.arange(half, dtype=F32) / half)
    ang = pos.astype(F32)[:, None] * inv[None, :]
    cos, sin = jnp.cos(ang), jnp.sin(ang)
    reps = LANES // HEAD_DIM
    return (jnp.tile(jnp.concatenate([cos, cos], axis=1), (1, reps)),
            jnp.tile(jnp.concatenate([-sin, sin], axis=1), (1, reps)))


def kernel(x_prompt, x_sample, mem_prompt, cache_swa_k, cache_swa_v, cache_mem_k, cache_mem_v, g_mix, w_in, attn_sinks, g_sgu, w_spatial, b_spatial, g_attn_out, g_gmlp_out, w_out, g_cross, g_mem, w_cq, w_mk, w_mv, w_co, g_ffn, w_peer_q, peer_sub_keys, peer_u, peer_v, g_final):
    batch, seq, d = x_prompt.shape
    dec_batch, dec_seq, _ = x_sample.shape
    depth = g_mix.shape[0]
    n_mem = mem_prompt.shape[1]
    assert seq % WINDOW == 0 and dec_seq == SUBLANES and cache_swa_k.shape[2] == WINDOW

    perm = _head_perm()
    cos_p, sin_p = _rope_tables(jnp.arange(seq, dtype=I32))
    cos_s, sin_s = _rope_tables(PAST_LEN +(jnp.arange(dec_batch * dec_seq, dtype=I32) % dec_seq))
    seg = jnp.asarray(np.kron(np.eye(N_GMLP_HEADS), np.full((HEAD_DIM, HEAD_DIM), 1.0 / HEAD_DIM)), BF16)
    row2 = lambda a: a.reshape(1, -1)

    xp = x_prompt.reshape(batch * seq, d)
    xs = x_sample.reshape(dec_batch * dec_seq, d)
    outs = {name: [] for name in ("kp", "vp", "mk", "mv", "ks", "vs", "gvs")}
    for l in range(depth):
        w_in_l = jnp.concatenate([w_in[l][:, :ATTN_WIDTH][:, perm], w_in[l][:, ATTN_WIDTH:]], axis=1).astype(BF16)
        w_out_l = jnp.concatenate([w_out[l][:ATTN_WIDTH][perm], w_out[l][ATTN_WIDTH:]], axis=0).astype(BF16)
        ga = row2(g_attn_out[l][perm])
        gg = row2(g_gmlp_out[l])
        gs = row2(g_sgu[l])
        bias = jnp.repeat(b_spatial[l].T, HEAD_DIM, axis=1)
        wt = jnp.tril(w_spatial[l][:, :dec_seq, :dec_seq])
        ctab = jnp.repeat(jnp.transpose(wt, (2, 1, 0)), HEAD_DIM, axis=2)
        sink_tab = jnp.broadcast_to(jnp.repeat(attn_sinks[l], dec_seq)[:, None], (N_Q_HEADS * dec_seq, LANES))

        mix = functools.partial(_mix_in, g=row2(g_mix[l]), w_in=w_in_l, gs=gs, seg=seg)
        rows_p = next(r for r in (1024, 512, WINDOW) if seq % r == 0)
        qp, kp, vp, gup, gvp = mix(xp, cosf=cos_p, sinf=sin_p, rows=rows_p, table_blocks=seq // rows_p)
        rows_s = min(512, dec_batch * dec_seq)
        qs, ks, vs, gus, gvs = mix(xs, cosf=cos_s, sinf=sin_s, rows=rows_s,
                                   table_blocks=dec_batch * dec_seq // rows_s)

        xp = _mix_prompt(attn_sinks[l], qp, kp, vp, gup, gvp, xp, w_spatial[l], bias, ga, gg, w_out_l,
                         batch=batch, seq=seq, rows=next(r for r in (512, 256, WINDOW) if seq % r == 0))
        ck = cache_swa_k[l].reshape(dec_batch, WINDOW, N_KV_HEADS * HEAD_DIM)
        cv = cache_swa_v[l].reshape(dec_batch, WINDOW, N_KV_HEADS * HEAD_DIM)
        xs, nks, nvs = _mix_sample(qs, ks, vs, gus, gvs, xs, ck, cv, sink_tab, ctab, bias[:dec_seq], ga, gg,
                                   w_out_l, group=min(16, dec_batch))

        w_kv = jnp.concatenate([w_mk[l], w_mv[l]], axis=1).astype(BF16)
        mk, mv = _mem_kv(mem_prompt.reshape(batch * n_mem, d), row2(g_mem[l]), w_kv, rows=n_mem)
        wq = w_cq[l].astype(BF16)
        wo = w_co[l].astype(BF16)
        gc = row2(g_cross[l])
        xp = _mem_attn_prompt(xp, gc, wq, mk, mv, wo, batch=batch, seq=seq,
                              rows=next(r for r in (1024, 512, 256) if seq % r == 0))
        cmk = cache_mem_k[l].reshape(dec_batch, n_mem, MEM_HEADS * MEM_HEAD_DIM)
        cmv = cache_mem_v[l].reshape(dec_batch, n_mem, MEM_HEADS * MEM_HEAD_DIM)
        xs = _mem_attn_sample(xs, gc, wq, cmk, cmv, wo, group=min(8, dec_batch))

        wpq = w_peer_q[l].astype(BF16)
        keys = peer_sub_keys[l].reshape(2 * PEER_HEADS, N_KEYS, D_HALF).astype(BF16)
        last = l == depth - 1
        assert last, "stacked layers need an un-normalised PEER output"
        peer = functools.partial(_peer, g_ffn=row2(g_ffn[l]), wq=wpq, keys=keys, g_final=row2(g_final))
        xp, u, v = peer(xp, u=peer_u[l], v=peer_v[l])
        xs, _, _ = peer(xs, u=u, v=v)

        outs["kp"].append(kp.reshape(batch, seq, N_KV_HEADS, HEAD_DIM)[:, seq - WINDOW:])
        outs["vp"].append(vp.reshape(batch, seq, N_KV_HEADS, HEAD_DIM)[:, seq - WINDOW:])
        outs["mk"].append(mk.reshape(batch, n_mem, MEM_HEADS, MEM_HEAD_DIM))
        outs["mv"].append(mv.reshape(batch, n_mem, MEM_HEADS, MEM_HEAD_DIM))
        outs["ks"].append(nks.reshape(dec_batch, WINDOW, N_KV_HEADS, HEAD_DIM))
        outs["vs"].append(nvs.reshape(dec_batch, WINDOW, N_KV_HEADS, HEAD_DIM))
        outs["gvs"].append(gvs.reshape(dec_batch, dec_seq, N_GMLP_HEADS, HEAD_DIM))

    stack = lambda name: jnp.stack(outs[name])
    return (xp.reshape(batch, seq, d), xs.reshape(dec_batch, dec_seq, d),
            stack("kp"), stack("vp"), stack("mk"), stack("mv"), stack("ks"), stack("vs"), stack("gvs"))
```

```python
import functools

import jax
import jax.numpy as jnp
import numpy as np
from jax import lax
from jax.experimental import pallas as pl
from jax.experimental.pallas import tpu as pltpu

F32 = jnp.float32
BF16 = jnp.bfloat16
I32 = jnp.int32

LANES = 128
SUBLANES = 8
VMEM_LIMIT_BYTES = 56 * 1024 * 1024

HEAD_DIM = 64
N_Q_HEADS = 8
N_KV_HEADS = 2
ATTN_WIDTH = N_Q_HEADS * HEAD_DIM
WINDOW = 128
PAST_LEN = 8192
ROPE_THETA = 10000.0
N_GMLP_HEADS = 8
GMLP_WIDTH = N_GMLP_HEADS * HEAD_DIM
CHUNK = 128
MEM_HEADS = 4
MEM_HEAD_DIM = 64
PEER_HEADS = 8
N_KEYS = 128
PEER_TOPK = 16
D_HALF = 128
EPS = 1e-6
NEG_INF = -1e30
SQRT_HALF = float(np.sqrt(0.5))

GATE_PITCH = N_KEYS + SUBLANES
ROUTE_HEADS_PER_BODY = 8
GATE_TILES_PER_STEP = 8

NT_DIMS = (((1,), (1,)), ((), ()))


def _params(*semantics):
    return pltpu.CompilerParams(dimension_semantics=semantics, vmem_limit_bytes=VMEM_LIMIT_BYTES)


def _full(shape):
    zeros = (0,) * len(shape)
    return pl.BlockSpec(shape, lambda *_: zeros)


def _rms(x, g):
    return x * lax.rsqrt(jnp.mean(x * x, axis=-1, keepdims=True) + EPS) * g


def _gelu(x):
    return 0.5 * x * (1.0 + lax.erf(x * SQRT_HALF))


def _dot(a, b):
    return jnp.dot(a, b, preferred_element_type=F32)


def _dot_nt(a, b):
    return lax.dot_general(a, b, NT_DIMS, preferred_element_type=F32)


def _split(x):
    hi = x.astype(BF16)
    return hi, (x - hi.astype(F32)).astype(BF16)


def _half_masks():
    lane = lax.broadcasted_iota(I32, (1, LANES), 1)
    low = lane < HEAD_DIM
    return low, low.astype(F32), 1.0 - low.astype(F32)


def _mix_in_kernel(x_ref, g_ref, w_ref, cos_ref, sin_ref, gs_ref, seg_ref,
                   q_ref, k_ref, v_ref, gu_ref, gv_ref):
    xn = _rms(x_ref[...], g_ref[...]).astype(BF16)
    proj = _dot(xn, w_ref[...])
    cosf = cos_ref[...]
    sinf = sin_ref[...]
    lane = lax.broadcasted_iota(I32, (1, LANES), 1)
    first_half = (lane & (HEAD_DIM - 1)) < HEAD_DIM // 2

    def rope(c):
        partner = jnp.where(first_half, pltpu.roll(c, LANES - HEAD_DIM // 2, 1),
                            pltpu.roll(c, HEAD_DIM // 2, 1))
        return c * cosf + partner * sinf

    for c in range(ATTN_WIDTH // LANES):
        sl = slice(c * LANES, (c + 1) * LANES)
        q_ref[:, sl] = rope(proj[:, sl]) * (HEAD_DIM ** -0.5)
    k_ref[...] = rope(proj[:, 512:640])
    v_ref[...] = proj[:, 640:768]
    gu_ref[...] = _gelu(proj[:, 768:1280])
    gv = _gelu(proj[:, 1280:1792])
    hi, lo = _split(gv * gv)
    seg = seg_ref[...]
    ms = _dot(hi, seg) + _dot(lo, seg)
    gv_ref[...] = gv * lax.rsqrt(ms + EPS) * gs_ref[...]


def _mix_in(x, g, w_in, cosf, sinf, gs, seg, *, rows, table_blocks):
    t = x.shape[0]
    d = x.shape[1]
    row = lambda width: pl.BlockSpec((rows, width), lambda i: (i, 0))
    tab = pl.BlockSpec((rows, LANES), lambda i: (i % table_blocks, 0))
    return pl.pallas_call(
        _mix_in_kernel,
        grid=(t // rows,),
        in_specs=[row(d), _full(g.shape), _full(w_in.shape), tab, tab, _full(gs.shape), _full(seg.shape)],
        out_specs=[row(512), row(128), row(128), row(512), row(512)],
        out_shape=[jax.ShapeDtypeStruct((t, 512), F32), jax.ShapeDtypeStruct((t, 128), F32),
                   jax.ShapeDtypeStruct((t, 128), F32), jax.ShapeDtypeStruct((t, 512), F32),
                   jax.ShapeDtypeStruct((t, 512), F32)],
        compiler_params=_params("parallel"),
        name="mix_in",
    )(x, g, w_in, cosf, sinf, gs, seg)


def _stack_heads(q, lowf, highf):
    parts = [(q[:, c * LANES:(c + 1) * LANES] * m).astype(BF16)
             for m in (lowf, highf) for c in range(4)]
    return jnp.concatenate(parts, axis=0)


def _sink_softmax(s, sink):
    m = jnp.maximum(jnp.max(s, axis=-1, keepdims=True), sink)
    p = jnp.exp(s - m)
    den = jnp.sum(p, axis=-1, keepdims=True) + jnp.exp(sink - m)
    return (p / den).astype(BF16)


def _merge_out(attn, sgu, ga, gg, wo, x):
    merged = jnp.concatenate([_rms(attn, ga), _rms(sgu, gg)], axis=1).astype(BF16)
    return x + _dot(merged, wo)


def _mix_prompt_kernel(sinks_ref, q_ref, kp_ref, kc_ref, vp_ref, vc_ref, gu_ref, gv_ref, x_ref,
                       ws_ref, bias_ref, ga_ref, gg_ref, wo_ref, o_ref):
    n = pl.program_id(1)
    low, lowf, highf = _half_masks()
    blk = WINDOW
    nsub = q_ref.shape[0] // blk
    sub = [slice(s * blk, (s + 1) * blk) for s in range(nsub)]
    band = [slice(s * blk, (s + 2) * blk) for s in range(nsub)]
    kall = jnp.concatenate([kp_ref[...], kc_ref[...]], axis=0).astype(BF16)
    vall = jnp.concatenate([vp_ref[...], vc_ref[...]], axis=0).astype(BF16)
    q = q_ref[...]
    i = lax.broadcasted_iota(I32, (blk, 2 * blk), 0)
    j = lax.broadcasted_iota(I32, (blk, 2 * blk), 1)
    in_window = (j > i) & (j <= i + blk)
    first_key = jnp.where(n > 0, 0, blk)
    valid = [in_window & (j >= first_key)] + [in_window] * (nsub - 1)
    scores = [_dot_nt(_stack_heads(q[sub[s]], lowf, highf), kall[band[s]]) for s in range(nsub)]
    probs = [jnp.concatenate(
        [_sink_softmax(jnp.where(valid[s], scores[s][h * blk:(h + 1) * blk], NEG_INF), sinks_ref[h])
         for h in range(N_Q_HEADS)], axis=0) for s in range(nsub)]
    outs = [_dot(probs[s], vall[band[s]]) for s in range(nsub)]
    attn = jnp.concatenate([jnp.concatenate(
        [jnp.where(low, o[c * blk:(c + 1) * blk], o[(4 + c) * blk:(5 + c) * blk]) for c in range(4)], axis=1)
        for o in outs], axis=0)

    gvb = gv_ref[...].astype(BF16)
    r = lax.broadcasted_iota(I32, (CHUNK, CHUNK), 0)
    c_ = lax.broadcasted_iota(I32, (CHUNK, CHUNK), 1)
    tril = r >= c_
    w = [jnp.where(tril, ws_ref[h], 0.0).astype(BF16) for h in range(N_GMLP_HEADS)]
    bias = bias_ref[...]
    mixed = jnp.concatenate([jnp.concatenate(
        [jnp.where(low, _dot(w[2 * c], gvb[sub[s], c * LANES:(c + 1) * LANES]),
                   _dot(w[2 * c + 1], gvb[sub[s], c * LANES:(c + 1) * LANES])) for c in range(4)], axis=1) + bias
        for s in range(nsub)], axis=0)
    sgu = gu_ref[...] * mixed
    o_ref[...] = _merge_out(attn, sgu, ga_ref[...], gg_ref[...], wo_ref[...], x_ref[...])


def _mix_prompt(sinks, q, k, v, gu, gv, x, ws, bias, ga, gg, wo, *, batch, seq, rows):
    nb = seq // rows
    per = rows // WINDOW
    cur = lambda width: pl.BlockSpec((rows, width), lambda b, n: (b * nb + n, 0))
    prev = lambda width: pl.BlockSpec(
        (WINDOW, width), lambda b, n: (b * nb * per + jnp.maximum(n * per - 1, 0), 0))
    return pl.pallas_call(
        _mix_prompt_kernel,
        grid=(batch, nb),
        in_specs=[pl.BlockSpec(memory_space=pltpu.SMEM),
                  cur(512), prev(128), cur(128), prev(128), cur(128), cur(512), cur(512), cur(x.shape[1]),
                  _full(ws.shape), _full(bias.shape), _full(ga.shape), _full(gg.shape), _full(wo.shape)],
        out_specs=cur(x.shape[1]),
        out_shape=jax.ShapeDtypeStruct(x.shape, F32),
        compiler_params=_params("parallel", "parallel"),
        name="mix_prompt",
    )(sinks, q, k, k, v, v, gu, gv, x, ws, bias, ga, gg, wo)


def _mix_sample_kernel(q_ref, kn_ref, vn_ref, gu_ref, gv_ref, x_ref, ck_ref, cv_ref,
                       sink_ref, ctab_ref, bias_ref, ga_ref, gg_ref, wo_ref,
                       o_ref, nk_ref, nv_ref, attn_s, sgu_s):
    nseq = ck_ref.shape[0]
    t = q_ref.shape[0] // nseq
    w = ck_ref.shape[1]
    low, lowf, highf = _half_masks()
    rows = N_Q_HEADS * t
    band = 2 * w
    i = lax.broadcasted_iota(I32, (rows, band), 0) & (t - 1)
    j = lax.broadcasted_iota(I32, (rows, band), 1)
    valid = (j > i) & (j <= i + w)
    sink = sink_ref[:, 0:1]
    pad = jnp.zeros((band - w - t, LANES), F32)

    seq_rows = [slice(b * t, (b + 1) * t) for b in range(nseq)]
    scores = []
    for b, rs in enumerate(seq_rows):
        qs = _stack_heads(q_ref[rs, :], lowf, highf)
        kb = jnp.concatenate([ck_ref[b], kn_ref[rs, :], pad], axis=0).astype(BF16)
        scores.append(jnp.where(valid, _dot_nt(qs, kb), NEG_INF))
    probs = [_sink_softmax(s, sink) for s in scores]
    for b, rs in enumerate(seq_rows):
        vc, vn = cv_ref[b], vn_ref[rs, :]
        o_all = _dot(probs[b], jnp.concatenate([vc, vn, pad], axis=0).astype(BF16))
        for c in range(4):
            attn_s[rs, c * LANES:(c + 1) * LANES] = jnp.where(
                low, o_all[c * t:(c + 1) * t], o_all[(4 + c) * t:(5 + c) * t])
        nk_ref[b] = jnp.concatenate([ck_ref[b, t:, :], kn_ref[rs, :]], axis=0)
        nv_ref[b] = jnp.concatenate([vc[t:], vn], axis=0)
    for b, rs in enumerate(seq_rows):
        gvb = gv_ref[rs, :]
        mixed = ctab_ref[0] * gvb[0:1, :]
        for r in range(1, t):
            mixed = mixed + ctab_ref[r] * gvb[r:r + 1, :]
        sgu_s[rs, :] = gu_ref[rs, :] * (mixed + bias_ref[...])
    o_ref[...] = _merge_out(attn_s[...], sgu_s[...], ga_ref[...], gg_ref[...], wo_ref[...], x_ref[...])


def _mix_sample(q, k, v, gu, gv, x, ck, cv, sink_tab, ctab, bias, ga, gg, wo, *, group):
    nseq, w, _ = ck.shape
    t = x.shape[0] // nseq
    assert t == SUBLANES and w == WINDOW and nseq % group == 0
    rows = group * t
    row = lambda width: pl.BlockSpec((rows, width), lambda i: (i, 0))
    cache = pl.BlockSpec((group, w, LANES), lambda i: (i, 0, 0))
    return pl.pallas_call(
        _mix_sample_kernel,
        grid=(nseq // group,),
        in_specs=[row(512), row(128), row(128), row(512), row(512), row(x.shape[1]), cache, cache,
                  _full(sink_tab.shape), _full(ctab.shape), _full(bias.shape),
                  _full(ga.shape), _full(gg.shape), _full(wo.shape)],
        out_specs=[row(x.shape[1]), cache, cache],
        out_shape=[jax.ShapeDtypeStruct(x.shape, F32), jax.ShapeDtypeStruct(ck.shape, F32),
                   jax.ShapeDtypeStruct(cv.shape, F32)],
        scratch_shapes=[pltpu.VMEM((rows, 512), F32), pltpu.VMEM((rows, 512), F32)],
        compiler_params=_params("parallel"),
        name="mix_sample",
    )(q, k, v, gu, gv, x, ck, cv, sink_tab, ctab, bias, ga, gg, wo)


def _mem_kv_kernel(m_ref, g_ref, w_ref, k_ref, v_ref):
    mn = _rms(m_ref[...], g_ref[...]).astype(BF16)
    kv = _dot(mn, w_ref[...])
    half = kv.shape[1] // 2
    k_ref[...] = kv[:, :half]
    v_ref[...] = kv[:, half:]


def _mem_kv(mem, g, w_kv, *, rows):
    t, d = mem.shape
    width = w_kv.shape[1] // 2
    row = lambda wd: pl.BlockSpec((rows, wd), lambda i: (i, 0))
    return pl.pallas_call(
        _mem_kv_kernel,
        grid=(t // rows,),
        in_specs=[row(d), _full(g.shape), _full(w_kv.shape)],
        out_specs=[row(width), row(width)],
        out_shape=[jax.ShapeDtypeStruct((t, width), F32)] * 2,
        compiler_params=_params("parallel"),
        name="mem_kv",
    )(mem, g, w_kv)


def _softmax(s):
    p = jnp.exp(s - jnp.max(s, axis=-1, keepdims=True))
    return (p / jnp.sum(p, axis=-1, keepdims=True)).astype(BF16)


def _mem_heads(q, mk, mv, low, lowf, highf):
    rows = q.shape[0]
    outs = []
    for c in range(MEM_HEADS // 2):
        sl = slice(c * LANES, (c + 1) * LANES)
        qc = q[:, sl]
        q2 = jnp.concatenate([(qc * lowf).astype(BF16), (qc * highf).astype(BF16)], axis=0)
        o = _dot(_softmax(_dot_nt(q2, mk[:, sl].astype(BF16))), mv[:, sl].astype(BF16))
        outs.append(jnp.where(low, o[:rows], o[rows:]))
    return jnp.concatenate(outs, axis=1)


def _mem_attn_prompt_kernel(x_ref, g_ref, wq_ref, mk_ref, mv_ref, wo_ref, o_ref):
    low, lowf, highf = _half_masks()
    x = x_ref[...]
    q = _dot(_rms(x, g_ref[...]).astype(BF16), wq_ref[...]) * (MEM_HEAD_DIM ** -0.5)
    o = _mem_heads(q, mk_ref[...], mv_ref[...], low, lowf, highf)
    o_ref[...] = x + _dot(o.astype(BF16), wo_ref[...])


def _mem_attn_prompt(x, g, wq, mk, mv, wo, *, batch, seq, rows):
    n_mem = mk.shape[0] // batch
    nb = seq // rows
    row = pl.BlockSpec((rows, x.shape[1]), lambda b, n: (b * nb + n, 0))
    mem = pl.BlockSpec((n_mem, mk.shape[1]), lambda b, n: (b, 0))
    return pl.pallas_call(
        _mem_attn_prompt_kernel,
        grid=(batch, nb),
        in_specs=[row, _full(g.shape), _full(wq.shape), mem, mem, _full(wo.shape)],
        out_specs=row,
        out_shape=jax.ShapeDtypeStruct(x.shape, F32),
        compiler_params=_params("parallel", "parallel"),
        name="mem_attn_prompt",
    )(x, g, wq, mk, mv, wo)


def _mem_attn_sample_kernel(x_ref, g_ref, wq_ref, mk_ref, mv_ref, wo_ref, o_ref, q_s, a_s):
    nseq = mk_ref.shape[0]
    t = x_ref.shape[0] // nseq
    low, lowf, highf = _half_masks()
    x = x_ref[...]
    q_s[...] = _dot(_rms(x, g_ref[...]).astype(BF16), wq_ref[...]) * (MEM_HEAD_DIM ** -0.5)
    for c in range(MEM_HEADS // 2):
        sl = slice(c * LANES, (c + 1) * LANES)
        scores = []
        for b in range(nseq):
            qc = q_s[b * t:(b + 1) * t, sl]
            q2 = jnp.concatenate([(qc * lowf).astype(BF16), (qc * highf).astype(BF16)], axis=0)
            scores.append(_dot_nt(q2, mk_ref[b, :, sl].astype(BF16)))
        p = _softmax(jnp.concatenate(scores, axis=0))
        for b in range(nseq):
            o = _dot(p[2 * t * b:2 * t * (b + 1)], mv_ref[b, :, sl].astype(BF16))
            a_s[b * t:(b + 1) * t, sl] = jnp.where(low, o[:t], o[t:])
    o_ref[...] = x + _dot(a_s[...].astype(BF16), wo_ref[...])


def _mem_attn_sample(x, g, wq, mk, mv, wo, *, group):
    nseq, n_mem, width = mk.shape
    t = x.shape[0] // nseq
    rows = group * t
    row = pl.BlockSpec((rows, x.shape[1]), lambda i: (i, 0))
    mem = pl.BlockSpec((group, n_mem, width), lambda i: (i, 0, 0))
    return pl.pallas_call(
        _mem_attn_sample_kernel,
        grid=(nseq // group,),
        in_specs=[row, _full(g.shape), _full(wq.shape), mem, mem, _full(wo.shape)],
        out_specs=row,
        out_shape=jax.ShapeDtypeStruct(x.shape, F32),
        scratch_shapes=[pltpu.VMEM((rows, width), F32), pltpu.VMEM((rows, width), F32)],
        compiler_params=_params("parallel"),
        name="mem_attn_sample",
    )(x, g, wq, mk, mv, wo)


def _top16(s):
    nrows, cols = s.shape
    r = lax.broadcasted_iota(I32, (nrows, cols), 0).astype(F32)
    slot = lax.broadcasted_iota(I32, (PEER_TOPK, cols), 0)
    vals = jnp.zeros((PEER_TOPK, cols), F32)
    idxs = jnp.zeros((PEER_TOPK, cols), F32)
    for p in range(PEER_TOPK):
        m = jnp.max(s, axis=0, keepdims=True)
        am = jnp.min(jnp.where(s == m, r, float(nrows)), axis=0, keepdims=True)
        vals = jnp.where(slot == p, m, vals)
        idxs = jnp.where(slot == p, am, idxs)
        s = jnp.where(r == am, -jnp.inf, s)
    return vals, idxs


def _sort16_pairs():
    def merge(lo, hi, r):
        step = r * 2
        if step < hi - lo:
            yield from merge(lo, hi, step)
            yield from merge(lo + r, hi, step)
            yield from [(i, i + r) for i in range(lo + r, hi - r, step)]
        else:
            yield (lo, lo + r)

    def sort(lo, hi):
        if hi > lo:
            mid = lo + (hi - lo) // 2
            yield from sort(lo, mid)
            yield from sort(mid + 1, hi)
            yield from merge(lo, hi, 1)

    return tuple(sort(0, PEER_TOPK - 1))


SORT16 = _sort16_pairs()


def _cmpx(v, pays, i, j):
    swap = v[j] > v[i]
    hi, lo = jnp.maximum(v[i], v[j]), jnp.minimum(v[i], v[j])
    for p in pays:
        p[i], p[j] = jnp.where(swap, p[j], p[i]), jnp.where(swap, p[i], p[j])
    v[i], v[j] = hi, lo


def _merge16(va, pa, vb, pb):
    n = PEER_TOPK
    v, pays = [], [[] for _ in pa]
    for i in range(n):
        other = vb[n - 1 - i]
        if other is None:
            v.append(va[i])
            for k in range(len(pa)):
                pays[k].append(pa[k][i])
            continue
        take = other > va[i]
        v.append(jnp.maximum(va[i], other))
        for k in range(len(pa)):
            pays[k].append(jnp.where(take, pb[k][n - 1 - i], pa[k][i]))
    d = n // 2
    while d:
        for i in range(n):
            if not i & d:
                _cmpx(v, pays, i, i + d)
        d //= 2
    return v, pays


def _xor_rows(x, d, sub):
    if d == SUBLANES // 2:
        return pltpu.roll(x, d, 0)
    return jnp.where((sub & d) == 0, pltpu.roll(x, SUBLANES - d, 0), pltpu.roll(x, d, 0))


def _row(x, k):
    return jnp.broadcast_to(x[k:k + 1, :], x.shape)


def _route_sorted(s_s, rows, i1_s, i2_s, gt_s):
    n = PEER_TOPK
    sub = lax.broadcasted_iota(I32, (SUBLANES, LANES), 0)
    subf = sub.astype(F32)
    bit2 = (sub & 4) == 0
    bit1 = (sub & 2) == 0
    halves = [slice(0, LANES), slice(LANES, 2 * LANES)]
    bad = jnp.zeros((SUBLANES, LANES), F32)

    def packed_merge(mask, d, xa, xb):
        va, pa = xa
        vb, pb = xb
        lv = [jnp.where(mask, a, b) for a, b in zip(va, vb)]
        rv = [_xor_rows(jnp.where(mask, b, a), d, sub) for a, b in zip(va, vb)]
        lp = [[jnp.where(mask, a, b) for a, b in zip(qa, qb)] for qa, qb in zip(pa, pb)]
        rp = [[_xor_rows(jnp.where(mask, b, a), d, sub) for a, b in zip(qa, qb)] for qa, qb in zip(pa, pb)]
        return _merge16(lv, lp, rv, rp)

    def self_merge(d, x):
        v, p = x
        return _merge16(v, p, [_xor_rows(a, d, sub) for a in v], [[_xor_rows(a, d, sub) for a in q] for q in p])

    def decreasing(v):
        ok = v[0] > v[1]
        flag = jnp.where(ok, 0.0, 1.0)
        for r in range(1, n - 1):
            flag = jnp.where(v[r] > v[r + 1], flag, 1.0)
        return flag

    sorted_lists = {}
    for c in range(2):
        for part in range(2):
            v = [s_s[c, SUBLANES * g:SUBLANES * (g + 1), halves[part]] for g in range(N_KEYS // SUBLANES)]
            ix = [subf + float(SUBLANES * g) for g in range(N_KEYS // SUBLANES)]
            pays = [ix]
            for i, j in SORT16:
                _cmpx(v, pays, i, j)
            sorted_lists[c, part] = (v, pays)
    x1 = [packed_merge(bit2, 4, sorted_lists[c, 0], sorted_lists[c, 1]) for c in range(2)]
    x2 = packed_merge(bit1, 2, x1[0], x1[1])
    tv, (ti,) = self_merge(1, x2)
    bad = jnp.maximum(bad, decreasing(tv))
    for c in range(2):
        for part in range(2):
            thr = _row(tv[n - 1], 4 * part + 2 * c)
            cnt = jnp.zeros((SUBLANES, LANES), F32)
            for g in range(N_KEYS // SUBLANES):
                cnt = cnt + jnp.where(s_s[c, SUBLANES * g:SUBLANES * (g + 1), halves[part]] >= thr, 1.0, 0.0)
            total = jnp.sum(cnt, axis=0, keepdims=True)
            bad = jnp.maximum(bad, jnp.broadcast_to(jnp.where(total == float(n), 0.0, 1.0), bad.shape))

    half = n // 2
    lists, tails = [], []
    for part in range(2):
        base = 4 * part
        s1 = [_row(tv[r], base) for r in range(n)]
        a1 = [_row(ti[r], base) for r in range(n)]
        s2 = [_row(tv[r], base + 2) for r in range(n)]
        a2 = [_row(ti[r], base + 2) for r in range(n)]
        p1, ip1 = s1[0], a1[0]
        for p in range(1, half):
            p1 = jnp.where(sub == p, s1[p], p1)
            ip1 = jnp.where(sub == p, a1[p], ip1)
        lists.append(([p1 + s2[q] for q in range(n)], [[ip1] * n, list(a2)]))
        tails.append(([s1[half + i] + s2[0] for i in range(half)],
                      [[a1[half + i] for i in range(half)], [a2[0]] * half]))
    y = packed_merge(bit2, 4, lists[0], lists[1])
    y = self_merge(2, y)
    y = self_merge(1, y)
    dv = [jnp.where(bit2, a, b) for a, b in zip(tails[0][0], tails[1][0])] + [None] * half
    dp = [[jnp.where(bit2, a, b) for a, b in zip(qa, qb)] + [None] * half
          for qa, qb in zip(tails[0][1], tails[1][1])]
    top, (sel1, sel2) = _merge16(y[0], y[1], dv, dp)
    bad = jnp.maximum(bad, decreasing(top))
    for part in range(2):
        thr = _row(top[n - 1], 4 * part)
        cnt = jnp.zeros((SUBLANES, LANES), F32)
        for q in range(n):
            cnt = cnt + jnp.where(lists[part][0][q] >= thr, 1.0, 0.0)
        total = jnp.sum(cnt, axis=0, keepdims=True)
        for i in range(half):
            total = total + jnp.where(tails[part][0][i][0:1, :] >= thr[0:1, :], 1.0, 0.0)
        bad = jnp.maximum(bad, jnp.broadcast_to(jnp.where(total == float(n), 0.0, 1.0), bad.shape))

    e = [jnp.exp(t - top[0]) for t in top]
    z = e[0]
    for k in range(1, n):
        z = z + e[k]
    gate = [ek / z for ek in e]
    for src, dst in ((sel1, i1_s), (sel2, i2_s), (gate, gt_s)):
        for grp in range(2):
            for part in range(2):
                tile = None
                for k in range(SUBLANES):
                    slab = src[grp * SUBLANES + k]
                    if (k < 4) != (part == 0):
                        slab = pltpu.roll(slab, 4, 0)
                    tile = slab if tile is None else jnp.where(sub == k, slab, tile)
                dst[rows[grp], halves[part]] = tile
    return jnp.max(bad)


def _peer_route_kernel(x_ref, g_ref, w_ref, k_ref, *refs):
    if len(refs) > 9:
        u_ref, v_ref, xn_ref, i1_ref, i2_ref, gate_ref, ub_ref, vb_ref, q_s, s_s, i1_s, i2_s, gt_s = refs
        ub_ref[...] = u_ref[...].astype(BF16)
        vb_ref[...] = v_ref[...].astype(BF16)
    else:
        xn_ref, i1_ref, i2_ref, gate_ref, q_s, s_s, i1_s, i2_s, gt_s = refs
    tb = x_ref.shape[0]
    xn = _rms(x_ref[...], g_ref[...]).astype(BF16)
    xn_ref[...] = xn
    q = _dot(xn, w_ref[...]).astype(BF16)
    for hc in range(2 * PEER_HEADS):
        q_s[hc] = q[:, hc * D_HALF:(hc + 1) * D_HALF]

    group = s_s.shape[0]

    def head_pair(k, carry):
        tied = []
        for hh in range(group):
            h = group * k + hh
            for c in range(2):
                s_s[hh, c] = _dot_nt(k_ref[2 * h + c], q_s[2 * h + c])
            rows = [pl.ds(pl.multiple_of(h * PEER_TOPK + r, SUBLANES), SUBLANES) for r in (0, SUBLANES)]
            tied.append(_route_sorted(s_s.at[hh], rows, i1_s, i2_s, gt_s))

        for hh in range(group):
            @pl.when(tied[hh] > 0.0)
            def _():
                for part in range(tb // LANES):
                    head_part(group * k + hh, hh, part)

        return carry

    def head_part(h, hh, part):
        cols = slice(part * LANES, (part + 1) * LANES)
        (s1, i1), (s2, i2) = [_top16(s_s[hh, c, :, cols]) for c in range(2)]
        tb = LANES
        half = PEER_TOPK // 2
        cand = jnp.concatenate(
            [jnp.broadcast_to(s1[0:1], (PEER_TOPK, tb)) + s2]
            + [jnp.broadcast_to(s1[p:p + 1], (half, tb)) + s2[:half] for p in range(1, half)]
            + [s1[half:] + jnp.broadcast_to(s2[0:1], (half, tb))], axis=0)
        top, cidx = _top16(cand)
        ci = cidx.astype(I32)
        mid = ci - PEER_TOPK
        tail = PEER_TOPK + half * (half - 1)
        pi = jnp.where(ci < PEER_TOPK, 0,
                       jnp.where(ci < tail, 1 + (mid >> (half.bit_length() - 1)), ci - tail + half))
        qi = jnp.where(ci < PEER_TOPK, ci, jnp.where(ci < tail, mid & (half - 1), 0))
        e1 = jnp.zeros((PEER_TOPK, tb), F32)
        e2 = jnp.zeros((PEER_TOPK, tb), F32)
        for p in range(PEER_TOPK):
            e1 = jnp.where(pi == p, jnp.broadcast_to(i1[p:p + 1], (PEER_TOPK, tb)), e1)
            e2 = jnp.where(qi == p, jnp.broadcast_to(i2[p:p + 1], (PEER_TOPK, tb)), e2)
        e = jnp.exp(top - jnp.max(top, axis=0, keepdims=True))
        rs = pl.ds(pl.multiple_of(h * PEER_TOPK, PEER_TOPK), PEER_TOPK)
        i1_s[rs, cols] = e1
        i2_s[rs, cols] = e2
        gt_s[rs, cols] = e / jnp.sum(e, axis=0, keepdims=True)

    lax.fori_loop(0, PEER_HEADS // group, head_pair, 0)
    i1_ref[...] = i1_s[...].T.astype(I32)
    i2_ref[...] = i2_s[...].T.astype(I32)
    gate_ref[...] = gt_s[...].T


def _peer_route(x, g, w, k, tables=()):
    t, d = x.shape
    tb = 2 * LANES
    assert t % tb == 0
    steps = t // tb
    sel = PEER_HEADS * PEER_TOPK
    row = lambda width: pl.BlockSpec((tb, width), lambda i: (i, 0))
    in_specs = [row(d), _full(g.shape), _full(w.shape), _full(k.shape)]
    out_specs = [row(d), row(sel), row(sel), row(sel)]
    out_shape = [jax.ShapeDtypeStruct((t, d), BF16), jax.ShapeDtypeStruct((t, sel), I32),
                 jax.ShapeDtypeStruct((t, sel), I32), jax.ShapeDtypeStruct((t, sel), F32)]
    for tab in tables:
        n_exp = tab.shape[0]
        assert n_exp % steps == 0 and (n_exp // steps) % (2 * SUBLANES) == 0
        spec = pl.BlockSpec((n_exp // steps, tab.shape[1]), lambda i: (i, 0))
        in_specs.append(spec)
        out_specs.append(spec)
        out_shape.append(jax.ShapeDtypeStruct(tab.shape, BF16))
    return pl.pallas_call(
        _peer_route_kernel,
        grid=(steps,),
        in_specs=in_specs,
        out_specs=out_specs,
        out_shape=out_shape,
        scratch_shapes=[pltpu.VMEM((2 * PEER_HEADS, tb, D_HALF), BF16),
                        pltpu.VMEM((ROUTE_HEADS_PER_BODY, 2, N_KEYS, tb), F32),
                        pltpu.VMEM((sel, tb), F32), pltpu.VMEM((sel, tb), F32), pltpu.VMEM((sel, tb), F32)],
        compiler_params=_params("parallel"),
        name="peer_route",
    )(x, g, w, k, *tables)


def _peer_gates_kernel(i1_ref, i2_ref, gate_ref, o_ref, g_s):
    tb = i1_ref.shape[0]
    sel = i1_ref.shape[1]
    key = lax.broadcasted_iota(I32, (N_KEYS, sel), 0)
    tile = 2 * SUBLANES

    def build(tidx):
        w1, w2 = [], []
        for half in range(2):
            t0 = pl.multiple_of(tidx * tile + half * SUBLANES, SUBLANES)
            i1g = i1_ref[pl.ds(t0, SUBLANES), :]
            i2g = i2_ref[pl.ds(t0, SUBLANES), :]
            gtg = gate_ref[pl.ds(t0, SUBLANES), :]
            w1 += [jnp.where(key == i1g[s:s + 1, :], gtg[s:s + 1, :], 0.0).astype(BF16) for s in range(SUBLANES)]
            w2 += [jnp.where(key == i2g[s:s + 1, :], 1.0, 0.0).astype(BF16) for s in range(SUBLANES)]
        for s in range(0, tile, 2):
            word = pltpu.pack_elementwise([_dot_nt(w1[s], w2[s]), _dot_nt(w1[s + 1], w2[s + 1])],
                                          packed_dtype=BF16)
            pair = tidx * SUBLANES + s // 2
            g_s[pl.ds(pl.multiple_of(pair * GATE_PITCH, SUBLANES), N_KEYS), :] = word

    def relayout(tidx):
        rows = pl.ds(pl.multiple_of(tidx * tile, tile), tile)
        for a in range(N_KEYS):
            words = g_s[pl.ds(tidx * SUBLANES * GATE_PITCH + a, SUBLANES, stride=GATE_PITCH), :]
            o_ref[rows, a * N_KEYS:(a + 1) * N_KEYS] = pltpu.bitcast(words, BF16)

    group = GATE_TILES_PER_STEP
    n_groups = tb // (tile * group)
    for u in range(group):
        build(u)

    def step(gidx, carry):
        for u in range(group):
            relayout((gidx - 1) * group + u)
        for u in range(group):
            build(gidx * group + u)
        return carry

    lax.fori_loop(1, n_groups, step, 0)
    for u in range(group):
        relayout((n_groups - 1) * group + u)


def _peer_gates(i1, i2, gate, *, rows):
    t, sel = i1.shape
    row = pl.BlockSpec((rows, sel), lambda i: (i, 0))
    return pl.pallas_call(
        _peer_gates_kernel,
        grid=(t // rows,),
        in_specs=[row, row, row],
        out_specs=pl.BlockSpec((rows, N_KEYS * N_KEYS), lambda i: (i, 0)),
        out_shape=jax.ShapeDtypeStruct((t, N_KEYS * N_KEYS), BF16),
        scratch_shapes=[pltpu.VMEM((rows // 2 * GATE_PITCH, N_KEYS), jnp.uint32)],
        compiler_params=_params("parallel"),
        name="peer_gates",
    )(i1, i2, gate)


def _peer_dense_kernel(xn_ref, gates_ref, u_ref, v_ref, x_ref, g_ref, o_ref, acc):
    k = pl.program_id(1)

    @pl.when(k == 0)
    def _():
        acc[...] = jnp.zeros_like(acc)

    h = _dot_nt(xn_ref[...], u_ref[...])
    a = (_gelu(h) * gates_ref[...].astype(F32)).astype(BF16)
    acc[...] += _dot(a, v_ref[...])

    @pl.when(k == pl.num_programs(1) - 1)
    def _():
        o_ref[...] = _rms(x_ref[...] + acc[...], g_ref[...])


def _peer_dense(xn, gates, u, v, x, g, *, rows, experts):
    t, d = x.shape
    n_exp = u.shape[0]
    tok = pl.BlockSpec((rows, d), lambda i, k: (i, 0))
    tab = pl.BlockSpec((experts, d), lambda i, k: (k, 0))
    return pl.pallas_call(
        _peer_dense_kernel,
        grid=(t // rows, n_exp // experts),
        in_specs=[tok, pl.BlockSpec((rows, experts), lambda i, k: (i, k)), tab, tab, tok, _full(g.shape)],
        out_specs=tok,
        out_shape=jax.ShapeDtypeStruct((t, d), F32),
        scratch_shapes=[pltpu.VMEM((rows, d), F32)],
        compiler_params=_params("parallel", "arbitrary"),
        name="peer_dense",
    )(xn, gates, u, v, x, g)


def _peer(x, g_ffn, wq, keys, u, v, g_final):
    t = x.shape[0]
    xn, i1, i2, gate, *rounded = _peer_route(x, g_ffn, wq, keys, tables=(u, v) if u.dtype == F32 else ())
    if rounded:
        u, v = rounded
    gates = _peer_gates(i1, i2, gate, rows=2 * LANES)
    return _peer_dense(xn, gates, u, v, x, g_final, rows=min(512, t), experts=2048), u, v


def _head_perm():
    new = np.zeros(ATTN_WIDTH, np.int32)
    for h in range(N_Q_HEADS):
        dst = (h % 4) * LANES + (h // 4) * HEAD_DIM
        new[dst:dst + HEAD_DIM] = np.arange(h * HEAD_DIM, (h + 1) * HEAD_DIM)
    return new


def _rope_tables(pos):
    half = HEAD_DIM // 2
    inv = ROPE_THETA ** (-jnp.arange(half, dtype=F32) / half)
    ang = pos.astype(F32)[:, None] * inv[None, :]
    cos, sin = jnp.cos(ang), jnp.sin(ang)
    reps = LANES // HEAD_DIM
    return (jnp.tile(jnp.concatenate([cos, cos], axis=1), (1, reps)),
            jnp.tile(jnp.concatenate([-sin, sin], axis=1), (1, reps)))


def kernel(x_prompt, x_sample, mem_prompt, cache_swa_k, cache_swa_v, cache_mem_k, cache_mem_v, g_mix, w_in, attn_sinks, g_sgu, w_spatial, b_spatial, g_attn_out, g_gmlp_out, w_out, g_cross, g_mem, w_cq, w_mk, w_mv, w_co, g_ffn, w_peer_q, peer_sub_keys, peer_u, peer_v, g_final):
    batch, seq, d = x_prompt.shape
    dec_batch, dec_seq, _ = x_sample.shape
    depth = g_mix.shape[0]
    n_mem = mem_prompt.shape[1]
    assert seq % WINDOW == 0 and dec_seq == SUBLANES and cache_swa_k.shape[2] == WINDOW

    perm = _head_perm()
    cos_p, sin_p = _rope_tables(jnp.arange(seq, dtype=I32))
    cos_s, sin_s = _rope_tables(PAST_LEN +(jnp.arange(dec_batch * dec_seq, dtype=I32) % dec_seq))
    seg = jnp.asarray(np.kron(np.eye(N_GMLP_HEADS), np.full((HEAD_DIM, HEAD_DIM), 1.0 / HEAD_DIM)), BF16)
    row2 = lambda a: a.reshape(1, -1)

    xp = x_prompt.reshape(batch * seq, d)
    xs = x_sample.reshape(dec_batch * dec_seq, d)
    outs = {name: [] for name in ("kp", "vp", "mk", "mv", "ks", "vs", "gvs")}
    for l in range(depth):
        w_in_l = jnp.concatenate([w_in[l][:, :ATTN_WIDTH][:, perm], w_in[l][:, ATTN_WIDTH:]], axis=1).astype(BF16)
        w_out_l = jnp.concatenate([w_out[l][:ATTN_WIDTH][perm], w_out[l][ATTN_WIDTH:]], axis=0).astype(BF16)
        ga = row2(g_attn_out[l][perm])
        gg = row2(g_gmlp_out[l])
        gs = row2(g_sgu[l])
        bias = jnp.repeat(b_spatial[l].T, HEAD_DIM, axis=1)
        wt = jnp.tril(w_spatial[l][:, :dec_seq, :dec_seq])
        ctab = jnp.repeat(jnp.transpose(wt, (2, 1, 0)), HEAD_DIM, axis=2)
        sink_tab = jnp.broadcast_to(jnp.repeat(attn_sinks[l], dec_seq)[:, None], (N_Q_HEADS * dec_seq, LANES))

        mix = functools.partial(_mix_in, g=row2(g_mix[l]), w_in=w_in_l, gs=gs, seg=seg)
        rows_p = next(r for r in (1024, 512, WINDOW) if seq % r == 0)
        qp, kp, vp, gup, gvp = mix(xp, cosf=cos_p, sinf=sin_p, rows=rows_p, table_blocks=seq // rows_p)
        rows_s = min(512, dec_batch * dec_seq)
        qs, ks, vs, gus, gvs = mix(xs, cosf=cos_s, sinf=sin_s, rows=rows_s,
                                   table_blocks=dec_batch * dec_seq // rows_s)

        xp = _mix_prompt(attn_sinks[l], qp, kp, vp, gup, gvp, xp, w_spatial[l], bias, ga, gg, w_out_l,
                         batch=batch, seq=seq, rows=next(r for r in (512, 256, WINDOW) if seq % r == 0))
        ck = cache_swa_k[l].reshape(dec_batch, WINDOW, N_KV_HEADS * HEAD_DIM)
        cv = cache_swa_v[l].reshape(dec_batch, WINDOW, N_KV_HEADS * HEAD_DIM)
        xs, nks, nvs = _mix_sample(qs, ks, vs, gus, gvs, xs, ck, cv, sink_tab, ctab, bias[:dec_seq], ga, gg,
                                   w_out_l, group=min(16, dec_batch))

        w_kv = jnp.concatenate([w_mk[l], w_mv[l]], axis=1).astype(BF16)
        mk, mv = _mem_kv(mem_prompt.reshape(batch * n_mem, d), row2(g_mem[l]), w_kv, rows=n_mem)
        wq = w_cq[l].astype(BF16)
        wo = w_co[l].astype(BF16)
        gc = row2(g_cross[l])
        xp = _mem_attn_prompt(xp, gc, wq, mk, mv, wo, batch=batch, seq=seq,
                              rows=next(r for r in (1024, 512, 256) if seq % r == 0))
        cmk = cache_mem_k[l].reshape(dec_batch, n_mem, MEM_HEADS * MEM_HEAD_DIM)
        cmv = cache_mem_v[l].reshape(dec_batch, n_mem, MEM_HEADS * MEM_HEAD_DIM)
        xs = _mem_attn_sample(xs, gc, wq, cmk, cmv, wo, group=min(8, dec_batch))

        wpq = w_peer_q[l].astype(BF16)
        keys = peer_sub_keys[l].reshape(2 * PEER_HEADS, N_KEYS, D_HALF).astype(BF16)
        last = l == depth - 1
        assert last, "stacked layers need an un-normalised PEER output"
        peer = functools.partial(_peer, g_ffn=row2(g_ffn[l]), wq=wpq, keys=keys, g_final=row2(g_final))
        xp, u, v = peer(xp, u=peer_u[l], v=peer_v[l])
        xs, _, _ = peer(xs, u=u, v=v)

        outs["kp"].append(kp.reshape(batch, seq, N_KV_HEADS, HEAD_DIM)[:, seq - WINDOW:])
        outs["vp"].append(vp.reshape(batch, seq, N_KV_HEADS, HEAD_DIM)[:, seq - WINDOW:])
        outs["mk"].append(mk.reshape(batch, n_mem, MEM_HEADS, MEM_HEAD_DIM))
        outs["mv"].append(mv.reshape(batch, n_mem, MEM_HEADS, MEM_HEAD_DIM))
        outs["ks"].append(nks.reshape(dec_batch, WINDOW, N_KV_HEADS, HEAD_DIM))
        outs["vs"].append(nvs.reshape(dec_batch, WINDOW, N_KV_HEADS, HEAD_DIM))
        outs["gvs"].append(gvs.reshape(dec_batch, dec_seq, N_GMLP_HEADS, HEAD_DIM))

    stack = lambda name: jnp.stack(outs[name])
    return (xp.reshape(batch, seq, d), xs.reshape(dec_batch, dec_seq, d),
            stack("kp"), stack("vp"), stack("mk"), stack("mv"), stack("ks"), stack("vs"), stack("gvs"))
```

```python
import functools

import jax
import jax.numpy as jnp
import numpy as np
from jax import lax
from jax.experimental import pallas as pl
from jax.experimental.pallas import tpu as pltpu

F32 = jnp.float32
BF16 = jnp.bfloat16
I32 = jnp.int32

LANES = 128
SUBLANES = 8
VMEM_LIMIT_BYTES = 56 * 1024 * 1024

HEAD_DIM = 64
N_Q_HEADS = 8
N_KV_HEADS = 2
ATTN_WIDTH = N_Q_HEADS * HEAD_DIM
WINDOW = 128
PAST_LEN = 8192
ROPE_THETA = 10000.0
N_GMLP_HEADS = 8
GMLP_WIDTH = N_GMLP_HEADS * HEAD_DIM
CHUNK = 128
MEM_HEADS = 4
MEM_HEAD_DIM = 64
PEER_HEADS = 8
N_KEYS = 128
PEER_TOPK = 16
D_HALF = 128
EPS = 1e-6
NEG_INF = -1e30
SQRT_HALF = float(np.sqrt(0.5))

GATE_PITCH = N_KEYS + SUBLANES
ROUTE_HEADS_PER_BODY = 8
GATE_TILES_PER_STEP = 8

NT_DIMS = (((1,), (1,)), ((), ()))


def _params(*semantics):
    return pltpu.CompilerParams(dimension_semantics=semantics, vmem_limit_bytes=VMEM_LIMIT_BYTES)


def _full(shape):
    zeros = (0,) * len(shape)
    return pl.BlockSpec(shape, lambda *_: zeros)


def _rms(x, g):
    return x * lax.rsqrt(jnp.mean(x * x, axis=-1, keepdims=True) + EPS) * g


def _gelu(x):
    return 0.5 * x * (1.0 + lax.erf(x * SQRT_HALF))


def _dot(a, b):
    return jnp.dot(a, b, preferred_element_type=F32)


def _dot_nt(a, b):
    return lax.dot_general(a, b, NT_DIMS, preferred_element_type=F32)


def _split(x):
    hi = x.astype(BF16)
    return hi, (x - hi.astype(F32)).astype(BF16)


def _half_masks():
    lane = lax.broadcasted_iota(I32, (1, LANES), 1)
    low = lane < HEAD_DIM
    return low, low.astype(F32), 1.0 - low.astype(F32)


def _mix_in_kernel(x_ref, g_ref, w_ref, cos_ref, sin_ref, gs_ref, seg_ref,
                   q_ref, k_ref, v_ref, gu_ref, gv_ref):
    xn = _rms(x_ref[...], g_ref[...]).astype(BF16)
    proj = _dot(xn, w_ref[...])
    cosf = cos_ref[...]
    sinf = sin_ref[...]
    lane = lax.broadcasted_iota(I32, (1, LANES), 1)
    first_half = (lane & (HEAD_DIM - 1)) < HEAD_DIM // 2

    def rope(c):
        partner = jnp.where(first_half, pltpu.roll(c, LANES - HEAD_DIM // 2, 1),
                            pltpu.roll(c, HEAD_DIM // 2, 1))
        return c * cosf + partner * sinf

    for c in range(ATTN_WIDTH // LANES):
        sl = slice(c * LANES, (c + 1) * LANES)
        q_ref[:, sl] = rope(proj[:, sl]) * (HEAD_DIM ** -0.5)
    k_ref[...] = rope(proj[:, 512:640])
    v_ref[...] = proj[:, 640:768]
    gu_ref[...] = _gelu(proj[:, 768:1280])
    gv = _gelu(proj[:, 1280:1792])
    hi, lo = _split(gv * gv)
    seg = seg_ref[...]
    ms = _dot(hi, seg) + _dot(lo, seg)
    gv_ref[...] = gv * lax.rsqrt(ms + EPS) * gs_ref[...]


def _mix_in(x, g, w_in, cosf, sinf, gs, seg, *, rows, table_blocks):
    t = x.shape[0]
    d = x.shape[1]
    row = lambda width: pl.BlockSpec((rows, width), lambda i: (i, 0))
    tab = pl.BlockSpec((rows, LANES), lambda i: (i % table_blocks, 0))
    return pl.pallas_call(
        _mix_in_kernel,
        grid=(t // rows,),
        in_specs=[row(d), _full(g.shape), _full(w_in.shape), tab, tab, _full(gs.shape), _full(seg.shape)],
        out_specs=[row(512), row(128), row(128), row(512), row(512)],
        out_shape=[jax.ShapeDtypeStruct((t, 512), F32), jax.ShapeDtypeStruct((t, 128), F32),
                   jax.ShapeDtypeStruct((t, 128), F32), jax.ShapeDtypeStruct((t, 512), F32),
                   jax.ShapeDtypeStruct((t, 512), F32)],
        compiler_params=_params("parallel"),
        name="mix_in",
    )(x, g, w_in, cosf, sinf, gs, seg)


def _stack_heads(q, lowf, highf):
    parts = [(q[:, c * LANES:(c + 1) * LANES] * m).astype(BF16)
             for m in (lowf, highf) for c in range(4)]
    return jnp.concatenate(parts, axis=0)


def _sink_softmax(s, sink):
    m = jnp.maximum(jnp.max(s, axis=-1, keepdims=True), sink)
    p = jnp.exp(s - m)
    den = jnp.sum(p, axis=-1, keepdims=True) + jnp.exp(sink - m)
    return (p / den).astype(BF16)


def _merge_out(attn, sgu, ga, gg, wo, x):
    merged = jnp.concatenate([_rms(attn, ga), _rms(sgu, gg)], axis=1).astype(BF16)
    return x + _dot(merged, wo)


def _mix_prompt_kernel(sinks_ref, q_ref, kp_ref, kc_ref, vp_ref, vc_ref, gu_ref, gv_ref, x_ref,
                       ws_ref, bias_ref, ga_ref, gg_ref, wo_ref, o_ref):
    n = pl.program_id(1)
    low, lowf, highf = _half_masks()
    blk = WINDOW
    nsub = q_ref.shape[0] // blk
    sub = [slice(s * blk, (s + 1) * blk) for s in range(nsub)]
    band = [slice(s * blk, (s + 2) * blk) for s in range(nsub)]
    kall = jnp.concatenate([kp_ref[...], kc_ref[...]], axis=0).astype(BF16)
    vall = jnp.concatenate([vp_ref[...], vc_ref[...]], axis=0).astype(BF16)
    q = q_ref[...]
    i = lax.broadcasted_iota(I32, (blk, 2 * blk), 0)
    j = lax.broadcasted_iota(I32, (blk, 2 * blk), 1)
    in_window = (j > i) & (j <= i + blk)
    first_key = jnp.where(n > 0, 0, blk)
    valid = [in_window & (j >= first_key)] + [in_window] * (nsub - 1)
    scores = [_dot_nt(_stack_heads(q[sub[s]], lowf, highf), kall[band[s]]) for s in range(nsub)]
    probs = [jnp.concatenate(
        [_sink_softmax(jnp.where(valid[s], scores[s][h * blk:(h + 1) * blk], NEG_INF), sinks_ref[h])
         for h in range(N_Q_HEADS)], axis=0) for s in range(nsub)]
    outs = [_dot(probs[s], vall[band[s]]) for s in range(nsub)]
    attn = jnp.concatenate([jnp.concatenate(
        [jnp.where(low, o[c * blk:(c + 1) * blk], o[(4 + c) * blk:(5 + c) * blk]) for c in range(4)], axis=1)
        for o in outs], axis=0)

    gvb = gv_ref[...].astype(BF16)
    r = lax.broadcasted_iota(I32, (CHUNK, CHUNK), 0)
    c_ = lax.broadcasted_iota(I32, (CHUNK, CHUNK), 1)
    tril = r >= c_
    w = [jnp.where(tril, ws_ref[h], 0.0).astype(BF16) for h in range(N_GMLP_HEADS)]
    bias = bias_ref[...]
    mixed = jnp.concatenate([jnp.concatenate(
        [jnp.where(low, _dot(w[2 * c], gvb[sub[s], c * LANES:(c + 1) * LANES]),
                   _dot(w[2 * c + 1], gvb[sub[s], c * LANES:(c + 1) * LANES])) for c in range(4)], axis=1) + bias
        for s in range(nsub)], axis=0)
    sgu = gu_ref[...] * mixed
    o_ref[...] = _merge_out(attn, sgu, ga_ref[...], gg_ref[...], wo_ref[...], x_ref[...])


def _mix_prompt(sinks, q, k, v, gu, gv, x, ws, bias, ga, gg, wo, *, batch, seq, rows):
    nb = seq // rows
    per = rows // WINDOW
    cur = lambda width: pl.BlockSpec((rows, width), lambda b, n: (b * nb + n, 0))
    prev = lambda width: pl.BlockSpec(
        (WINDOW, width), lambda b, n: (b * nb * per + jnp.maximum(n * per - 1, 0), 0))
    return pl.pallas_call(
        _mix_prompt_kernel,
        grid=(batch, nb),
        in_specs=[pl.BlockSpec(memory_space=pltpu.SMEM),
                  cur(512), prev(128), cur(128), prev(128), cur(128), cur(512), cur(512), cur(x.shape[1]),
                  _full(ws.shape), _full(bias.shape), _full(ga.shape), _full(gg.shape), _full(wo.shape)],
        out_specs=cur(x.shape[1]),
        out_shape=jax.ShapeDtypeStruct(x.shape, F32),
        compiler_params=_params("parallel", "parallel"),
        name="mix_prompt",
    )(sinks, q, k, k, v, v, gu, gv, x, ws, bias, ga, gg, wo)


def _mix_sample_kernel(q_ref, kn_ref, vn_ref, gu_ref, gv_ref, x_ref, ck_ref, cv_ref,
                       sink_ref, ctab_ref, bias_ref, ga_ref, gg_ref, wo_ref,
                       o_ref, nk_ref, nv_ref, attn_s, sgu_s):
    nseq = ck_ref.shape[0]
    t = q_ref.shape[0] // nseq
    w = ck_ref.shape[1]
    low, lowf, highf = _half_masks()
    rows = N_Q_HEADS * t
    band = 2 * w
    i = lax.broadcasted_iota(I32, (rows, band), 0) & (t - 1)
    j = lax.broadcasted_iota(I32, (rows, band), 1)
    valid = (j > i) & (j <= i + w)
    sink = sink_ref[:, 0:1]
    pad = jnp.zeros((band - w - t, LANES), F32)

    seq_rows = [slice(b * t, (b + 1) * t) for b in range(nseq)]
    scores = []
    for b, rs in enumerate(seq_rows):
        qs = _stack_heads(q_ref[rs, :], lowf, highf)
        kb = jnp.concatenate([ck_ref[b], kn_ref[rs, :], pad], axis=0).astype(BF16)
        scores.append(jnp.where(valid, _dot_nt(qs, kb), NEG_INF))
    probs = [_sink_softmax(s, sink) for s in scores]
    for b, rs in enumerate(seq_rows):
        vc, vn = cv_ref[b], vn_ref[rs, :]
        o_all = _dot(probs[b], jnp.concatenate([vc, vn, pad], axis=0).astype(BF16))
        for c in range(4):
            attn_s[rs, c * LANES:(c + 1) * LANES] = jnp.where(
                low, o_all[c * t:(c + 1) * t], o_all[(4 + c) * t:(5 + c) * t])
        nk_ref[b] = jnp.concatenate([ck_ref[b, t:, :], kn_ref[rs, :]], axis=0)
        nv_ref[b] = jnp.concatenate([vc[t:], vn], axis=0)
    for b, rs in enumerate(seq_rows):
        gvb = gv_ref[rs, :]
        mixed = ctab_ref[0] * gvb[0:1, :]
        for r in range(1, t):
            mixed = mixed + ctab_ref[r] * gvb[r:r + 1, :]
        sgu_s[rs, :] = gu_ref[rs, :] * (mixed + bias_ref[...])
    o_ref[...] = _merge_out(attn_s[...], sgu_s[...], ga_ref[...], gg_ref[...], wo_ref[...], x_ref[...])


def _mix_sample(q, k, v, gu, gv, x, ck, cv, sink_tab, ctab, bias, ga, gg, wo, *, group):
    nseq, w, _ = ck.shape
    t = x.shape[0] // nseq
    assert t == SUBLANES and w == WINDOW and nseq % group == 0
    rows = group * t
    row = lambda width: pl.BlockSpec((rows, width), lambda i: (i, 0))
    cache = pl.BlockSpec((group, w, LANES), lambda i: (i, 0, 0))
    return pl.pallas_call(
        _mix_sample_kernel,
        grid=(nseq // group,),
        in_specs=[row(512), row(128), row(128), row(512), row(512), row(x.shape[1]), cache, cache,
                  _full(sink_tab.shape), _full(ctab.shape), _full(bias.shape),
                  _full(ga.shape), _full(gg.shape), _full(wo.shape)],
        out_specs=[row(x.shape[1]), cache, cache],
        out_shape=[jax.ShapeDtypeStruct(x.shape, F32), jax.ShapeDtypeStruct(ck.shape, F32),
                   jax.ShapeDtypeStruct(cv.shape, F32)],
        scratch_shapes=[pltpu.VMEM((rows, 512), F32), pltpu.VMEM((rows, 512), F32)],
        compiler_params=_params("parallel"),
        name="mix_sample",
    )(q, k, v, gu, gv, x, ck, cv, sink_tab, ctab, bias, ga, gg, wo)


def _mem_kv_kernel(m_ref, g_ref, w_ref, k_ref, v_ref):
    mn = _rms(m_ref[...], g_ref[...]).astype(BF16)
    kv = _dot(mn, w_ref[...])
    half = kv.shape[1] // 2
    k_ref[...] = kv[:, :half]
    v_ref[...] = kv[:, half:]


def _mem_kv(mem, g, w_kv, *, rows):
    t, d = mem.shape
    width = w_kv.shape[1] // 2
    row = lambda wd: pl.BlockSpec((rows, wd), lambda i: (i, 0))
    return pl.pallas_call(
        _mem_kv_kernel,
        grid=(t // rows,),
        in_specs=[row(d), _full(g.shape), _full(w_kv.shape)],
        out_specs=[row(width), row(width)],
        out_shape=[jax.ShapeDtypeStruct((t, width), F32)] * 2,
        compiler_params=_params("parallel"),
        name="mem_kv",
    )(mem, g, w_kv)


def _softmax(s):
    p = jnp.exp(s - jnp.max(s, axis=-1, keepdims=True))
    return (p / jnp.sum(p, axis=-1, keepdims=True)).astype(BF16)


def _mem_heads(q, mk, mv, low, lowf, highf):
    rows = q.shape[0]
    outs = []
    for c in range(MEM_HEADS // 2):
        sl = slice(c * LANES, (c + 1) * LANES)
        qc = q[:, sl]
        q2 = jnp.concatenate([(qc * lowf).astype(BF16), (qc * highf).astype(BF16)], axis=0)
        o = _dot(_softmax(_dot_nt(q2, mk[:, sl].astype(BF16))), mv[:, sl].astype(BF16))
        outs.append(jnp.where(low, o[:rows], o[rows:]))
    return jnp.concatenate(outs, axis=1)


def _mem_attn_prompt_kernel(x_ref, g_ref, wq_ref, mk_ref, mv_ref, wo_ref, o_ref):
    low, lowf, highf = _half_masks()
    x = x_ref[...]
    q = _dot(_rms(x, g_ref[...]).astype(BF16), wq_ref[...]) * (MEM_HEAD_DIM ** -0.5)
    o = _mem_heads(q, mk_ref[...], mv_ref[...], low, lowf, highf)
    o_ref[...] = x + _dot(o.astype(BF16), wo_ref[...])


def _mem_attn_prompt(x, g, wq, mk, mv, wo, *, batch, seq, rows):
    n_mem = mk.shape[0] // batch
    nb = seq // rows
    row = pl.BlockSpec((rows, x.shape[1]), lambda b, n: (b * nb + n, 0))
    mem = pl.BlockSpec((n_mem, mk.shape[1]), lambda b, n: (b, 0))
    return pl.pallas_call(
        _mem_attn_prompt_kernel,
        grid=(batch, nb),
        in_specs=[row, _full(g.shape), _full(wq.shape), mem, mem, _full(wo.shape)],
        out_specs=row,
        out_shape=jax.ShapeDtypeStruct(x.shape, F32),
        compiler_params=_params("parallel", "parallel"),
        name="mem_attn_prompt",
    )(x, g, wq, mk, mv, wo)


def _mem_attn_sample_kernel(x_ref, g_ref, wq_ref, mk_ref, mv_ref, wo_ref, o_ref, q_s, a_s):
    nseq = mk_ref.shape[0]
    t = x_ref.shape[0] // nseq
    low, lowf, highf = _half_masks()
    x = x_ref[...]
    q_s[...] = _dot(_rms(x, g_ref[...]).astype(BF16), wq_ref[...]) * (MEM_HEAD_DIM ** -0.5)
    for c in range(MEM_HEADS // 2):
        sl = slice(c * LANES, (c + 1) * LANES)
        scores = []
        for b in range(nseq):
            qc = q_s[b * t:(b + 1) * t, sl]
            q2 = jnp.concatenate([(qc * lowf).astype(BF16), (qc * highf).astype(BF16)], axis=0)
            scores.append(_dot_nt(q2, mk_ref[b, :, sl].astype(BF16)))
        p = _softmax(jnp.concatenate(scores, axis=0))
        for b in range(nseq):
            o = _dot(p[2 * t * b:2 * t * (b + 1)], mv_ref[b, :, sl].astype(BF16))
            a_s[b * t:(b + 1) * t, sl] = jnp.where(low, o[:t], o[t:])
    o_ref[...] = x + _dot(a_s[...].astype(BF16), wo_ref[...])


def _mem_attn_sample(x, g, wq, mk, mv, wo, *, group):
    nseq, n_mem, width = mk.shape
    t = x.shape[0] // nseq
    rows = group * t
    row = pl.BlockSpec((rows, x.shape[1]), lambda i: (i, 0))
    mem = pl.BlockSpec((group, n_mem, width), lambda i: (i, 0, 0))
    return pl.pallas_call(
        _mem_attn_sample_kernel,
        grid=(nseq // group,),
        in_specs=[row, _full(g.shape), _full(wq.shape), mem, mem, _full(wo.shape)],
        out_specs=row,
        out_shape=jax.ShapeDtypeStruct(x.shape, F32),
        scratch_shapes=[pltpu.VMEM((rows, width), F32), pltpu.VMEM((rows, width), F32)],
        compiler_params=_params("parallel"),
        name="mem_attn_sample",
    )(x, g, wq, mk, mv, wo)


def _top16(s):
    nrows, cols = s.shape
    r = lax.broadcasted_iota(I32, (nrows, cols), 0).astype(F32)
    slot = lax.broadcasted_iota(I32, (PEER_TOPK, cols), 0)
    vals = jnp.zeros((PEER_TOPK, cols), F32)
    idxs = jnp.zeros((PEER_TOPK, cols), F32)
    for p in range(PEER_TOPK):
        m = jnp.max(s, axis=0, keepdims=True)
        am = jnp.min(jnp.where(s == m, r, float(nrows)), axis=0, keepdims=True)
        vals = jnp.where(slot == p, m, vals)
        idxs = jnp.where(slot == p, am, idxs)
        s = jnp.where(r == am, -jnp.inf, s)
    return vals, idxs


def _sort16_pairs():
    def merge(lo, hi, r):
        step = r * 2
        if step < hi - lo:
            yield from merge(lo, hi, step)
            yield from merge(lo + r, hi, step)
            yield from [(i, i + r) for i in range(lo + r, hi - r, step)]
        else:
            yield (lo, lo + r)

    def sort(lo, hi):
        if hi > lo:
            mid = lo + (hi - lo) // 2
            yield from sort(lo, mid)
            yield from sort(mid + 1, hi)
            yield from merge(lo, hi, 1)

    return tuple(sort(0, PEER_TOPK - 1))


SORT16 = _sort16_pairs()


def _cmpx(v, pays, i, j):
    swap = v[j] > v[i]
    hi, lo = jnp.maximum(v[i], v[j]), jnp.minimum(v[i], v[j])
    for p in pays:
        p[i], p[j] = jnp.where(swap, p[j], p[i]), jnp.where(swap, p[i], p[j])
    v[i], v[j] = hi, lo


def _merge16(va, pa, vb, pb):
    n = PEER_TOPK
    v, pays = [], [[] for _ in pa]
    for i in range(n):
        other = vb[n - 1 - i]
        if other is None:
            v.append(va[i])
            for k in range(len(pa)):
                pays[k].append(pa[k][i])
            continue
        take = other > va[i]
        v.append(jnp.maximum(va[i], other))
        for k in range(len(pa)):
            pays[k].append(jnp.where(take, pb[k][n - 1 - i], pa[k][i]))
    d = n // 2
    while d:
        for i in range(n):
            if not i & d:
                _cmpx(v, pays, i, i + d)
        d //= 2
    return v, pays


def _xor_rows(x, d, sub):
    if d == SUBLANES // 2:
        return pltpu.roll(x, d, 0)
    return jnp.where((sub & d) == 0, pltpu.roll(x, SUBLANES - d, 0), pltpu.roll(x, d, 0))


def _row(x, k):
    return jnp.broadcast_to(x[k:k + 1, :], x.shape)


def _route_sorted(s_s, rows, i1_s, i2_s, gt_s):
    n = PEER_TOPK
    sub = lax.broadcasted_iota(I32, (SUBLANES, LANES), 0)
    subf = sub.astype(F32)
    bit2 = (sub & 4) == 0
    bit1 = (sub & 2) == 0
    halves = [slice(0, LANES), slice(LANES, 2 * LANES)]
    bad = jnp.zeros((SUBLANES, LANES), F32)

    def packed_merge(mask, d, xa, xb):
        va, pa = xa
        vb, pb = xb
        lv = [jnp.where(mask, a, b) for a, b in zip(va, vb)]
        rv = [_xor_rows(jnp.where(mask, b, a), d, sub) for a, b in zip(va, vb)]
        lp = [[jnp.where(mask, a, b) for a, b in zip(qa, qb)] for qa, qb in zip(pa, pb)]
        rp = [[_xor_rows(jnp.where(mask, b, a), d, sub) for a, b in zip(qa, qb)] for qa, qb in zip(pa, pb)]
        return _merge16(lv, lp, rv, rp)

    def self_merge(d, x):
        v, p = x
        return _merge16(v, p, [_xor_rows(a, d, sub) for a in v], [[_xor_rows(a, d, sub) for a in q] for q in p])

    def decreasing(v):
        ok = v[0] > v[1]
        flag = jnp.where(ok, 0.0, 1.0)
        for r in range(1, n - 1):
            flag = jnp.where(v[r] > v[r + 1], flag, 1.0)
        return flag

    sorted_lists = {}
    for c in range(2):
        for part in range(2):
            v = [s_s[c, SUBLANES * g:SUBLANES * (g + 1), halves[part]] for g in range(N_KEYS // SUBLANES)]
            ix = [subf + float(SUBLANES * g) for g in range(N_KEYS // SUBLANES)]
            pays = [ix]
            for i, j in SORT16:
                _cmpx(v, pays, i, j)
            sorted_lists[c, part] = (v, pays)
    x1 = [packed_merge(bit2, 4, sorted_lists[c, 0], sorted_lists[c, 1]) for c in range(2)]
    x2 = packed_merge(bit1, 2, x1[0], x1[1])
    tv, (ti,) = self_merge(1, x2)
    bad = jnp.maximum(bad, decreasing(tv))
    for c in range(2):
        for part in range(2):
            thr = _row(tv[n - 1], 4 * part + 2 * c)
            cnt = jnp.zeros((SUBLANES, LANES), F32)
            for g in range(N_KEYS // SUBLANES):
                cnt = cnt + jnp.where(s_s[c, SUBLANES * g:SUBLANES * (g + 1), halves[part]] >= thr, 1.0, 0.0)
            total = jnp.sum(cnt, axis=0, keepdims=True)
            bad = jnp.maximum(bad, jnp.broadcast_to(jnp.where(total == float(n), 0.0, 1.0), bad.shape))

    half = n // 2
    lists, tails = [], []
    for part in range(2):
        base = 4 * part
        s1 = [_row(tv[r], base) for r in range(n)]
        a1 = [_row(ti[r], base) for r in range(n)]
        s2 = [_row(tv[r], base + 2) for r in range(n)]
        a2 = [_row(ti[r], base + 2) for r in range(n)]
        p1, ip1 = s1[0], a1[0]
        for p in range(1, half):
            p1 = jnp.where(sub == p, s1[p], p1)
            ip1 = jnp.where(sub == p, a1[p], ip1)
        lists.append(([p1 + s2[q] for q in range(n)], [[ip1] * n, list(a2)]))
        tails.append(([s1[half + i] + s2[0] for i in range(half)],
                      [[a1[half + i] for i in range(half)], [a2[0]] * half]))
    y = packed_merge(bit2, 4, lists[0], lists[1])
    y = self_merge(2, y)
    y = self_merge(1, y)
    dv = [jnp.where(bit2, a, b) for a, b in zip(tails[0][0], tails[1][0])] + [None] * half
    dp = [[jnp.where(bit2, a, b) for a, b in zip(qa, qb)] + [None] * half
          for qa, qb in zip(tails[0][1], tails[1][1])]
    top, (sel1, sel2) = _merge16(y[0], y[1], dv, dp)
    bad = jnp.maximum(bad, decreasing(top))
    for part in range(2):
        thr = _row(top[n - 1], 4 * part)
        cnt = jnp.zeros((SUBLANES, LANES), F32)
        for q in range(n):
            cnt = cnt + jnp.where(lists[part][0][q] >= thr, 1.0, 0.0)
        total = jnp.sum(cnt, axis=0, keepdims=True)
        for i in range(half):
            total = total + jnp.where(tails[part][0][i][0:1, :] >= thr[0:1, :], 1.0, 0.0)
        bad = jnp.maximum(bad, jnp.broadcast_to(jnp.where(total == float(n), 0.0, 1.0), bad.shape))

    e = [jnp.exp(t - top[0]) for t in top]
    z = e[0]
    for k in range(1, n):
        z = z + e[k]
    gate = [ek / z for ek in e]
    for src, dst in ((sel1, i1_s), (sel2, i2_s), (gate, gt_s)):
        for grp in range(2):
            for part in range(2):
                tile = None
                for k in range(SUBLANES):
                    slab = src[grp * SUBLANES + k]
                    if (k < 4) != (part == 0):
                        slab = pltpu.roll(slab, 4, 0)
                    tile = slab if tile is None else jnp.where(sub == k, slab, tile)
                dst[rows[grp], halves[part]] = tile
    return jnp.max(bad)


def _peer_route_kernel(x_ref, g_ref, w_ref, k_ref, *refs):
    if len(refs) > 9:
        u_ref, v_ref, xn_ref, i1_ref, i2_ref, gate_ref, ub_ref, vb_ref, q_s, s_s, i1_s, i2_s, gt_s = refs
        ub_ref[...] = u_ref[...].astype(BF16)
        vb_ref[...] = v_ref[...].astype(BF16)
    else:
        xn_ref, i1_ref, i2_ref, gate_ref, q_s, s_s, i1_s, i2_s, gt_s = refs
    tb = x_ref.shape[0]
    xn = _rms(x_ref[...], g_ref[...]).astype(BF16)
    xn_ref[...] = xn
    q = _dot(xn, w_ref[...]).astype(BF16)
    for hc in range(2 * PEER_HEADS):
        q_s[hc] = q[:, hc * D_HALF:(hc + 1) * D_HALF]

    group = s_s.shape[0]

    def head_pair(k, carry):
        tied = []
        for hh in range(group):
            h = group * k + hh
            for c in range(2):
                s_s[hh, c] = _dot_nt(k_ref[2 * h + c], q_s[2 * h + c])
            rows = [pl.ds(pl.multiple_of(h * PEER_TOPK + r, SUBLANES), SUBLANES) for r in (0, SUBLANES)]
            tied.append(_route_sorted(s_s.at[hh], rows, i1_s, i2_s, gt_s))

        for hh in range(group):
            @pl.when(tied[hh] > 0.0)
            def _():
                for part in range(tb // LANES):
                    head_part(group * k + hh, hh, part)

        return carry

    def head_part(h, hh, part):
        cols = slice(part * LANES, (part + 1) * LANES)
        (s1, i1), (s2, i2) = [_top16(s_s[hh, c, :, cols]) for c in range(2)]
        tb = LANES
        half = PEER_TOPK // 2
        cand = jnp.concatenate(
            [jnp.broadcast_to(s1[0:1], (PEER_TOPK, tb)) + s2]
            + [jnp.broadcast_to(s1[p:p + 1], (half, tb)) + s2[:half] for p in range(1, half)]
            + [s1[half:] + jnp.broadcast_to(s2[0:1], (half, tb))], axis=0)
        top, cidx = _top16(cand)
        ci = cidx.astype(I32)
        mid = ci - PEER_TOPK
        tail = PEER_TOPK + half * (half - 1)
        pi = jnp.where(ci < PEER_TOPK, 0,
                       jnp.where(ci < tail, 1 + (mid >> (half.bit_length() - 1)), ci - tail + half))
        qi = jnp.where(ci < PEER_TOPK, ci, jnp.where(ci < tail, mid & (half - 1), 0))
        e1 = jnp.zeros((PEER_TOPK, tb), F32)
        e2 = jnp.zeros((PEER_TOPK, tb), F32)
        for p in range(PEER_TOPK):
            e1 = jnp.where(pi == p, jnp.broadcast_to(i1[p:p + 1], (PEER_TOPK, tb)), e1)
            e2 = jnp.where(qi == p, jnp.broadcast_to(i2[p:p + 1], (PEER_TOPK, tb)), e2)
        e = jnp.exp(top - jnp.max(top, axis=0, keepdims=True))
        rs = pl.ds(pl.multiple_of(h * PEER_TOPK, PEER_TOPK), PEER_TOPK)
        i1_s[rs, cols] = e1
        i2_s[rs, cols] = e2
        gt_s[rs, cols] = e / jnp.sum(e, axis=0, keepdims=True)

    lax.fori_loop(0, PEER_HEADS // group, head_pair, 0)
    i1_ref[...] = i1_s[...].T.astype(I32)
    i2_ref[...] = i2_s[...].T.astype(I32)
    gate_ref[...] = gt_s[...].T


def _peer_route(x, g, w, k, tables=()):
    t, d = x.shape
    tb = 2 * LANES
    assert t % tb == 0
    steps = t // tb
    sel = PEER_HEADS * PEER_TOPK
    row = lambda width: pl.BlockSpec((tb, width), lambda i: (i, 0))
    in_specs = [row(d), _full(g.shape), _full(w.shape), _full(k.shape)]
    out_specs = [row(d), row(sel), row(sel), row(sel)]
    out_shape = [jax.ShapeDtypeStruct((t, d), BF16), jax.ShapeDtypeStruct((t, sel), I32),
                 jax.ShapeDtypeStruct((t, sel), I32), jax.ShapeDtypeStruct((t, sel), F32)]
    for tab in tables:
        n_exp = tab.shape[0]
        assert n_exp % steps == 0 and (n_exp // steps) % (2 * SUBLANES) == 0
        spec = pl.BlockSpec((n_exp // steps, tab.shape[1]), lambda i: (i, 0))
        in_specs.append(spec)
        out_specs.append(spec)
        out_shape.append(jax.ShapeDtypeStruct(tab.shape, BF16))
    return pl.pallas_call(
        _peer_route_kernel,
        grid=(steps,),
        in_specs=in_specs,
        out_specs=out_specs,
        out_shape=out_shape,
        scratch_shapes=[pltpu.VMEM((2 * PEER_HEADS, tb, D_HALF), BF16),
                        pltpu.VMEM((ROUTE_HEADS_PER_BODY, 2, N_KEYS, tb), F32),
                        pltpu.VMEM((sel, tb), F32), pltpu.VMEM((sel, tb), F32), pltpu.VMEM((sel, tb), F32)],
        compiler_params=_params("parallel"),
        name="peer_route",
    )(x, g, w, k, *tables)


def _peer_gates_kernel(i1_ref, i2_ref, gate_ref, o_ref, g_s):
    tb = i1_ref.shape[0]
    sel = i1_ref.shape[1]
    key = lax.broadcasted_iota(I32, (N_KEYS, sel), 0)
    tile = 2 * SUBLANES

    def build(tidx):
        w1, w2 = [], []
        for half in range(2):
            t0 = pl.multiple_of(tidx * tile + half * SUBLANES, SUBLANES)
            i1g = i1_ref[pl.ds(t0, SUBLANES), :]
            i2g = i2_ref[pl.ds(t0, SUBLANES), :]
            gtg = gate_ref[pl.ds(t0, SUBLANES), :]
            w1 += [jnp.where(key == i1g[s:s + 1, :], gtg[s:s + 1, :], 0.0).astype(BF16) for s in range(SUBLANES)]
            w2 += [jnp.where(key == i2g[s:s + 1, :], 1.0, 0.0).astype(BF16) for s in range(SUBLANES)]
        zero = jnp.zeros((N_KEYS, sel), BF16)
        for s in range(0, tile, 2):
            lhs = jnp.concatenate([w1[s], w1[s + 1]], axis=1)
            rhs = jnp.concatenate([jnp.concatenate([w2[s], zero], axis=1),
                                   jnp.concatenate([zero, w2[s + 1]], axis=1)], axis=0)
            g = _dot_nt(lhs, rhs)
            word = pltpu.pack_elementwise([g[:, :N_KEYS], g[:, N_KEYS:]], packed_dtype=BF16)
            pair = tidx * SUBLANES + s // 2
            g_s[pl.ds(pl.multiple_of(pair * GATE_PITCH, SUBLANES), N_KEYS), :] = word

    def relayout(tidx):
        rows = pl.ds(pl.multiple_of(tidx * tile, tile), tile)
        for a in range(N_KEYS):
            words = g_s[pl.ds(tidx * SUBLANES * GATE_PITCH + a, SUBLANES, stride=GATE_PITCH), :]
            o_ref[rows, a * N_KEYS:(a + 1) * N_KEYS] = pltpu.bitcast(words, BF16)

    group = GATE_TILES_PER_STEP
    n_groups = tb // (tile * group)
    for u in range(group):
        build(u)

    def step(gidx, carry):
        for u in range(group):
            relayout((gidx - 1) * group + u)
        for u in range(group):
            build(gidx * group + u)
        return carry

    lax.fori_loop(1, n_groups, step, 0)
    for u in range(group):
        relayout((n_groups - 1) * group + u)


def _peer_gates(i1, i2, gate, *, rows):
    t, sel = i1.shape
    row = pl.BlockSpec((rows, sel), lambda i: (i, 0))
    return pl.pallas_call(
        _peer_gates_kernel,
        grid=(t // rows,),
        in_specs=[row, row, row],
        out_specs=pl.BlockSpec((rows, N_KEYS * N_KEYS), lambda i: (i, 0)),
        out_shape=jax.ShapeDtypeStruct((t, N_KEYS * N_KEYS), BF16),
        scratch_shapes=[pltpu.VMEM((rows // 2 * GATE_PITCH, N_KEYS), jnp.uint32)],
        compiler_params=_params("parallel"),
        name="peer_gates",
    )(i1, i2, gate)


def _peer_dense_kernel(xn_ref, gates_ref, u_ref, v_ref, x_ref, g_ref, o_ref, acc):
    k = pl.program_id(1)

    @pl.when(k == 0)
    def _():
        acc[...] = jnp.zeros_like(acc)

    h = _dot_nt(xn_ref[...], u_ref[...])
    a = (_gelu(h) * gates_ref[...].astype(F32)).astype(BF16)
    acc[...] += _dot(a, v_ref[...])

    @pl.when(k == pl.num_programs(1) - 1)
    def _():
        o_ref[...] = _rms(x_ref[...] + acc[...], g_ref[...])


def _peer_dense(xn, gates, u, v, x, g, *, rows, experts):
    t, d = x.shape
    n_exp = u.shape[0]
    tok = pl.BlockSpec((rows, d), lambda i, k: (i, 0))
    tab = pl.BlockSpec((experts, d), lambda i, k: (k, 0))
    return pl.pallas_call(
        _peer_dense_kernel,
        grid=(t // rows, n_exp // experts),
        in_specs=[tok, pl.BlockSpec((rows, experts), lambda i, k: (i, k)), tab, tab, tok, _full(g.shape)],
        out_specs=tok,
        out_shape=jax.ShapeDtypeStruct((t, d), F32),
        scratch_shapes=[pltpu.VMEM((rows, d), F32)],
        compiler_params=_params("parallel", "arbitrary"),
        name="peer_dense",
    )(xn, gates, u, v, x, g)


def _peer(x, g_ffn, wq, keys, u, v, g_final):
    t = x.shape[0]
    xn, i1, i2, gate, *rounded = _peer_route(x, g_ffn, wq, keys, tables=(u, v) if u.dtype == F32 else ())
    if rounded:
        u, v = rounded
    gates = _peer_gates(i1, i2, gate, rows=2 * LANES)
    return _peer_dense(xn, gates, u, v, x, g_final, rows=min(512, t), experts=2048), u, v


def _head_perm():
    new = np.zeros(ATTN_WIDTH, np.int32)
    for h in range(N_Q_HEADS):
        dst = (h % 4) * LANES + (h // 4) * HEAD_DIM
        new[dst:dst + HEAD_DIM] = np.arange(h * HEAD_DIM, (h + 1) * HEAD_DIM)
    return new


def _rope_tables(pos):
    half = HEAD_DIM // 2
    inv = ROPE_THETA ** (-jnp.arange(half, dtype=F32) / half)
    ang = pos.astype(F32)[:, None] * inv[None, :]
    cos, sin = jnp.cos(ang), jnp.sin(ang)
    reps = LANES // HEAD_DIM
    return (jnp.tile(jnp.concatenate([cos, cos], axis=1), (1, reps)),
            jnp.tile(jnp.concatenate([-sin, sin], axis=1), (1, reps)))


def kernel(x_prompt, x_sample, mem_prompt, cache_swa_k, cache_swa_v, cache_mem_k, cache_mem_v, g_mix, w_in, attn_sinks, g_sgu, w_spatial, b_spatial, g_attn_out, g_gmlp_out, w_out, g_cross, g_mem, w_cq, w_mk, w_mv, w_co, g_ffn, w_peer_q, peer_sub_keys, peer_u, peer_v, g_final):
    batch, seq, d = x_prompt.shape
    dec_batch, dec_seq, _ = x_sample.shape
    depth = g_mix.shape[0]
    n_mem = mem_prompt.shape[1]
    assert seq % WINDOW == 0 and dec_seq == SUBLANES and cache_swa_k.shape[2] == WINDOW

    perm = _head_perm()
    cos_p, sin_p = _rope_tables(jnp.arange(seq, dtype=I32))
    cos_s, sin_s = _rope_tables(PAST_LEN +(jnp.arange(dec_batch * dec_seq, dtype=I32) % dec_seq))
    seg = jnp.asarray(np.kron(np.eye(N_GMLP_HEADS), np.full((HEAD_DIM, HEAD_DIM), 1.0 / HEAD_DIM)), BF16)
    row2 = lambda a: a.reshape(1, -1)

    xp = x_prompt.reshape(batch * seq, d)
    xs = x_sample.reshape(dec_batch * dec_seq, d)
    outs = {name: [] for name in ("kp", "vp", "mk", "mv", "ks", "vs", "gvs")}
    for l in range(depth):
        w_in_l = jnp.concatenate([w_in[l][:, :ATTN_WIDTH][:, perm], w_in[l][:, ATTN_WIDTH:]], axis=1).astype(BF16)
        w_out_l = jnp.concatenate([w_out[l][:ATTN_WIDTH][perm], w_out[l][ATTN_WIDTH:]], axis=0).astype(BF16)
        ga = row2(g_attn_out[l][perm])
        gg = row2(g_gmlp_out[l])
        gs = row2(g_sgu[l])
        bias = jnp.repeat(b_spatial[l].T, HEAD_DIM, axis=1)
        wt = jnp.tril(w_spatial[l][:, :dec_seq, :dec_seq])
        ctab = jnp.repeat(jnp.transpose(wt, (2, 1, 0)), HEAD_DIM, axis=2)
        sink_tab = jnp.broadcast_to(jnp.repeat(attn_sinks[l], dec_seq)[:, None], (N_Q_HEADS * dec_seq, LANES))

        mix = functools.partial(_mix_in, g=row2(g_mix[l]), w_in=w_in_l, gs=gs, seg=seg)
        rows_p = next(r for r in (1024, 512, WINDOW) if seq % r == 0)
        qp, kp, vp, gup, gvp = mix(xp, cosf=cos_p, sinf=sin_p, rows=rows_p, table_blocks=seq // rows_p)
        rows_s = min(512, dec_batch * dec_seq)
        qs, ks, vs, gus, gvs = mix(xs, cosf=cos_s, sinf=sin_s, rows=rows_s,
                                   table_blocks=dec_batch * dec_seq // rows_s)

        xp = _mix_prompt(attn_sinks[l], qp, kp, vp, gup, gvp, xp, w_spatial[l], bias, ga, gg, w_out_l,
                         batch=batch, seq=seq, rows=next(r for r in (512, 256, WINDOW) if seq % r == 0))
        ck = cache_swa_k[l].reshape(dec_batch, WINDOW, N_KV_HEADS * HEAD_DIM)
        cv = cache_swa_v[l].reshape(dec_batch, WINDOW, N_KV_HEADS * HEAD_DIM)
        xs, nks, nvs = _mix_sample(qs, ks, vs, gus, gvs, xs, ck, cv, sink_tab, ctab, bias[:dec_seq], ga, gg,
                                   w_out_l, group=min(16, dec_batch))

        w_kv = jnp.concatenate([w_mk[l], w_mv[l]], axis=1).astype(BF16)
        mk, mv = _mem_kv(mem_prompt.reshape(batch * n_mem, d), row2(g_mem[l]), w_kv, rows=n_mem)
        wq = w_cq[l].astype(BF16)
        wo = w_co[l].astype(BF16)
        gc = row2(g_cross[l])
        xp = _mem_attn_prompt(xp, gc, wq, mk, mv, wo, batch=batch, seq=seq,
                              rows=next(r for r in (1024, 512, 256) if seq % r == 0))
        cmk = cache_mem_k[l].reshape(dec_batch, n_mem, MEM_HEADS * MEM_HEAD_DIM)
        cmv = cache_mem_v[l].reshape(dec_batch, n_mem, MEM_HEADS * MEM_HEAD_DIM)
        xs = _mem_attn_sample(xs, gc, wq, cmk, cmv, wo, group=min(8, dec_batch))

        wpq = w_peer_q[l].astype(BF16)
        keys = peer_sub_keys[l].reshape(2 * PEER_HEADS, N_KEYS, D_HALF).astype(BF16)
        last = l == depth - 1
        assert last, "stacked layers need an un-normalised PEER output"
        peer = functools.partial(_peer, g_ffn=row2(g_ffn[l]), wq=wpq, keys=keys, g_final=row2(g_final))
        xp, u, v = peer(xp, u=peer_u[l], v=peer_v[l])
        xs, _, _ = peer(xs, u=u, v=v)

        outs["kp"].append(kp.reshape(batch, seq, N_KV_HEADS, HEAD_DIM)[:, seq - WINDOW:])
        outs["vp"].append(vp.reshape(batch, seq, N_KV_HEADS, HEAD_DIM)[:, seq - WINDOW:])
        outs["mk"].append(mk.reshape(batch, n_mem, MEM_HEADS, MEM_HEAD_DIM))
        outs["mv"].append(mv.reshape(batch, n_mem, MEM_HEADS, MEM_HEAD_DIM))
        outs["ks"].append(nks.reshape(dec_batch, WINDOW, N_KV_HEADS, HEAD_DIM))
        outs["vs"].append(nvs.reshape(dec_batch, WINDOW, N_KV_HEADS, HEAD_DIM))
        outs["gvs"].append(gvs.reshape(dec_batch, dec_seq, N_GMLP_HEADS, HEAD_DIM))

    stack = lambda name: jnp.stack(outs[name])
    return (xp.reshape(batch, seq, d), xs.reshape(dec_batch, dec_seq, d),
            stack("kp"), stack("vp"), stack("mk"), stack("mv"), stack("ks"), stack("vs"), stack("gvs"))
```
